```python
import functools
import jax
import jax.numpy as jnp
from jax import lax
import numpy as np

D_MODEL = 1024
BATCH = 32
SEQ = 256
DEPTH = 2
DEC_BATCH = 2
DEC_SEQ = 4096
PAST_LEN = 512

GRID_W = 64
HGRN_HEADS = 4
HGRN_DK = 128
HGRN_DV = 128
HGRN_WIDTH = HGRN_HEADS * HGRN_DK
CHUNK = 32
GATE_CLIP = 30.0
POOL_WINDOWS = (2, 4, 8, 16)
POOL_GROUPS = 4
POOL_GROUP_DIM = 128
POOL_WIDTH = POOL_GROUPS * POOL_GROUP_DIM
NA_HEADS = 8
NA_HEAD_DIM = 64
NA_WIDTH = NA_HEADS * NA_HEAD_DIM
WIN_ROWS = 8
WIN_COLS = 16
Q_BLOCK_COLS = 16
K_BLOCK_COLS = Q_BLOCK_COLS + WIN_COLS
Q_BLOCK = 128
N_BRANCH = 3
IN_COLS = 5 * HGRN_WIDTH + POOL_WIDTH + 3 * NA_WIDTH + N_BRANCH * D_MODEL
F_DENSE = 2816
N_EXPERTS = 8
TOP_K = 2
F_EXPERT = 3584
N_DENSE = (DEPTH + 1) // 2
N_MOE = DEPTH // 2
EPS = 1e-6
NEG_BIG = -1e30

kernel_name = "hybrid_diffusion_ctx_prefix_step"


def split_in(z):
    sizes = (HGRN_WIDTH,) * 5 + (POOL_WIDTH,) + (NA_WIDTH,) * 3 + (D_MODEL,) * N_BRANCH
    idx = np.cumsum(np.array(sizes))[:-1].tolist()
    return jnp.split(z, idx, axis=-1)


def rms_norm(x, w):
    xf = x.astype(jnp.float32)
    y = xf * lax.rsqrt(jnp.mean(xf * xf, axis=-1, keepdims=True) + EPS)
    return (y * w.astype(jnp.float32)).astype(x.dtype)


def adaln(cvec, w_ada, b_ada):
    m = jax.nn.silu(cvec) @ w_ada + b_ada
    return jnp.split(m[..., None, :], 6, axis=-1)


def modulate(h, shift, scale):
    return h * (1.0 + scale) + shift


def swiglu(h, w_in, w_out):
    a, b = jnp.split(h @ w_in, 2, axis=-1)
    return (jax.nn.silu(a) * b) @ w_out


def moe_ffn(h, router, w_in, w_out):
    logits = (h @ router).astype(jnp.float32)
    top_v, top_i = lax.top_k(logits, TOP_K)
    w = jax.nn.softmax(top_v, axis=-1)
    gates = jnp.sum(jax.nn.one_hot(top_i, N_EXPERTS, dtype=jnp.float32) * w[..., None], axis=-2).astype(h.dtype)
    out = jnp.zeros_like(h)
    for e in range(N_EXPERTS):
        out = out + gates[..., e:e + 1] * swiglu(h, w_in[e], w_out[e])
    return out


def hgrn_lower_bounds(lb_param):
    sm = jax.nn.softmax(lb_param.astype(jnp.float32), axis=0)
    return jnp.cumsum(sm, axis=0) - sm[0:1]


def hgrn_gates(fx, lb):
    fx = jnp.clip(fx.astype(jnp.float32), -GATE_CLIP, GATE_CLIP)
    f = lb + (1.0 - lb) * jax.nn.sigmoid(fx)
    return jnp.log(f), (1.0 - lb) * jax.nn.sigmoid(-fx)


def hgrn_chunk_scan(q, k, v, log_f, s0):
    B, T, H, _ = q.shape
    nc = T // CHUNK

    def chunks(a):
        return a.astype(jnp.float32).reshape(B, nc, CHUNK, H, a.shape[-1]).transpose(1, 0, 3, 2, 4)

    tri = jnp.tril(jnp.ones((CHUNK, CHUNK), dtype=bool))[:, :, None]

    def step(S, xs):
        qc, kc, vc, lfc = xs
        b = jnp.cumsum(lfc, axis=2)
        diff = b[:, :, :, None, :] - b[:, :, None, :, :]
        decay = jnp.where(tri, jnp.exp(jnp.where(tri, diff, 0.0)), 0.0)
        a = jnp.einsum('bhtd,bhsd,bhtsd->bhts', qc, kc, decay)
        o = jnp.einsum('bhts,bhsv->bhtv', a, vc) + jnp.einsum('bhtd,bhdv->bhtv', qc * jnp.exp(b), S)
        b_end = b[:, :, -1:, :]
        S = jnp.exp(b_end[:, :, 0, :, None]) * S + jnp.einsum('bhsd,bhsv->bhdv', kc * jnp.exp(b_end - b), vc)
        return S, o

    S, o = lax.scan(step, s0.astype(jnp.float32), (chunks(q), chunks(k), chunks(v), chunks(log_f)))
    o = o.transpose(1, 0, 3, 2, 4).reshape(B, T, H, v.shape[-1])
    return o.astype(v.dtype), S.astype(s0.dtype)


def hgrn_mixer(a_q, a_ff, a_fb, a_i, a_g, lb_f, lb_b, norm_w, s_f0, s_b0):
    B, T, _ = a_q.shape

    def heads(t):
        return t.reshape(B, T, HGRN_HEADS, -1)

    def flip(t):
        return jnp.flip(t, axis=1)

    qh = heads(jax.nn.silu(a_q))
    vh = heads(a_i)
    lf_f, k_f = hgrn_gates(a_ff, lb_f)
    lf_b, k_b = hgrn_gates(a_fb, lb_b)
    o_f, s_f = hgrn_chunk_scan(qh, heads(k_f), vh, heads(lf_f), s_f0)
    o_b, s_b = hgrn_chunk_scan(flip(qh), flip(heads(k_b)), flip(vh), flip(heads(lf_b)), s_b0)
    o = o_f + flip(o_b)
    o = rms_norm(o, norm_w).reshape(B, T, HGRN_WIDTH) * jax.nn.silu(a_g)
    return o, s_f, s_b


def pool_mixer(u, w_groups, scale):
    B, T, _ = u.shape
    ug = u.reshape(B, T, POOL_GROUPS, POOL_GROUP_DIM)
    cs = jnp.concatenate([jnp.zeros((B, 1, POOL_GROUPS, POOL_GROUP_DIM), jnp.float32),
                          jnp.cumsum(ug.astype(jnp.float32), axis=1)], axis=1)
    t = jnp.arange(T)
    pooled = []
    for gi, w in enumerate(POOL_WINDOWS):
        lo = jnp.clip(t - w // 2, 0, T)
        hi = jnp.clip(t + w // 2, 0, T)
        pooled.append((cs[:, hi, gi] - cs[:, lo, gi]) / (hi - lo).astype(jnp.float32)[None, :, None])
    pooled = jnp.stack(pooled, axis=2)
    d = (pooled - ug.astype(jnp.float32)).astype(u.dtype)
    y = jnp.einsum('btgc,gcd->btgd', d, w_groups).reshape(B, T, POOL_WIDTH)
    return y * scale


def context_attention(q, k, v):
    B, T, H, hd = q.shape
    qb = q.reshape(B, T // Q_BLOCK, Q_BLOCK, H, hd).transpose(1, 0, 2, 3, 4)

    def block(qi):
        s = jnp.einsum('bqhd,bkhd->bhqk', qi, k).astype(jnp.float32) * (hd ** -0.5)
        p = jax.nn.softmax(s, axis=-1).astype(v.dtype)
        return jnp.einsum('bhqk,bkhd->bqhd', p, v)

    o = lax.map(block, qb)
    return o.transpose(1, 0, 2, 3, 4).reshape(B, T, H * hd)


def neighbourhood_attention(q, k, v, k_ctx, v_ctx, rpb):
    B, N, H, hd = q.shape
    rows = N // GRID_W
    wr = min(WIN_ROWS, rows)
    ncb = GRID_W // Q_BLOCK_COLS
    scale = hd ** -0.5
    r = jnp.arange(rows)
    row_start = jnp.clip(r - wr // 2, 0, rows - wr)
    row_idx = row_start[:, None] + jnp.arange(wr)
    cb = jnp.arange(ncb)
    kcol_start = jnp.clip(cb * Q_BLOCK_COLS - WIN_COLS // 2, 0, GRID_W - K_BLOCK_COLS)
    col_idx = kcol_start[:, None] + jnp.arange(K_BLOCK_COLS)
    qcol = cb[:, None] * Q_BLOCK_COLS + jnp.arange(Q_BLOCK_COLS)
    qcol_start = jnp.clip(qcol - WIN_COLS // 2, 0, GRID_W - WIN_COLS)
    kcol = col_idx[:, None, :]
    col_in = (kcol >= qcol_start[..., None]) & (kcol < qcol_start[..., None] + WIN_COLS)
    dcol = jnp.clip(kcol - qcol[..., None], -(WIN_COLS - 1), WIN_COLS - 1)
    drow = row_idx - r[:, None]
    gidx_r = row_idx[:, None, :, None]
    gidx_c = col_idx[None, :, None, :]
    kg = k.reshape(B, rows, GRID_W, H, hd)[:, gidx_r, gidx_c]
    vg = v.reshape(B, rows, GRID_W, H, hd)[:, gidx_r, gidx_c]
    qb = q.reshape(B, rows, ncb, Q_BLOCK_COLS, H, hd)
    bias = rpb[:, drow[:, None, None, :, None] + (WIN_ROWS - 1), dcol[None, :, :, None, :] + (WIN_COLS - 1)]
    s_loc = jnp.einsum('brnqhd,brnwkhd->bhrnqwk', qb, kg).astype(jnp.float32) * scale + bias[None].astype(jnp.float32)
    s_loc = jnp.where(col_in[None, None, None, :, :, None, :], s_loc, NEG_BIG)
    s_loc = s_loc.reshape(B, H, rows, ncb, Q_BLOCK_COLS, wr * K_BLOCK_COLS)
    s_ctx = jnp.einsum('brnqhd,blhd->bhrnql', qb, k_ctx).astype(jnp.float32) * scale
    p = jax.nn.softmax(jnp.concatenate([s_loc, s_ctx], axis=-1), axis=-1)
    p_loc = p[..., :wr * K_BLOCK_COLS].reshape(B, H, rows, ncb, Q_BLOCK_COLS, wr, K_BLOCK_COLS).astype(v.dtype)
    p_ctx = p[..., wr * K_BLOCK_COLS:].astype(v.dtype)
    o = jnp.einsum('bhrnqwk,brnwkhd->brnqhd', p_loc, vg) + jnp.einsum('bhrnql,blhd->brnqhd', p_ctx, v_ctx)
    return o.reshape(B, N, H * hd)


def trunk_layer(x, cvec, l, attn_fn, s_f0, s_b0, p):
    B, T, _ = x.shape
    sh1, sc1, g1, sh2, sc2, g2 = adaln(cvec, p['w_ada'][l], p['b_ada'][l])
    h = modulate(rms_norm(x, p['norm_mix'][l]), sh1, sc1)
    (a_q, a_ff, a_fb, a_i, a_g, b_u, c_q, c_k, c_v, gl_a, gl_b, gl_c) = split_in(h @ p['w_in'][l])
    lb = hgrn_lower_bounds(p['hgrn_lb'])[l]
    o_a, s_f, s_b = hgrn_mixer(a_q, a_ff, a_fb, a_i, a_g, lb[0], lb[1], p['hgrn_norm'][l], s_f0, s_b0)
    o_b = pool_mixer(b_u, p['pool_w'][l], p['pool_scale'][l])
    kc = c_k.reshape(B, T, NA_HEADS, NA_HEAD_DIM)
    vc = c_v.reshape(B, T, NA_HEADS, NA_HEAD_DIM)
    o_c = attn_fn(c_q.reshape(B, T, NA_HEADS, NA_HEAD_DIM), kc, vc)
    mix = (jax.nn.sigmoid(gl_a) * (o_a @ p['w_branch_a'][l])
           + jax.nn.sigmoid(gl_b) * (o_b @ p['w_branch_b'][l])
           + jax.nn.sigmoid(gl_c) * (o_c @ p['w_branch_c'][l]))
    x = x + g1 * (mix @ p['w_out'][l])
    h2 = modulate(rms_norm(x, p['norm_ffn'][l]), sh2, sc2)
    if l % 2 == 0:
        f = swiglu(h2, p['ffn_w_in'][l // 2], p['ffn_w_out'][l // 2])
    else:
        f = moe_ffn(h2, p['router'][l // 2], p['moe_w_in'][l // 2], p['moe_w_out'][l // 2])
    x = x + g2 * f
    return x, kc, vc, s_f, s_b


def setup_inputs(seed: int = 0) -> dict:
    key = jax.random.key(seed)
    ks = jax.random.split(key, 27)
    d = D_MODEL

    def nrm(k, shape, scale=1.0):
        return jax.random.normal(k, shape, jnp.float32) * scale

    return {
        'x_prompt': nrm(ks[0], (BATCH, SEQ, d)),
        'x_sample': nrm(ks[1], (DEC_BATCH, DEC_SEQ, d)),
        'cache_k': nrm(ks[2], (DEC_BATCH, DEPTH, PAST_LEN, NA_HEADS, NA_HEAD_DIM)),
        'cache_v': nrm(ks[3], (DEC_BATCH, DEPTH, PAST_LEN, NA_HEADS, NA_HEAD_DIM)),
        'state_hgrn': nrm(ks[4], (DEC_BATCH, DEPTH, 2, HGRN_HEADS, HGRN_DK, HGRN_DV), 0.5),
        'c': nrm(ks[5], (DEC_BATCH, d)),
        'c_ctx': nrm(ks[6], (d,)),
        'w_ada': nrm(ks[7], (DEPTH, d, 6 * d), 0.5 * d ** -0.5),
        'b_ada': nrm(ks[8], (DEPTH, 6 * d), 0.02),
        'norm_mix': 1.0 + nrm(ks[9], (DEPTH, d), 0.05),
        'norm_ffn': 1.0 + nrm(ks[10], (DEPTH, d), 0.05),
        'w_in': nrm(ks[11], (DEPTH, d, IN_COLS), d ** -0.5),
        'hgrn_lb': nrm(ks[12], (DEPTH, 2, HGRN_WIDTH)),
        'hgrn_norm': 1.0 + nrm(ks[13], (DEPTH, HGRN_DV), 0.05),
        'pool_w': nrm(ks[14], (DEPTH, POOL_GROUPS, POOL_GROUP_DIM, POOL_GROUP_DIM), POOL_GROUP_DIM ** -0.5),
        'pool_scale': 1.0 + nrm(ks[15], (DEPTH, POOL_WIDTH), 0.1),
        'rpb': nrm(ks[16], (DEPTH, NA_HEADS, 2 * WIN_ROWS - 1, 2 * WIN_COLS - 1), 0.5),
        'w_branch_a': nrm(ks[17], (DEPTH, HGRN_WIDTH, d), HGRN_WIDTH ** -0.5),
        'w_branch_b': nrm(ks[18], (DEPTH, POOL_WIDTH, d), POOL_WIDTH ** -0.5),
        'w_branch_c': nrm(ks[19], (DEPTH, NA_WIDTH, d), NA_WIDTH ** -0.5),
        'w_out': nrm(ks[20], (DEPTH, d, d), d ** -0.5),
        'ffn_w_in': nrm(ks[21], (N_DENSE, d, 2 * F_DENSE), d ** -0.5),
        'ffn_w_out': nrm(ks[22], (N_DENSE, F_DENSE, d), F_DENSE ** -0.5),
        'router': nrm(ks[23], (N_MOE, d, N_EXPERTS), d ** -0.5),
        'moe_w_in': nrm(ks[24], (N_MOE, N_EXPERTS, d, 2 * F_EXPERT), d ** -0.5),
        'moe_w_out': nrm(ks[25], (N_MOE, N_EXPERTS, F_EXPERT, d), F_EXPERT ** -0.5),
        'norm_final': 1.0 + nrm(ks[26], (d,), 0.05),
    }


def reference(x_prompt, x_sample, cache_k, cache_v, state_hgrn, c, c_ctx, w_ada, b_ada, norm_mix, norm_ffn,
              w_in, hgrn_lb, hgrn_norm, pool_w, pool_scale, rpb, w_branch_a, w_branch_b, w_branch_c, w_out,
              ffn_w_in, ffn_w_out, router, moe_w_in, moe_w_out, norm_final):
    p = {
        'w_ada': w_ada, 'b_ada': b_ada, 'norm_mix': norm_mix, 'norm_ffn': norm_ffn, 'w_in': w_in,
        'hgrn_lb': hgrn_lb, 'hgrn_norm': hgrn_norm, 'pool_w': pool_w, 'pool_scale': pool_scale,
        'w_branch_a': w_branch_a, 'w_branch_b': w_branch_b, 'w_branch_c': w_branch_c, 'w_out': w_out,
        'ffn_w_in': ffn_w_in, 'ffn_w_out': ffn_w_out, 'router': router, 'moe_w_in': moe_w_in,
        'moe_w_out': moe_w_out,
    }
    xp = x_prompt
    zero_state = jnp.zeros((x_prompt.shape[0], HGRN_HEADS, HGRN_DK, HGRN_DV), x_prompt.dtype)
    ks, vs, ss = [], [], []
    for l in range(DEPTH):
        xp, kc, vc, s_f, s_b = trunk_layer(xp, c_ctx, l, context_attention, zero_state, zero_state, p)
        ks.append(kc)
        vs.append(vc)
        ss.append(jnp.stack([s_f, s_b], axis=1))
    y_prompt = rms_norm(xp, norm_final)
    new_cache_k = jnp.stack(ks, axis=1)
    new_cache_v = jnp.stack(vs, axis=1)
    new_state_hgrn = jnp.stack(ss, axis=1)
    xs = x_sample
    for l in range(DEPTH):
        attn = functools.partial(neighbourhood_attention, k_ctx=cache_k[:, l], v_ctx=cache_v[:, l], rpb=rpb[l])
        xs, _, _, _, _ = trunk_layer(xs, c, l, attn, state_hgrn[:, l, 0], state_hgrn[:, l, 1], p)
    y_sample = rms_norm(xs, norm_final)
    return (y_prompt, y_sample, new_cache_k, new_cache_v, new_state_hgrn)
```

```python
import functools

import numpy as np
import jax
import jax.numpy as jnp
from jax import lax
from jax.experimental import pallas as pl
from jax.experimental.pallas import tpu as pltpu

F32 = jnp.float32
BF16 = jnp.bfloat16

D = 1024
N_CTX_B, CTX_T = 32, 256
N_LAT_B, LAT_T = 2, 4096
N_CTX = N_CTX_B * CTX_T
N_LAT = N_LAT_B * LAT_T
N_TOK = N_CTX + N_LAT
DEPTH = 2
GRID_W = 64
GRID_ROWS = LAT_T // GRID_W
PAST = 512
HG_H, HG_D = 4, 128
WIDTH = 512
NA_H, NA_D = 8, 64
WIN_ROWS, WIN_COLS = 8, 16
POOL_WINDOWS = (2, 4, 8, 16)
IN_COLS = 7680
F_DENSE = 2816
N_EXP = 8
F_EXP = 3584
GATE_CLIP = 30.0
EPS = 1e-6
NEG_BIG = -1e30

COL_AQ, COL_AFF, COL_AFB, COL_AI, COL_AG, COL_BU, COL_CQ, COL_CK, COL_CV = range(9)
GL_COL0 = 9 * WIDTH

VMEM_LIMIT = 56 * 1024 * 1024


def _cparams(sem):
    return pltpu.CompilerParams(dimension_semantics=sem, vmem_limit_bytes=VMEM_LIMIT)


def _sigmoid(x):
    return 1.0 / (1.0 + jnp.exp(-x))


def _silu(x):
    return x / (1.0 + jnp.exp(-x))


def _group_of_rows(row0):
    return jnp.maximum(row0 - N_CTX + LAT_T, 0) // LAT_T


def _adaln_kernel(c_ref, w_ref, b_ref, o_ref):
    s = _silu(c_ref[...]).astype(BF16)
    o_ref[...] = jnp.dot(s, w_ref[...].astype(BF16), preferred_element_type=F32) + b_ref[...]


def _adaln(cond8, w_ada, b_ada):
    tn = 1536
    return pl.pallas_call(
        _adaln_kernel,
        grid=(DEPTH, 6 * D // tn),
        in_specs=[pl.BlockSpec((8, D), lambda l, j: (0, 0)),
                  pl.BlockSpec((None, D, tn), lambda l, j: (l, 0, j)),
                  pl.BlockSpec((None, 1, tn), lambda l, j: (l, 0, j))],
        out_specs=pl.BlockSpec((None, 8, tn), lambda l, j: (l, 0, j)),
        out_shape=jax.ShapeDtypeStruct((DEPTH, 8, 6 * D), F32),
        compiler_params=_cparams(("parallel", "parallel")),
        name="adaln",
    )(cond8, w_ada, b_ada.reshape(DEPTH, 1, 6 * D))


def _inproj_kernel(x_ref, nw_ref, m_ref, w_ref, z_ref, h_scr):
    @pl.when(pl.program_id(1) == 0)
    def _():
        x = x_ref[...]
        y = x * lax.rsqrt(jnp.mean(x * x, axis=-1, keepdims=True) + EPS) * nw_ref[...]
        h_scr[...] = (y * (1.0 + m_ref[1:2, :]) + m_ref[0:1, :]).astype(BF16)

    z_ref[...] = jnp.dot(h_scr[...], w_ref[...].astype(BF16), preferred_element_type=F32)


def _inproj(x, norm_w, mods, w_in, layer):
    bm, bn = 1024, 1536
    return pl.pallas_call(
        _inproj_kernel,
        grid=(N_TOK // bm, IN_COLS // bn),
        in_specs=[pl.BlockSpec((bm, D), lambda i, j: (i, 0)),
                  pl.BlockSpec((1, D), lambda i, j: (0, 0)),
                  pl.BlockSpec((None, 6, D), lambda i, j: (_group_of_rows(i * bm), 0, 0)),
                  pl.BlockSpec((None, D, bn), lambda i, j: (layer, 0, j))],
        out_specs=pl.BlockSpec((bm, bn), lambda i, j: (i, j)),
        out_shape=jax.ShapeDtypeStruct((N_TOK, IN_COLS), F32),
        scratch_shapes=[pltpu.VMEM((bm, D), BF16)],
        compiler_params=_cparams(("parallel", "arbitrary")),
        name="inproj",
    )(x, norm_w.reshape(1, D), mods, w_in)


HG_C = 128
HG_LEVELS = (4, 8, 16, 32, 64, 128)


def _hgrn_level_ids(reverse):
    t = np.arange(HG_C)[:, None]
    s = np.arange(HG_C)[None, :]
    if reverse:
        t, s = s, t
    lev = np.full((HG_C, HG_C), -1, np.int32)
    lev[(t // 4 == s // 4) & (s <= t)] = 0
    for li, L in enumerate(HG_LEVELS[1:], start=1):
        m = (t // L == s // L) & (t % L >= L // 2) & (s % L < L // 2)
        lev[m] = li
    return lev


def _hgrn_ref_rows(b_scr, d, reverse):
    out = []
    r_lo, r_hi = (2, 6) if reverse else (1, 5)
    sub = lax.broadcasted_iota(jnp.int32, (8, WIDTH), 0)
    pieces = []
    for g in range(HG_C // 8):
        lo = jnp.broadcast_to(b_scr[d, 8 * g + r_lo:8 * g + r_lo + 1, :], (8, WIDTH))
        hi = jnp.broadcast_to(b_scr[d, 8 * g + r_hi:8 * g + r_hi + 1, :], (8, WIDTH))
        pieces.append(jnp.where(sub < 4, lo, hi))
    out.append(jnp.concatenate(pieces, axis=0))
    for L in HG_LEVELS[1:]:
        r = L // 2 - 1 if reverse else L // 2
        pieces = [jnp.broadcast_to(b_scr[d, L * g + r:L * g + r + 1, :], (L, WIDTH))
                  for g in range(HG_C // L)]
        out.append(pieces[0] if len(pieces) == 1 else jnp.concatenate(pieces, axis=0))
    return out


def _hgrn_kernel(*refs, has_init):
    if has_init:
        (qf_ref, ff_ref, vf_ref, qb_ref, fb_ref, vb_ref, lb_ref, levf_ref, levb_ref, s0_ref,
         _of_prev, _ob_prev, of_ref, ob_ref, so_ref, st_scr, b_scr) = refs
    else:
        (qf_ref, ff_ref, vf_ref, qb_ref, fb_ref, vb_ref, lb_ref, levf_ref, levb_ref,
         of_ref, ob_ref, so_ref, st_scr, b_scr) = refs
    c = pl.program_id(1)
    C = HG_C

    @pl.when(c == 0)
    def _():
        for d in range(2):
            for h in range(HG_H):
                if has_init:
                    st_scr[d, h] = s0_ref[d, h].T
                else:
                    st_scr[d, h] = jnp.zeros((HG_D, HG_D), F32)

    row = lax.broadcasted_iota(jnp.int32, (C, C), 0)
    col = lax.broadcasted_iota(jnp.int32, (C, C), 1)
    dirs = ((qf_ref, ff_ref, vf_ref, levf_ref, of_ref), (qb_ref, fb_ref, vb_ref, levb_ref, ob_ref))
    for d, (q_ref, f_ref, v_ref, lev_ref, o_ref) in enumerate(dirs):
        reverse = d == 1
        tri = jnp.where((col >= row) if reverse else (col <= row), 1.0, 0.0).astype(BF16)
        q = _silu(q_ref[...])
        fx = jnp.clip(f_ref[...], -GATE_CLIP, GATE_CLIP)
        e = jnp.exp(-fx)
        sig_pos = 1.0 / (1.0 + e)
        sig_neg = e * sig_pos
        lb = lb_ref[d:d + 1, :]
        lf = jnp.log(lb + (1.0 - lb) * sig_pos)
        k = (1.0 - lb) * sig_neg
        hi = lf.astype(BF16)
        r1 = lf - hi.astype(F32)
        mid = r1.astype(BF16)
        lo = (r1 - mid.astype(F32)).astype(BF16)
        b = (jnp.dot(tri, hi, preferred_element_type=F32)
             + jnp.dot(tri, mid, preferred_element_type=F32)
             + jnp.dot(tri, lo, preferred_element_type=F32))
        b_scr[d] = b
        refs_m = _hgrn_ref_rows(b_scr, d, reverse)
        qs, ks = [], []
        for li, m in enumerate(refs_m):
            dlt = b - m
            cap = 80.0 if li == 0 else 0.0
            qs.append((q * jnp.exp(jnp.minimum(dlt, cap))).astype(BF16))
            ks.append((k * jnp.exp(jnp.minimum(-dlt, cap))).astype(BF16))
        lev = lev_ref[...]
        b_end = b[0:1, :] if reverse else b[C - 1:C, :]
        q_in = (q * jnp.exp(b)).astype(BF16)
        k_out = (k * jnp.exp(b_end - b)).astype(BF16)
        dec = jnp.exp(b_end)
        vb16 = v_ref[...].astype(BF16)
        for h in range(HG_H):
            sl = slice(h * HG_D, (h + 1) * HG_D)
            a = jnp.zeros((C, C), F32)
            for li in range(len(HG_LEVELS)):
                p = lax.dot_general(qs[li][:, sl], ks[li][:, sl], (((1,), (1,)), ((), ())),
                                    preferred_element_type=F32)
                a = jnp.where(lev == li, p, a)
            vh = vb16[:, sl]
            st = st_scr[d, h]
            o = (jnp.dot(a.astype(BF16), vh, preferred_element_type=F32)
                 + lax.dot_general(q_in[:, sl], st.astype(BF16), (((1,), (1,)), ((), ())),
                                   preferred_element_type=F32))
            o_ref[:, sl] = o
            upd = lax.dot_general(vh, k_out[:, sl], (((0,), (0,)), ((), ())),
                                  preferred_element_type=F32)
            st_scr[d, h] = st * dec[:, sl] + upd

    @pl.when(c == pl.num_programs(1) - 1)
    def _():
        for d in range(2):
            for h in range(HG_H):
                so_ref[d, h] = st_scr[d, h].T


def _hgrn(z, lb, s0, prev, row_off, n_b, seq_t):
    C = HG_C
    nc = seq_t // C
    base = row_off // C
    has_init = s0 is not None

    def fwd(col):
        return pl.BlockSpec((C, WIDTH), lambda b, c: (base + b * nc + c, col))

    def bwd(col):
        return pl.BlockSpec((C, WIDTH), lambda b, c: (base + b * nc + nc - 1 - c, col))

    full = lambda shape: pl.BlockSpec(shape, lambda b, c: (0,) * len(shape))
    in_specs = [fwd(COL_AQ), fwd(COL_AFF), fwd(COL_AI), bwd(COL_AQ), bwd(COL_AFB), bwd(COL_AI),
                full((2, WIDTH)), full((C, C)), full((C, C))]
    args = [z, z, z, z, z, z, lb, jnp.asarray(_hgrn_level_ids(False)), jnp.asarray(_hgrn_level_ids(True))]
    aliases = {}
    if has_init:
        in_specs.append(pl.BlockSpec((None, 2, HG_H, HG_D, HG_D), lambda b, c: (b, 0, 0, 0, 0)))
        args.append(s0)
    if prev is not None:
        in_specs += [pl.BlockSpec(memory_space=pl.ANY)] * 2
        aliases = {len(args): 0, len(args) + 1: 1}
        args += list(prev)
    kern = functools.partial(_hgrn_kernel, has_init=has_init)
    if has_init != (prev is not None):
        raise ValueError("the latent call carries both the initial state and the context outputs")
    return pl.pallas_call(
        kern,
        grid=(n_b, nc),
        in_specs=in_specs,
        out_specs=[pl.BlockSpec((C, WIDTH), lambda b, c: (base + b * nc + c, 0)),
                   pl.BlockSpec((C, WIDTH), lambda b, c: (base + b * nc + nc - 1 - c, 0)),
                   pl.BlockSpec((None, 2, HG_H, HG_D, HG_D), lambda b, c: (b, 0, 0, 0, 0))],
        out_shape=[jax.ShapeDtypeStruct((N_TOK, WIDTH), F32),
                   jax.ShapeDtypeStruct((N_TOK, WIDTH), F32),
                   jax.ShapeDtypeStruct((n_b, 2, HG_H, HG_D, HG_D), F32)],
        scratch_shapes=[pltpu.VMEM((2, HG_H, HG_D, HG_D), F32),
                        pltpu.VMEM((2, C, WIDTH), F32)],
        input_output_aliases=aliases,
        compiler_params=_cparams(("parallel", "arbitrary")),
        name="hgrn_lat" if has_init else "hgrn_ctx",
    )(*args)


POOL_T = 256
POOL_HALO = 16


def _pool_kernel(u_ref, up_ref, un_ref, w_ref, sc_ref, o_ref):
    i = pl.program_id(0)
    is_lat = i >= N_CTX // POOL_T
    seq_t = jnp.where(is_lat, LAT_T, CTX_T)
    t0 = jnp.where(is_lat, ((i - N_CTX // POOL_T) % (LAT_T // POOL_T)) * POOL_T, 0)
    TT, HL = POOL_T, POOL_HALO
    diff = lax.broadcasted_iota(jnp.int32, (TT, TT), 1) - lax.broadcasted_iota(jnp.int32, (TT, TT), 0)
    diff_h = lax.broadcasted_iota(jnp.int32, (TT, HL), 1) - lax.broadcasted_iota(jnp.int32, (TT, HL), 0)
    t_glob = t0 + lax.broadcasted_iota(jnp.int32, (TT, POOL_T // 2), 0)
    has_prev = t0 > 0
    has_next = t0 + TT < seq_t
    for g, w in enumerate(POOL_WINDOWS):
        half = w // 2
        sl = slice(g * 128, (g + 1) * 128)
        band = jnp.where((diff >= -half) & (diff < half), 1.0, 0.0).astype(BF16)
        dp = diff_h - HL
        band_p = jnp.where((dp >= -half) & (dp < half) & has_prev, 1.0, 0.0).astype(BF16)
        dn = diff_h + TT
        band_n = jnp.where((dn >= -half) & (dn < half) & has_next, 1.0, 0.0).astype(BF16)
        u = u_ref[:, sl]
        s = (jnp.dot(band, u.astype(BF16), preferred_element_type=F32)
             + jnp.dot(band_p, up_ref[:, sl].astype(BF16), preferred_element_type=F32)
             + jnp.dot(band_n, un_ref[:, sl].astype(BF16), preferred_element_type=F32))
        cnt = (jnp.minimum(t_glob + half, seq_t) - jnp.maximum(t_glob - half, 0)).astype(F32)
        dd = s / cnt - u
        y = jnp.dot(dd.astype(BF16), w_ref[g], preferred_element_type=F32)
        o_ref[:, sl] = y * sc_ref[:, sl]


def _pool(z, pool_w16, pool_scale):
    nt = N_TOK // POOL_T
    per = POOL_T // POOL_HALO
    return pl.pallas_call(
        _pool_kernel,
        grid=(nt,),
        in_specs=[pl.BlockSpec((POOL_T, WIDTH), lambda i: (i, COL_BU)),
                  pl.BlockSpec((POOL_HALO, WIDTH), lambda i: (jnp.maximum(i * per - 1, 0), COL_BU)),
                  pl.BlockSpec((POOL_HALO, WIDTH), lambda i: (jnp.minimum((i + 1) * per, nt * per - 1), COL_BU)),
                  pl.BlockSpec((4, 128, 128), lambda i: (0, 0, 0)),
                  pl.BlockSpec((1, WIDTH), lambda i: (0, 0))],
        out_specs=pl.BlockSpec((POOL_T, WIDTH), lambda i: (i, 0)),
        out_shape=jax.ShapeDtypeStruct((N_TOK, WIDTH), F32),
        compiler_params=_cparams(("parallel",)),
        name="pool",
    )(z, z, z, pool_w16, pool_scale.reshape(1, WIDTH))


def _ctx_attn_kernel(q_ref, k_ref, v_ref, o_ref):
    scale = NA_D ** -0.5
    lane = lax.broadcasted_iota(jnp.int32, (CTX_T, 128), 1)
    for j in range(NA_H // 2):
        sl = slice(j * 128, (j + 1) * 128)
        q = q_ref[:, sl]
        kt = k_ref[:, sl].astype(BF16)
        vt = v_ref[:, sl].astype(BF16)
        outs = []
        for hh in range(2):
            in_head = (lane >= hh * NA_D) & (lane < (hh + 1) * NA_D)
            qm = jnp.where(in_head, q, 0.0).astype(BF16)
            s = lax.dot_general(qm, kt, (((1,), (1,)), ((), ())), preferred_element_type=F32) * scale
            m = jnp.max(s, axis=-1, keepdims=True)
            p = jnp.exp(s - m)
            l = jnp.sum(p, axis=-1, keepdims=True)
            outs.append(jnp.dot(p.astype(BF16), vt, preferred_element_type=F32) / l)
        o_ref[:, sl] = jnp.where(lane < NA_D, outs[0], outs[1])


def _ctx_attn(z):
    spec = lambda col: pl.BlockSpec((CTX_T, WIDTH), lambda b: (b, col))
    return pl.pallas_call(
        _ctx_attn_kernel,
        grid=(N_CTX_B,),
        in_specs=[spec(COL_CQ), spec(COL_CK), spec(COL_CV)],
        out_specs=pl.BlockSpec((CTX_T, WIDTH), lambda b: (b, 0)),
        out_shape=jax.ShapeDtypeStruct((N_TOK, WIDTH), F32),
        compiler_params=_cparams(("parallel",)),
        name="ctx_attn",
    )(z, z, z)


NA_KEYS = WIN_ROWS * GRID_W


def _na_bias_table(rpb_l):
    qc = np.arange(GRID_W)[:, None]
    kc = np.arange(GRID_W)[None, :]
    q0 = np.clip(qc - WIN_COLS // 2, 0, GRID_W - WIN_COLS)
    col_in = (kc >= q0) & (kc < q0 + WIN_COLS)
    dcol = np.clip(kc - qc, -(WIN_COLS - 1), WIN_COLS - 1) + WIN_COLS - 1
    off = np.arange(WIN_ROWS)[:, None]
    w = np.arange(WIN_ROWS)[None, :]
    drow = np.clip(w - off + WIN_ROWS - 1, 0, 2 * WIN_ROWS - 2)
    tbl = rpb_l[:, drow[:, :, None, None], dcol[None, None, :, :]]
    tbl = jnp.where(col_in[None, None, None], tbl, NEG_BIG)
    tbl = tbl.transpose(1, 0, 3, 2, 4)
    return tbl.reshape(WIN_ROWS, NA_H, GRID_W, NA_KEYS).astype(F32)


def _na_row_start(r):
    return jnp.clip(r - WIN_ROWS // 2, 0, GRID_ROWS - WIN_ROWS)


def _na_attn_kernel(q_ref, k_ref, v_ref, kc_ref, vc_ref, tbl_ref, _prev, o_ref):
    scale = NA_D ** -0.5
    r = pl.program_id(1)
    k0 = pl.multiple_of(_na_row_start(r) * GRID_W, GRID_W)
    lane = lax.broadcasted_iota(jnp.int32, (GRID_W, 128), 1)
    for j in range(NA_H // 2):
        sl = slice(j * 128, (j + 1) * 128)
        q = q_ref[:, sl]
        kt = k_ref[pl.ds(k0, NA_KEYS), sl].astype(BF16)
        vt = v_ref[pl.ds(k0, NA_KEYS), sl].astype(BF16)
        kct = kc_ref[:, sl].astype(BF16)
        vct = vc_ref[:, sl].astype(BF16)
        outs = []
        for hh in range(2):
            in_head = (lane >= hh * NA_D) & (lane < (hh + 1) * NA_D)
            qm = jnp.where(in_head, q, 0.0).astype(BF16)
            s_loc = lax.dot_general(qm, kt, (((1,), (1,)), ((), ())), preferred_element_type=F32)
            s_ctx = lax.dot_general(qm, kct, (((1,), (1,)), ((), ())), preferred_element_type=F32) * scale
            tb = tbl_ref[2 * j + hh]
            s_loc = jnp.where(tb > 0.5 * NEG_BIG, s_loc * scale + tb, NEG_BIG)
            m = jnp.maximum(jnp.max(s_loc, axis=-1, keepdims=True), jnp.max(s_ctx, axis=-1, keepdims=True))
            p_loc = jnp.exp(s_loc - m)
            p_ctx = jnp.exp(s_ctx - m)
            l = jnp.sum(p_loc, axis=-1, keepdims=True) + jnp.sum(p_ctx, axis=-1, keepdims=True)
            o = (jnp.dot(p_loc.astype(BF16), vt, preferred_element_type=F32)
                 + jnp.dot(p_ctx.astype(BF16), vct, preferred_element_type=F32))
            outs.append(o / l)
        o_ref[:, sl] = jnp.where(lane < NA_D, outs[0], outs[1])


def _na_attn(z, cache_k, cache_v, tbl, prev, layer):
    base64 = N_CTX // GRID_W
    base_t = N_CTX // LAT_T
    ctx_spec = pl.BlockSpec((None, None, PAST, WIDTH), lambda b, r: (b, layer, 0, 0))
    return pl.pallas_call(
        _na_attn_kernel,
        grid=(N_LAT_B, GRID_ROWS),
        in_specs=[pl.BlockSpec((GRID_W, WIDTH), lambda b, r: (base64 + b * GRID_ROWS + r, COL_CQ)),
                  pl.BlockSpec((LAT_T, WIDTH), lambda b, r: (base_t + b, COL_CK)),
                  pl.BlockSpec((LAT_T, WIDTH), lambda b, r: (base_t + b, COL_CV)),
                  ctx_spec, ctx_spec,
                  pl.BlockSpec((None, NA_H, GRID_W, NA_KEYS), lambda b, r: (r - _na_row_start(r), 0, 0, 0)),
                  pl.BlockSpec(memory_space=pl.ANY)],
        out_specs=pl.BlockSpec((GRID_W, WIDTH), lambda b, r: (base64 + b * GRID_ROWS + r, 0)),
        out_shape=jax.ShapeDtypeStruct((N_TOK, WIDTH), F32),
        input_output_aliases={6: 0},
        compiler_params=_cparams(("parallel", "arbitrary")),
        name="na_attn",
    )(z, z, z, cache_k, cache_v, tbl, prev)


def _merge_kernel(of_ref, ob_ref, ag_ref, op_ref, oc_ref, gl0_ref, gl1_ref, x_ref, m_ref, hn_ref, nf_ref,
                  wa_ref, wb_ref, wc_ref, wo_ref, rhi_ref, rlo_ref, xn_ref, h2_ref, gate_ref, *, with_router):
    o = of_ref[...] + ob_ref[...]
    parts = []
    for h in range(HG_H):
        oh = o[:, h * HG_D:(h + 1) * HG_D]
        parts.append(oh * lax.rsqrt(jnp.mean(oh * oh, axis=-1, keepdims=True) + EPS))
    oa = jnp.concatenate(parts, axis=1) * hn_ref[...] * _silu(ag_ref[...])
    gl0 = gl0_ref[...]
    gl1 = gl1_ref[...]
    gla = gl0[:, :D]
    glb = jnp.concatenate([gl0[:, D:], gl1[:, :WIDTH]], axis=1)
    glc = gl1[:, WIDTH:]
    mix = (_sigmoid(gla) * jnp.dot(oa.astype(BF16), wa_ref[...], preferred_element_type=F32)
           + _sigmoid(glb) * jnp.dot(op_ref[...].astype(BF16), wb_ref[...], preferred_element_type=F32)
           + _sigmoid(glc) * jnp.dot(oc_ref[...].astype(BF16), wc_ref[...], preferred_element_type=F32))
    xn = x_ref[...] + m_ref[2:3, :] * jnp.dot(mix.astype(BF16), wo_ref[...], preferred_element_type=F32)
    xn_ref[...] = xn
    y = xn * lax.rsqrt(jnp.mean(xn * xn, axis=-1, keepdims=True) + EPS) * nf_ref[...]
    h2 = y * (1.0 + m_ref[4:5, :]) + m_ref[3:4, :]
    h2_ref[...] = h2.astype(BF16)
    if with_router:
        hhi = h2.astype(BF16)
        hlo = (h2 - hhi.astype(F32)).astype(BF16)
        logits = (jnp.dot(hhi, rhi_ref[...], preferred_element_type=F32)
                  + jnp.dot(hhi, rlo_ref[...], preferred_element_type=F32)
                  + jnp.dot(hlo, rhi_ref[...], preferred_element_type=F32))
        lane = lax.broadcasted_iota(jnp.int32, logits.shape, 1).astype(F32)
        lg = jnp.where(lane < N_EXP, logits, -jnp.inf)
        m1 = jnp.max(lg, axis=-1, keepdims=True)
        i1 = jnp.min(jnp.where(lg == m1, lane, 128.0), axis=-1, keepdims=True)
        lg2 = jnp.where(lane == i1, -jnp.inf, lg)
        m2 = jnp.max(lg2, axis=-1, keepdims=True)
        i2 = jnp.min(jnp.where(lg2 == m2, lane, 128.0), axis=-1, keepdims=True)
        e = jnp.exp(m2 - m1)
        w1 = 1.0 / (1.0 + e)
        gate_ref[...] = jnp.where(lane == i1, w1, 0.0) + jnp.where(lane == i2, e * w1, 0.0)
    else:
        gate_ref[...] = jnp.zeros(gate_ref.shape, F32)


def _merge(of, ob, z, o_pool, o_attn, x, mods, hgrn_norm_l, norm_ffn_l, wa, wb, wc, wo, rhi, rlo, with_router):
    bm = 512
    glw = 1536
    row = lambda w, col=0: pl.BlockSpec((bm, w), lambda i: (i, col))
    const = lambda shape: pl.BlockSpec(shape, lambda i: (0,) * len(shape))
    return pl.pallas_call(
        functools.partial(_merge_kernel, with_router=with_router),
        grid=(N_TOK // bm,),
        in_specs=[row(WIDTH), row(WIDTH), row(WIDTH, COL_AG), row(WIDTH), row(WIDTH),
                  row(glw, GL_COL0 // glw), row(glw, GL_COL0 // glw + 1), row(D),
                  pl.BlockSpec((None, 6, D), lambda i: (_group_of_rows(i * bm), 0, 0)),
                  const((1, WIDTH)), const((1, D)),
                  const((WIDTH, D)), const((WIDTH, D)), const((WIDTH, D)), const((D, D)),
                  const((D, 128)), const((D, 128))],
        out_specs=[row(D), row(D), row(128)],
        out_shape=[jax.ShapeDtypeStruct((N_TOK, D), F32),
                   jax.ShapeDtypeStruct((N_TOK, D), BF16),
                   jax.ShapeDtypeStruct((N_TOK, 128), F32)],
        compiler_params=_cparams(("parallel",)),
        name="merge",
    )(of, ob, z, o_pool, o_attn, z, z, x, mods, jnp.tile(hgrn_norm_l, HG_H).reshape(1, WIDTH),
      norm_ffn_l.reshape(1, D), wa, wb, wc, wo, rhi, rlo)


def _ffn_kernel(h_ref, x_ref, m_ref, wa_ref, wb_ref, wo_ref, o_ref, acc_scr):
    f = pl.program_id(1)

    @pl.when(f == 0)
    def _():
        acc_scr[...] = jnp.zeros(acc_scr.shape, F32)

    h = h_ref[...]
    a = jnp.dot(h, wa_ref[...].astype(BF16), preferred_element_type=F32)
    b = jnp.dot(h, wb_ref[...].astype(BF16), preferred_element_type=F32)
    g = (_silu(a) * b).astype(BF16)
    acc_scr[...] += jnp.dot(g, wo_ref[...].astype(BF16), preferred_element_type=F32)

    @pl.when(f == pl.num_programs(1) - 1)
    def _():
        o_ref[...] = x_ref[...] + m_ref[5:6, :] * acc_scr[...]


def _ffn(h2, x, mods, w_in, w_out):
    bm, tf = 1024, 256
    nf = F_DENSE // tf
    return pl.pallas_call(
        _ffn_kernel,
        grid=(N_TOK // bm, nf),
        in_specs=[pl.BlockSpec((bm, D), lambda i, f: (i, 0)),
                  pl.BlockSpec((bm, D), lambda i, f: (i, 0)),
                  pl.BlockSpec((None, 6, D), lambda i, f: (_group_of_rows(i * bm), 0, 0)),
                  pl.BlockSpec((None, D, tf), lambda i, f: (0, 0, f)),
                  pl.BlockSpec((None, D, tf), lambda i, f: (0, 0, nf + f)),
                  pl.BlockSpec((None, tf, D), lambda i, f: (0, f, 0))],
        out_specs=pl.BlockSpec((bm, D), lambda i, f: (i, 0)),
        out_shape=jax.ShapeDtypeStruct((N_TOK, D), F32),
        scratch_shapes=[pltpu.VMEM((bm, D), F32)],
        compiler_params=_cparams(("parallel", "arbitrary")),
        name="ffn",
    )(h2, x, mods, w_in, w_in, w_out)


def _moe_kernel(h_ref, g_ref, x_ref, m_ref, wa_ref, wb_ref, wo_ref, nf_ref, y_ref, acc_scr, gcol_scr):
    e = pl.program_id(1)
    f = pl.program_id(2)

    @pl.when((e == 0) & (f == 0))
    def _():
        acc_scr[...] = jnp.zeros(acc_scr.shape, F32)

    @pl.when(f == 0)
    def _():
        g = g_ref[...]
        lane = lax.broadcasted_iota(jnp.int32, g.shape, 1)
        gsel = jnp.sum(jnp.where(lane == e, g, 0.0), axis=-1, keepdims=True)
        gcol_scr[...] = jnp.broadcast_to(gsel, gcol_scr.shape)

    h = h_ref[...]
    a = jnp.dot(h, wa_ref[...].astype(BF16), preferred_element_type=F32)
    b = jnp.dot(h, wb_ref[...].astype(BF16), preferred_element_type=F32)
    gc = gcol_scr[...]
    g = (_silu(a) * b * jnp.concatenate([gc, gc], axis=1)).astype(BF16)
    acc_scr[...] += jnp.dot(g, wo_ref[...].astype(BF16), preferred_element_type=F32)

    @pl.when((e == pl.num_programs(1) - 1) & (f == pl.num_programs(2) - 1))
    def _():
        xn = x_ref[...] + m_ref[5:6, :] * acc_scr[...]
        y_ref[...] = xn * lax.rsqrt(jnp.mean(xn * xn, axis=-1, keepdims=True) + EPS) * nf_ref[...]


def _moe(h2, gates, x, mods, w_in, w_out, norm_final):
    bm, tf = 1024, 256
    nf = F_EXP // tf
    return pl.pallas_call(
        _moe_kernel,
        grid=(N_TOK // bm, N_EXP, nf),
        in_specs=[pl.BlockSpec((bm, D), lambda i, e, f: (i, 0)),
                  pl.BlockSpec((bm, 128), lambda i, e, f: (i, 0)),
                  pl.BlockSpec((bm, D), lambda i, e, f: (i, 0)),
                  pl.BlockSpec((None, 6, D), lambda i, e, f: (_group_of_rows(i * bm), 0, 0)),
                  pl.BlockSpec((None, None, D, tf), lambda i, e, f: (0, e, 0, f)),
                  pl.BlockSpec((None, None, D, tf), lambda i, e, f: (0, e, 0, nf + f)),
                  pl.BlockSpec((None, None, tf, D), lambda i, e, f: (0, e, f, 0)),
                  pl.BlockSpec((1, D), lambda i, e, f: (0, 0))],
        out_specs=pl.BlockSpec((bm, D), lambda i, e, f: (i, 0)),
        out_shape=jax.ShapeDtypeStruct((N_TOK, D), F32),
        scratch_shapes=[pltpu.VMEM((bm, D), F32), pltpu.VMEM((bm, 128), F32)],
        compiler_params=_cparams(("parallel", "arbitrary", "arbitrary")),
        name="moe",
    )(h2, gates, x, mods, w_in, w_in, w_out, norm_final.reshape(1, D))


def _hgrn_lower_bounds(lb_param):
    sm = jax.nn.softmax(lb_param.astype(F32), axis=0)
    return jnp.cumsum(sm, axis=0) - sm[0:1]


def kernel(x_prompt, x_sample, cache_k, cache_v, state_hgrn, c, c_ctx, w_ada, b_ada, norm_mix, norm_ffn,
           w_in, hgrn_lb, hgrn_norm, pool_w, pool_scale, rpb, w_branch_a, w_branch_b, w_branch_c, w_out,
           ffn_w_in, ffn_w_out, router, moe_w_in, moe_w_out, norm_final):
    x = jnp.concatenate([x_prompt.reshape(N_CTX, D), x_sample.reshape(N_LAT, D)], axis=0)
    cond8 = jnp.concatenate([c_ctx[None], c, jnp.zeros((5, D), F32)], axis=0)
    mods = _adaln(cond8, w_ada, b_ada)[:, :3].reshape(DEPTH, 3, 6, D)
    lbs = _hgrn_lower_bounds(hgrn_lb)
    ck = cache_k.reshape(N_LAT_B, DEPTH, PAST, WIDTH)
    cv = cache_v.reshape(N_LAT_B, DEPTH, PAST, WIDTH)
    router_pad = jnp.pad(router[0], ((0, 0), (0, 128 - N_EXP)))
    r_hi = router_pad.astype(BF16)
    r_lo = (router_pad - r_hi.astype(F32)).astype(BF16)

    ks, vs, ss = [], [], []
    for l in range(DEPTH):
        z = _inproj(x, norm_mix[l], mods[l], w_in, l)
        of, ob, s_ctx = _hgrn(z, lbs[l], None, None, 0, N_CTX_B, CTX_T)
        of, ob, _ = _hgrn(z, lbs[l], state_hgrn[:, l], (of, ob), N_CTX, N_LAT_B, LAT_T)
        o_pool = _pool(z, pool_w[l].astype(BF16), pool_scale[l])
        o_attn = _ctx_attn(z)
        o_attn = _na_attn(z, ck, cv, _na_bias_table(rpb[l]), o_attn, l)
        x, h2, gates = _merge(of, ob, z, o_pool, o_attn, x, mods[l], hgrn_norm[l], norm_ffn[l],
                              w_branch_a[l].astype(BF16), w_branch_b[l].astype(BF16),
                              w_branch_c[l].astype(BF16), w_out[l].astype(BF16), r_hi, r_lo,
                              with_router=(l % 2 == 1))
        if l % 2 == 0:
            x = _ffn(h2, x, mods[l], ffn_w_in, ffn_w_out)
        else:
            x = _moe(h2, gates, x, mods[l], moe_w_in, moe_w_out, norm_final)
        ks.append(z[:N_CTX, COL_CK * WIDTH:(COL_CK + 1) * WIDTH].reshape(N_CTX_B, CTX_T, NA_H, NA_D))
        vs.append(z[:N_CTX, COL_CV * WIDTH:(COL_CV + 1) * WIDTH].reshape(N_CTX_B, CTX_T, NA_H, NA_D))
        ss.append(s_ctx)
    y_prompt = x[:N_CTX].reshape(N_CTX_B, CTX_T, D)
    y_sample = x[N_CTX:].reshape(N_LAT_B, LAT_T, D)
    return (y_prompt, y_sample, jnp.stack(ks, axis=1), jnp.stack(vs, axis=1), jnp.stack(ss, axis=1))
```

```python
import functools

import numpy as np
import jax
import jax.numpy as jnp
from jax import lax
from jax.experimental import pallas as pl
from jax.experimental.pallas import tpu as pltpu

F32 = jnp.float32
BF16 = jnp.bfloat16
U32 = jnp.uint32

D = 1024
N_CTX_B, CTX_T = 32, 256
N_LAT_B, LAT_T = 2, 4096
N_CTX = N_CTX_B * CTX_T
N_LAT = N_LAT_B * LAT_T
N_TOK = N_CTX + N_LAT
DEPTH = 2
GRID_W = 64
GRID_ROWS = LAT_T // GRID_W
PAST = 512
HG_H, HG_D = 4, 128
WIDTH = 512
NA_H, NA_D = 8, 64
WIN_ROWS, WIN_COLS = 8, 16
POOL_WINDOWS = (2, 4, 8, 16)
IN_COLS = 7680
F_DENSE = 2816
N_EXP = 8
F_EXP = 3584
GATE_CLIP = 30.0
EPS = 1e-6
NEG_BIG = -1e30

COL_AQ, COL_AFF, COL_AFB, COL_AI, COL_AG, COL_BU, COL_CQ, COL_CK, COL_CV = range(9)
GL_COL0 = 9 * WIDTH

MOE_SLABS = D // 256
ROUTE_I1, ROUTE_I2, ROUTE_W1, ROUTE_W2 = 8, 9, 10, 11

VMEM_LIMIT = 56 * 1024 * 1024


def _cparams(sem):
    return pltpu.CompilerParams(dimension_semantics=sem, vmem_limit_bytes=VMEM_LIMIT)


def _sigmoid(x):
    return 1.0 / (1.0 + jnp.exp(-x))


def _silu(x):
    return x / (1.0 + jnp.exp(-x))


def _group_of_rows(row0):
    return jnp.maximum(row0 - N_CTX + LAT_T, 0) // LAT_T


def _adaln_kernel(c_ref, w_ref, b_ref, o_ref):
    s = _silu(c_ref[...]).astype(BF16)
    o_ref[...] = jnp.dot(s, w_ref[...].astype(BF16), preferred_element_type=F32) + b_ref[...]


def _adaln(cond8, w_ada, b_ada):
    tn = 1536
    return pl.pallas_call(
        _adaln_kernel,
        grid=(DEPTH, 6 * D // tn),
        in_specs=[pl.BlockSpec((8, D), lambda l, j: (0, 0)),
                  pl.BlockSpec((None, D, tn), lambda l, j: (l, 0, j)),
                  pl.BlockSpec((None, 1, tn), lambda l, j: (l, 0, j))],
        out_specs=pl.BlockSpec((None, 8, tn), lambda l, j: (l, 0, j)),
        out_shape=jax.ShapeDtypeStruct((DEPTH, 8, 6 * D), F32),
        compiler_params=_cparams(("parallel", "parallel")),
        name="adaln",
    )(cond8, w_ada, b_ada.reshape(DEPTH, 1, 6 * D))


def _inproj_kernel(x_ref, nw_ref, m_ref, w_ref, z_ref, h_scr):
    @pl.when(pl.program_id(1) == 0)
    def _():
        x = x_ref[...]
        y = x * lax.rsqrt(jnp.mean(x * x, axis=-1, keepdims=True) + EPS) * nw_ref[...]
        h_scr[...] = (y * (1.0 + m_ref[1:2, :]) + m_ref[0:1, :]).astype(BF16)

    z_ref[...] = jnp.dot(h_scr[...], w_ref[...].astype(BF16), preferred_element_type=F32)


def _inproj(x, norm_w, mods, w_in, layer):
    bm, bn = 1024, 1536
    return pl.pallas_call(
        _inproj_kernel,
        grid=(N_TOK // bm, IN_COLS // bn),
        in_specs=[pl.BlockSpec((bm, D), lambda i, j: (i, 0)),
                  pl.BlockSpec((1, D), lambda i, j: (0, 0)),
                  pl.BlockSpec((None, 6, D), lambda i, j: (_group_of_rows(i * bm), 0, 0)),
                  pl.BlockSpec((None, D, bn), lambda i, j: (layer, 0, j))],
        out_specs=pl.BlockSpec((bm, bn), lambda i, j: (i, j)),
        out_shape=jax.ShapeDtypeStruct((N_TOK, IN_COLS), F32),
        scratch_shapes=[pltpu.VMEM((bm, D), BF16)],
        compiler_params=_cparams(("parallel", "arbitrary")),
        name="inproj",
    )(x, norm_w.reshape(1, D), mods, w_in)


HG_C = 128
HG_LEVELS = (4, 8, 16, 32, 64, 128)


def _hgrn_level_ids(reverse):
    t = np.arange(HG_C)[:, None]
    s = np.arange(HG_C)[None, :]
    if reverse:
        t, s = s, t
    lev = np.full((HG_C, HG_C), -1, np.int32)
    lev[(t // 4 == s // 4) & (s <= t)] = 0
    for li, L in enumerate(HG_LEVELS[1:], start=1):
        m = (t // L == s // L) & (t % L >= L // 2) & (s % L < L // 2)
        lev[m] = li
    return lev


def _hgrn_ref_rows(b_scr, d, reverse):
    out = []
    r_lo, r_hi = (2, 6) if reverse else (1, 5)
    sub = lax.broadcasted_iota(jnp.int32, (8, WIDTH), 0)
    pieces = []
    for g in range(HG_C // 8):
        lo = jnp.broadcast_to(b_scr[d, 8 * g + r_lo:8 * g + r_lo + 1, :], (8, WIDTH))
        hi = jnp.broadcast_to(b_scr[d, 8 * g + r_hi:8 * g + r_hi + 1, :], (8, WIDTH))
        pieces.append(jnp.where(sub < 4, lo, hi))
    out.append(jnp.concatenate(pieces, axis=0))
    for L in HG_LEVELS[1:]:
        r = L // 2 - 1 if reverse else L // 2
        pieces = [jnp.broadcast_to(b_scr[d, L * g + r:L * g + r + 1, :], (L, WIDTH))
                  for g in range(HG_C // L)]
        out.append(pieces[0] if len(pieces) == 1 else jnp.concatenate(pieces, axis=0))
    return out


def _hgrn_kernel(*refs, has_init):
    if has_init:
        (qf_ref, ff_ref, vf_ref, qb_ref, fb_ref, vb_ref, lb_ref, levf_ref, levb_ref, s0_ref,
         of_ref, ob_ref, so_ref, st_scr, b_scr) = refs
    else:
        (qf_ref, ff_ref, vf_ref, qb_ref, fb_ref, vb_ref, lb_ref, levf_ref, levb_ref,
         of_ref, ob_ref, so_ref, st_scr, b_scr) = refs
    c = pl.program_id(1)
    C = HG_C

    @pl.when(c == 0)
    def _():
        for d in range(2):
            for h in range(HG_H):
                if has_init:
                    st_scr[d, h] = s0_ref[d, h].T
                else:
                    st_scr[d, h] = jnp.zeros((HG_D, HG_D), F32)

    row = lax.broadcasted_iota(jnp.int32, (C, C), 0)
    col = lax.broadcasted_iota(jnp.int32, (C, C), 1)
    dirs = ((qf_ref, ff_ref, vf_ref, levf_ref, of_ref), (qb_ref, fb_ref, vb_ref, levb_ref, ob_ref))
    for d, (q_ref, f_ref, v_ref, lev_ref, o_ref) in enumerate(dirs):
        reverse = d == 1
        tri = jnp.where((col >= row) if reverse else (col <= row), 1.0, 0.0).astype(BF16)
        q = _silu(q_ref[...])
        fx = jnp.clip(f_ref[...], -GATE_CLIP, GATE_CLIP)
        e = jnp.exp(-fx)
        sig_pos = 1.0 / (1.0 + e)
        sig_neg = e * sig_pos
        lb = lb_ref[d:d + 1, :]
        lf = jnp.log(lb + (1.0 - lb) * sig_pos)
        k = (1.0 - lb) * sig_neg
        hi = lf.astype(BF16)
        r1 = lf - hi.astype(F32)
        mid = r1.astype(BF16)
        lo = (r1 - mid.astype(F32)).astype(BF16)
        b = (jnp.dot(tri, hi, preferred_element_type=F32)
             + jnp.dot(tri, mid, preferred_element_type=F32)
             + jnp.dot(tri, lo, preferred_element_type=F32))
        b_scr[d] = b
        refs_m = _hgrn_ref_rows(b_scr, d, reverse)
        qs, ks = [], []
        for li, m in enumerate(refs_m):
            dlt = b - m
            cap = 80.0 if li == 0 else 0.0
            qs.append((q * jnp.exp(jnp.minimum(dlt, cap))).astype(BF16))
            ks.append((k * jnp.exp(jnp.minimum(-dlt, cap))).astype(BF16))
        lev = lev_ref[...]
        b_end = b[0:1, :] if reverse else b[C - 1:C, :]
        q_in = (q * jnp.exp(b)).astype(BF16)
        k_out = (k * jnp.exp(b_end - b)).astype(BF16)
        dec = jnp.exp(b_end)
        vb16 = v_ref[...].astype(BF16)
        for h in range(HG_H):
            sl = slice(h * HG_D, (h + 1) * HG_D)
            a = jnp.zeros((C, C), F32)
            for li in range(len(HG_LEVELS)):
                p = lax.dot_general(qs[li][:, sl], ks[li][:, sl], (((1,), (1,)), ((), ())),
                                    preferred_element_type=F32)
                a = jnp.where(lev == li, p, a)
            vh = vb16[:, sl]
            st = st_scr[d, h]
            o = (jnp.dot(a.astype(BF16), vh, preferred_element_type=F32)
                 + lax.dot_general(q_in[:, sl], st.astype(BF16), (((1,), (1,)), ((), ())),
                                   preferred_element_type=F32))
            o_ref[:, sl] = o
            upd = lax.dot_general(vh, k_out[:, sl], (((0,), (0,)), ((), ())),
                                  preferred_element_type=F32)
            st_scr[d, h] = st * dec[:, sl] + upd

    @pl.when(c == pl.num_programs(1) - 1)
    def _():
        for d in range(2):
            for h in range(HG_H):
                so_ref[d, h] = st_scr[d, h].T


def _hgrn(z, lb, s0, row_off, n_b, seq_t):
    C = HG_C
    nc = seq_t // C
    base = row_off // C
    has_init = s0 is not None

    def fwd(col):
        return pl.BlockSpec((C, WIDTH), lambda b, c: (base + b * nc + c, col))

    def bwd(col):
        return pl.BlockSpec((C, WIDTH), lambda b, c: (base + b * nc + nc - 1 - c, col))

    full = lambda shape: pl.BlockSpec(shape, lambda b, c: (0,) * len(shape))
    in_specs = [fwd(COL_AQ), fwd(COL_AFF), fwd(COL_AI), bwd(COL_AQ), bwd(COL_AFB), bwd(COL_AI),
                full((2, WIDTH)), full((C, C)), full((C, C))]
    args = [z, z, z, z, z, z, lb, jnp.asarray(_hgrn_level_ids(False)), jnp.asarray(_hgrn_level_ids(True))]
    if has_init:
        in_specs.append(pl.BlockSpec((None, 2, HG_H, HG_D, HG_D), lambda b, c: (b, 0, 0, 0, 0)))
        args.append(s0)
    return pl.pallas_call(
        functools.partial(_hgrn_kernel, has_init=has_init),
        grid=(n_b, nc),
        in_specs=in_specs,
        out_specs=[pl.BlockSpec((C, WIDTH), lambda b, c: (b * nc + c, 0)),
                   pl.BlockSpec((C, WIDTH), lambda b, c: (b * nc + nc - 1 - c, 0)),
                   pl.BlockSpec((None, 2, HG_H, HG_D, HG_D), lambda b, c: (b, 0, 0, 0, 0))],
        out_shape=[jax.ShapeDtypeStruct((n_b * seq_t, WIDTH), F32),
                   jax.ShapeDtypeStruct((n_b * seq_t, WIDTH), F32),
                   jax.ShapeDtypeStruct((n_b, 2, HG_H, HG_D, HG_D), F32)],
        scratch_shapes=[pltpu.VMEM((2, HG_H, HG_D, HG_D), F32),
                        pltpu.VMEM((2, C, WIDTH), F32)],
        compiler_params=_cparams(("parallel", "arbitrary")),
        name="hgrn_lat" if has_init else "hgrn_ctx",
    )(*args)


POOL_T = 256
POOL_HALO = 16


def _pool_kernel(u_ref, up_ref, un_ref, w_ref, sc_ref, o_ref):
    i = pl.program_id(0)
    is_lat = i >= N_CTX // POOL_T
    seq_t = jnp.where(is_lat, LAT_T, CTX_T)
    t0 = jnp.where(is_lat, ((i - N_CTX // POOL_T) % (LAT_T // POOL_T)) * POOL_T, 0)
    TT, HL = POOL_T, POOL_HALO
    diff = lax.broadcasted_iota(jnp.int32, (TT, TT), 1) - lax.broadcasted_iota(jnp.int32, (TT, TT), 0)
    diff_h = lax.broadcasted_iota(jnp.int32, (TT, HL), 1) - lax.broadcasted_iota(jnp.int32, (TT, HL), 0)
    t_glob = t0 + lax.broadcasted_iota(jnp.int32, (TT, 128), 0)
    has_prev = t0 > 0
    has_next = t0 + TT < seq_t
    for g, w in enumerate(POOL_WINDOWS):
        half = w // 2
        sl = slice(g * 128, (g + 1) * 128)
        band = jnp.where((diff >= -half) & (diff < half), 1.0, 0.0).astype(BF16)
        dp = diff_h - HL
        band_p = jnp.where((dp >= -half) & (dp < half) & has_prev, 1.0, 0.0).astype(BF16)
        dn = diff_h + TT
        band_n = jnp.where((dn >= -half) & (dn < half) & has_next, 1.0, 0.0).astype(BF16)
        u = u_ref[:, sl]
        s = (jnp.dot(band, u.astype(BF16), preferred_element_type=F32)
             + jnp.dot(band_p, up_ref[:, sl].astype(BF16), preferred_element_type=F32)
             + jnp.dot(band_n, un_ref[:, sl].astype(BF16), preferred_element_type=F32))
        cnt = (jnp.minimum(t_glob + half, seq_t) - jnp.maximum(t_glob - half, 0)).astype(F32)
        dd = s / cnt - u
        y = jnp.dot(dd.astype(BF16), w_ref[g], preferred_element_type=F32)
        o_ref[:, sl] = y * sc_ref[:, sl]


def _pool(z, pool_w16, pool_scale):
    nt = N_TOK // POOL_T
    per = POOL_T // POOL_HALO
    return pl.pallas_call(
        _pool_kernel,
        grid=(nt,),
        in_specs=[pl.BlockSpec((POOL_T, WIDTH), lambda i: (i, COL_BU)),
                  pl.BlockSpec((POOL_HALO, WIDTH), lambda i: (jnp.maximum(i * per - 1, 0), COL_BU)),
                  pl.BlockSpec((POOL_HALO, WIDTH), lambda i: (jnp.minimum((i + 1) * per, nt * per - 1), COL_BU)),
                  pl.BlockSpec((4, 128, 128), lambda i: (0, 0, 0)),
                  pl.BlockSpec((1, WIDTH), lambda i: (0, 0))],
        out_specs=pl.BlockSpec((POOL_T, WIDTH), lambda i: (i, 0)),
        out_shape=jax.ShapeDtypeStruct((N_TOK, WIDTH), F32),
        compiler_params=_cparams(("parallel",)),
        name="pool",
    )(z, z, z, pool_w16, pool_scale.reshape(1, WIDTH))


def _ctx_attn_kernel(q_ref, k_ref, v_ref, o_ref):
    scale = NA_D ** -0.5
    lane = lax.broadcasted_iota(jnp.int32, (CTX_T, 128), 1)
    for j in range(NA_H // 2):
        sl = slice(j * 128, (j + 1) * 128)
        q = q_ref[:, sl]
        kt = k_ref[:, sl].astype(BF16)
        vt = v_ref[:, sl].astype(BF16)
        outs = []
        for hh in range(2):
            in_head = (lane >= hh * NA_D) & (lane < (hh + 1) * NA_D)
            qm = jnp.where(in_head, q, 0.0).astype(BF16)
            s = lax.dot_general(qm, kt, (((1,), (1,)), ((), ())), preferred_element_type=F32) * scale
            m = jnp.max(s, axis=-1, keepdims=True)
            p = jnp.exp(s - m)
            l = jnp.sum(p, axis=-1, keepdims=True)
            outs.append(jnp.dot(p.astype(BF16), vt, preferred_element_type=F32) / l)
        o_ref[:, sl] = jnp.where(lane < NA_D, outs[0], outs[1])


def _ctx_attn(z):
    spec = lambda col: pl.BlockSpec((CTX_T, WIDTH), lambda b: (b, col))
    return pl.pallas_call(
        _ctx_attn_kernel,
        grid=(N_CTX_B,),
        in_specs=[spec(COL_CQ), spec(COL_CK), spec(COL_CV)],
        out_specs=pl.BlockSpec((CTX_T, WIDTH), lambda b: (b, 0)),
        out_shape=jax.ShapeDtypeStruct((N_CTX, WIDTH), F32),
        compiler_params=_cparams(("parallel",)),
        name="ctx_attn",
    )(z, z, z)


NA_KEYS = WIN_ROWS * GRID_W


def _na_bias_table(rpb_l):
    qc = np.arange(GRID_W)[:, None]
    kc = np.arange(GRID_W)[None, :]
    q0 = np.clip(qc - WIN_COLS // 2, 0, GRID_W - WIN_COLS)
    col_in = (kc >= q0) & (kc < q0 + WIN_COLS)
    dcol = np.clip(kc - qc, -(WIN_COLS - 1), WIN_COLS - 1) + WIN_COLS - 1
    onehot = (dcol[None] == np.arange(2 * WIN_COLS - 1)[:, None, None]).astype(np.float32)
    t2 = jnp.einsum('hdc,cqk->hqdk', rpb_l, jnp.asarray(onehot), precision=lax.Precision.HIGHEST)
    t2 = jnp.where(col_in[None, :, None, :], t2, NEG_BIG)
    tbl = [t2[:, :, WIN_ROWS - 1 - o:2 * WIN_ROWS - 1 - o, :].reshape(NA_H, GRID_W, NA_KEYS)
           for o in range(WIN_ROWS)]
    return jnp.stack(tbl, axis=0)


def _na_row_start(r):
    return jnp.clip(r - WIN_ROWS // 2, 0, GRID_ROWS - WIN_ROWS)


def _na_attn_kernel(q_ref, k_ref, v_ref, kc_ref, vc_ref, tbl_ref, o_ref):
    scale = NA_D ** -0.5
    r = pl.program_id(1)
    k0 = pl.multiple_of(_na_row_start(r) * GRID_W, GRID_W)
    lane = lax.broadcasted_iota(jnp.int32, (GRID_W, 128), 1)
    for j in range(NA_H // 2):
        sl = slice(j * 128, (j + 1) * 128)
        q = q_ref[:, sl]
        kt = k_ref[pl.ds(k0, NA_KEYS), sl].astype(BF16)
        vt = v_ref[pl.ds(k0, NA_KEYS), sl].astype(BF16)
        kct = kc_ref[:, sl].astype(BF16)
        vct = vc_ref[:, sl].astype(BF16)
        outs = []
        for hh in range(2):
            in_head = (lane >= hh * NA_D) & (lane < (hh + 1) * NA_D)
            qm = jnp.where(in_head, q, 0.0).astype(BF16)
            s_loc = lax.dot_general(qm, kt, (((1,), (1,)), ((), ())), preferred_element_type=F32)
            s_ctx = lax.dot_general(qm, kct, (((1,), (1,)), ((), ())), preferred_element_type=F32) * scale
            tb = tbl_ref[2 * j + hh]
            s_loc = jnp.where(tb > 0.5 * NEG_BIG, s_loc * scale + tb, NEG_BIG)
            m = jnp.maximum(jnp.max(s_loc, axis=-1, keepdims=True), jnp.max(s_ctx, axis=-1, keepdims=True))
            p_loc = jnp.exp(s_loc - m)
            p_ctx = jnp.exp(s_ctx - m)
            l = jnp.sum(p_loc, axis=-1, keepdims=True) + jnp.sum(p_ctx, axis=-1, keepdims=True)
            o = (jnp.dot(p_loc.astype(BF16), vt, preferred_element_type=F32)
                 + jnp.dot(p_ctx.astype(BF16), vct, preferred_element_type=F32))
            outs.append(o / l)
        o_ref[:, sl] = jnp.where(lane < NA_D, outs[0], outs[1])


def _na_attn(z, cache_k, cache_v, tbl, layer):
    base64 = N_CTX // GRID_W
    base_t = N_CTX // LAT_T
    ctx_spec = pl.BlockSpec((None, None, PAST, WIDTH), lambda b, r: (b, layer, 0, 0))
    return pl.pallas_call(
        _na_attn_kernel,
        grid=(N_LAT_B, GRID_ROWS),
        in_specs=[pl.BlockSpec((GRID_W, WIDTH), lambda b, r: (base64 + b * GRID_ROWS + r, COL_CQ)),
                  pl.BlockSpec((LAT_T, WIDTH), lambda b, r: (base_t + b, COL_CK)),
                  pl.BlockSpec((LAT_T, WIDTH), lambda b, r: (base_t + b, COL_CV)),
                  ctx_spec, ctx_spec,
                  pl.BlockSpec((None, NA_H, GRID_W, NA_KEYS), lambda b, r: (r - _na_row_start(r), 0, 0, 0))],
        out_specs=pl.BlockSpec((GRID_W, WIDTH), lambda b, r: (b * GRID_ROWS + r, 0)),
        out_shape=jax.ShapeDtypeStruct((N_LAT, WIDTH), F32),
        compiler_params=_cparams(("parallel", "arbitrary")),
        name="na_attn",
    )(z, z, z, cache_k, cache_v, tbl)


def _pack_bf16_pair(hi_f32, lo_f32):
    hi = pltpu.bitcast(hi_f32.astype(BF16).astype(F32), U32)
    lo = pltpu.bitcast(lo_f32.astype(BF16).astype(F32), U32)
    return hi | (lo >> 16)


def _unpack_bf16_pair(u):
    return pltpu.bitcast(u & jnp.uint32(0xFFFF0000), F32), pltpu.bitcast(u << 16, F32)


MERGE_BM = 512


def _merge_kernel(*refs, with_router):
    (ofc_ref, obc_ref, occ_ref, ofl_ref, obl_ref, ocl_ref, ag_ref, op_ref, gl0_ref, gl1_ref, x_ref, m_ref,
     hn_ref, nf_ref, wa_ref, wb_ref, wc_ref, wo_ref) = refs[:18]
    if with_router:
        rhi_ref, rlo_ref, xn_ref, xp_ref, route_ref = refs[18:]
    else:
        xn_ref, h2_ref = refs[18:]
    is_ctx = pl.program_id(0) < N_CTX // MERGE_BM
    o = jnp.where(is_ctx, ofc_ref[...] + obc_ref[...], ofl_ref[...] + obl_ref[...])
    oc = jnp.where(is_ctx, occ_ref[...], ocl_ref[...])
    parts = []
    for h in range(HG_H):
        oh = o[:, h * HG_D:(h + 1) * HG_D]
        parts.append(oh * lax.rsqrt(jnp.mean(oh * oh, axis=-1, keepdims=True) + EPS))
    oa = jnp.concatenate(parts, axis=1) * hn_ref[...] * _silu(ag_ref[...])
    gl0 = gl0_ref[...]
    gl1 = gl1_ref[...]
    gla = gl0[:, :D]
    glb = jnp.concatenate([gl0[:, D:], gl1[:, :WIDTH]], axis=1)
    glc = gl1[:, WIDTH:]
    mix = (_sigmoid(gla) * jnp.dot(oa.astype(BF16), wa_ref[...], preferred_element_type=F32)
           + _sigmoid(glb) * jnp.dot(op_ref[...].astype(BF16), wb_ref[...], preferred_element_type=F32)
           + _sigmoid(glc) * jnp.dot(oc.astype(BF16), wc_ref[...], preferred_element_type=F32))
    xn = x_ref[...] + m_ref[2:3, :] * jnp.dot(mix.astype(BF16), wo_ref[...], preferred_element_type=F32)
    xn_ref[...] = xn
    y = xn * lax.rsqrt(jnp.mean(xn * xn, axis=-1, keepdims=True) + EPS) * nf_ref[...]
    h2 = y * (1.0 + m_ref[4:5, :]) + m_ref[3:4, :]
    if not with_router:
        h2_ref[...] = h2.astype(BF16)
        return
    for s in range(MOE_SLABS):
        xp_ref[:, s, :] = _pack_bf16_pair(h2[:, s * 128:(s + 1) * 128],
                                          h2[:, WIDTH + s * 128:WIDTH + (s + 1) * 128])
    hhi = h2.astype(BF16)
    hlo = (h2 - hhi.astype(F32)).astype(BF16)
    logits = (jnp.dot(hhi, rhi_ref[...], preferred_element_type=F32)
              + jnp.dot(hhi, rlo_ref[...], preferred_element_type=F32)
              + jnp.dot(hlo, rhi_ref[...], preferred_element_type=F32))
    lane = lax.broadcasted_iota(jnp.int32, logits.shape, 1).astype(F32)
    lg = jnp.where(lane < N_EXP, logits, -jnp.inf)
    m1 = jnp.max(lg, axis=-1, keepdims=True)
    i1 = jnp.min(jnp.where(lg == m1, lane, 128.0), axis=-1, keepdims=True)
    lg2 = jnp.where(lane == i1, -jnp.inf, lg)
    m2 = jnp.max(lg2, axis=-1, keepdims=True)
    i2 = jnp.min(jnp.where(lg2 == m2, lane, 128.0), axis=-1, keepdims=True)
    e = jnp.exp(m2 - m1)
    w1 = 1.0 / (1.0 + e)
    route_ref[...] = (jnp.where(lane == ROUTE_I1, i1, 0.0) + jnp.where(lane == ROUTE_I2, i2, 0.0)
                      + jnp.where(lane == ROUTE_W1, w1, 0.0) + jnp.where(lane == ROUTE_W2, e * w1, 0.0))


def _merge(ctx_parts, lat_parts, z, o_pool, x, mods, hgrn_norm_l, norm_ffn_l, wa, wb, wc, wo, router_split):
    bm = MERGE_BM
    glw = 1536
    nct = N_CTX // bm
    with_router = router_split is not None
    row = lambda w, col=0: pl.BlockSpec((bm, w), lambda i: (i, col))
    ctx_row = pl.BlockSpec((bm, WIDTH), lambda i: (jnp.minimum(i, nct - 1), 0))
    lat_row = pl.BlockSpec((bm, WIDTH), lambda i: (jnp.maximum(i - nct, 0), 0))
    const = lambda shape: pl.BlockSpec(shape, lambda i: (0,) * len(shape))
    in_specs = [ctx_row] * 3 + [lat_row] * 3 + [
        row(WIDTH, COL_AG), row(WIDTH),
        row(glw, GL_COL0 // glw), row(glw, GL_COL0 // glw + 1), row(D),
        pl.BlockSpec((None, 6, D), lambda i: (_group_of_rows(i * bm), 0, 0)),
        const((1, WIDTH)), const((1, D)),
        const((WIDTH, D)), const((WIDTH, D)), const((WIDTH, D)), const((D, D))]
    args = list(ctx_parts) + list(lat_parts) + [
        z, o_pool, z, z, x, mods, jnp.tile(hgrn_norm_l, HG_H).reshape(1, WIDTH),
        norm_ffn_l.reshape(1, D), wa, wb, wc, wo]
    if with_router:
        in_specs += [const((D, 128)), const((D, 128))]
        args += list(router_split)
        out_specs = [row(D), pl.BlockSpec((bm, MOE_SLABS, 128), lambda i: (i, 0, 0)), row(128)]
        out_shape = [jax.ShapeDtypeStruct((N_TOK, D), F32),
                     jax.ShapeDtypeStruct((N_TOK, MOE_SLABS, 128), U32),
                     jax.ShapeDtypeStruct((N_TOK, 128), F32)]
    else:
        out_specs = [row(D), row(D)]
        out_shape = [jax.ShapeDtypeStruct((N_TOK, D), F32), jax.ShapeDtypeStruct((N_TOK, D), BF16)]
    return pl.pallas_call(
        functools.partial(_merge_kernel, with_router=with_router),
        grid=(N_TOK // bm,),
        in_specs=in_specs,
        out_specs=out_specs,
        out_shape=out_shape,
        compiler_params=_cparams(("parallel",)),
        name="merge_route" if with_router else "merge",
    )(*args)


def _ffn_kernel(h_ref, x_ref, m_ref, wa_ref, wb_ref, wo_ref, o_ref, acc_scr):
    f = pl.program_id(1)

    @pl.when(f == 0)
    def _():
        acc_scr[...] = jnp.zeros(acc_scr.shape, F32)

    h = h_ref[...]
    a = jnp.dot(h, wa_ref[...].astype(BF16), preferred_element_type=F32)
    b = jnp.dot(h, wb_ref[...].astype(BF16), preferred_element_type=F32)
    g = (_silu(a) * b).astype(BF16)
    acc_scr[...] += jnp.dot(g, wo_ref[...].astype(BF16), preferred_element_type=F32)

    @pl.when(f == pl.num_programs(1) - 1)
    def _():
        o_ref[...] = x_ref[...] + m_ref[5:6, :] * acc_scr[...]


def _ffn(h2, x, mods, w_in, w_out):
    bm, tf = 1024, 256
    nf = F_DENSE // tf
    return pl.pallas_call(
        _ffn_kernel,
        grid=(N_TOK // bm, nf),
        in_specs=[pl.BlockSpec((bm, D), lambda i, f: (i, 0)),
                  pl.BlockSpec((bm, D), lambda i, f: (i, 0)),
                  pl.BlockSpec((None, 6, D), lambda i, f: (_group_of_rows(i * bm), 0, 0)),
                  pl.BlockSpec((None, D, tf), lambda i, f: (0, 0, f)),
                  pl.BlockSpec((None, D, tf), lambda i, f: (0, 0, nf + f)),
                  pl.BlockSpec((None, tf, D), lambda i, f: (0, f, 0))],
        out_specs=pl.BlockSpec((bm, D), lambda i, f: (i, 0)),
        out_shape=jax.ShapeDtypeStruct((N_TOK, D), F32),
        scratch_shapes=[pltpu.VMEM((bm, D), F32)],
        compiler_params=_cparams(("parallel", "arbitrary")),
        name="ffn",
    )(h2, x, mods, w_in, w_in, w_out)


MOE_BM = 1024
MOE_TF = 256
MOE_ROWS = 2 * N_TOK + N_EXP * MOE_BM
MOE_TILES = MOE_ROWS // MOE_BM


def _moe_routing(route):
    i1 = route[:, ROUTE_I1].astype(jnp.int32)
    i2 = route[:, ROUTE_I2].astype(jnp.int32)
    ep = jnp.stack([i1, i2], axis=1).reshape(-1)
    onehot = (ep[:, None] == jnp.arange(N_EXP, dtype=jnp.int32)[None, :]).astype(jnp.int32)
    csum = jnp.cumsum(onehot, axis=0)
    counts = csum[-1]
    padded = ((counts + MOE_BM - 1) // MOE_BM) * MOE_BM
    ends = jnp.cumsum(padded)
    starts = ends - padded
    dest = jnp.sum(onehot * (csum - 1 + starts[None, :]), axis=1)
    tile_row0 = jnp.arange(MOE_TILES, dtype=jnp.int32) * MOE_BM
    tile_active = (tile_row0 < ends[-1]).astype(jnp.int32)
    last_row0 = jnp.maximum(ends[-1] - MOE_BM, 0)
    tile_expert = jnp.sum((jnp.minimum(tile_row0, last_row0)[:, None] >= ends[None, :]).astype(jnp.int32), axis=1)
    return dest, jnp.minimum(tile_expert, N_EXP - 1), tile_active


def _moe_scatter_kernel(dest_ref, xp_ref, _init, xs_ref, sem):
    n = xp_ref.shape[0]

    def issue(j, carry):
        for k in range(2):
            pltpu.make_async_copy(xp_ref.at[j], xs_ref.at[dest_ref[0, 2 * j + k]], sem).start()
        return carry

    lax.fori_loop(0, n, issue, 0, unroll=8)

    def drain(j, carry):
        for k in range(2):
            pltpu.make_async_copy(xp_ref.at[0], xs_ref.at[0], sem).wait()
        return carry

    lax.fori_loop(0, n, drain, 0, unroll=8)


def _moe_scatter(xp, dest):
    bm = 512
    nt = N_TOK // bm
    return pl.pallas_call(
        _moe_scatter_kernel,
        grid=(nt,),
        in_specs=[pl.BlockSpec((None, 1, 2 * bm), lambda i: (i, 0, 0), memory_space=pltpu.SMEM),
                  pl.BlockSpec((bm, MOE_SLABS, 128), lambda i: (i, 0, 0)),
                  pl.BlockSpec(memory_space=pl.ANY)],
        out_specs=pl.BlockSpec(memory_space=pl.ANY),
        out_shape=jax.ShapeDtypeStruct((MOE_ROWS, MOE_SLABS, 128), U32),
        scratch_shapes=[pltpu.SemaphoreType.DMA(())],
        input_output_aliases={2: 0},
        compiler_params=_cparams(("arbitrary",)),
        name="moe_scatter",
    )(dest.reshape(nt, 1, 2 * bm), xp, jnp.zeros((MOE_ROWS, MOE_SLABS, 128), U32))


def _moe_group_kernel(te_ref, ta_ref, xs_ref, wa_ref, wb_ref, wo_ref, y_ref, h_scr, acc_scr):
    i = pl.program_id(0)
    f = pl.program_id(1)
    last = f == pl.num_programs(1) - 1
    active = ta_ref[i] == 1

    @pl.when(active)
    def _():
        @pl.when(f == 0)
        def _():
            for s in range(MOE_SLABS):
                left, right = _unpack_bf16_pair(xs_ref[:, s, :])
                h_scr[:, s * 128:(s + 1) * 128] = left.astype(BF16)
                h_scr[:, WIDTH + s * 128:WIDTH + (s + 1) * 128] = right.astype(BF16)
            acc_scr[...] = jnp.zeros(acc_scr.shape, F32)

        h = h_scr[...]
        a = jnp.dot(h, wa_ref[...].astype(BF16), preferred_element_type=F32)
        b = jnp.dot(h, wb_ref[...].astype(BF16), preferred_element_type=F32)
        g = (_silu(a) * b).astype(BF16)
        acc_scr[...] += jnp.dot(g, wo_ref[...].astype(BF16), preferred_element_type=F32)

        @pl.when(last)
        def _():
            for s in range(MOE_SLABS):
                y_ref[:, s, :] = _pack_bf16_pair(acc_scr[:, s * 128:(s + 1) * 128],
                                                 acc_scr[:, WIDTH + s * 128:WIDTH + (s + 1) * 128])

    @pl.when(jnp.logical_not(active) & last)
    def _():
        y_ref[...] = jnp.zeros(y_ref.shape, U32)


def _moe_group(xs, tile_expert, tile_active, w_in, w_out):
    nf = F_EXP // MOE_TF

    def fblk(i, f, ta):
        return jnp.where(ta[i] == 1, f, nf - 1)

    grid_spec = pltpu.PrefetchScalarGridSpec(
        num_scalar_prefetch=2,
        grid=(MOE_TILES, nf),
        in_specs=[pl.BlockSpec((MOE_BM, MOE_SLABS, 128), lambda i, f, te, ta: (i, 0, 0)),
                  pl.BlockSpec((None, None, D, MOE_TF), lambda i, f, te, ta: (0, te[i], 0, fblk(i, f, ta))),
                  pl.BlockSpec((None, None, D, MOE_TF), lambda i, f, te, ta: (0, te[i], 0, nf + fblk(i, f, ta))),
                  pl.BlockSpec((None, None, MOE_TF, D), lambda i, f, te, ta: (0, te[i], fblk(i, f, ta), 0))],
        out_specs=pl.BlockSpec((MOE_BM, MOE_SLABS, 128), lambda i, f, te, ta: (i, 0, 0)),
        scratch_shapes=[pltpu.VMEM((MOE_BM, D), BF16), pltpu.VMEM((MOE_BM, D), F32)])
    return pl.pallas_call(
        _moe_group_kernel,
        grid_spec=grid_spec,
        out_shape=jax.ShapeDtypeStruct((MOE_ROWS, MOE_SLABS, 128), U32),
        compiler_params=_cparams(("parallel", "arbitrary")),
        name="moe_group",
    )(tile_expert, tile_active, xs, w_in, w_in, w_out)


def _moe_combine_kernel(dest_ref, route_ref, x_ref, m_ref, nf_ref, ys_ref, out_ref, ybuf, sem):
    n = x_ref.shape[0]

    def issue(j, carry):
        for k in range(2):
            pltpu.make_async_copy(ys_ref.at[dest_ref[0, 2 * j + k]], ybuf.at[k * n + j], sem).start()
        return carry

    lax.fori_loop(0, n, issue, 0, unroll=8)

    def drain(j, carry):
        for k in range(2):
            pltpu.make_async_copy(ys_ref.at[0], ybuf.at[0], sem).wait()
        return carry

    lax.fori_loop(0, n, drain, 0, unroll=8)

    route = route_ref[...]
    f = None
    for k, lane in enumerate((ROUTE_W1, ROUTE_W2)):
        left, right = [], []
        for s in range(MOE_SLABS):
            a, b = _unpack_bf16_pair(ybuf[k * n:(k + 1) * n, s, :])
            left.append(a)
            right.append(b)
        yk = jnp.concatenate(left + right, axis=1) * route[:, lane:lane + 1]
        f = yk if f is None else f + yk
    xn = x_ref[...] + m_ref[5:6, :] * f
    out_ref[...] = xn * lax.rsqrt(jnp.mean(xn * xn, axis=-1, keepdims=True) + EPS) * nf_ref[...]


def _moe_combine(ys, dest, route, x, mods, norm_final):
    bm = 256
    nt = N_TOK // bm
    return pl.pallas_call(
        _moe_combine_kernel,
        grid=(nt,),
        in_specs=[pl.BlockSpec((None, 1, 2 * bm), lambda i: (i, 0, 0), memory_space=pltpu.SMEM),
                  pl.BlockSpec((bm, 128), lambda i: (i, 0)),
                  pl.BlockSpec((bm, D), lambda i: (i, 0)),
                  pl.BlockSpec((None, 6, D), lambda i: (_group_of_rows(i * bm), 0, 0)),
                  pl.BlockSpec((1, D), lambda i: (0, 0)),
                  pl.BlockSpec(memory_space=pl.ANY)],
        out_specs=pl.BlockSpec((bm, D), lambda i: (i, 0)),
        out_shape=jax.ShapeDtypeStruct((N_TOK, D), F32),
        scratch_shapes=[pltpu.VMEM((2 * bm, MOE_SLABS, 128), U32), pltpu.SemaphoreType.DMA(())],
        compiler_params=_cparams(("arbitrary",)),
        name="moe_combine",
    )(dest.reshape(nt, 1, 2 * bm), route, x, mods, norm_final.reshape(1, D), ys)


def _moe(xp, route, x, mods, w_in, w_out, norm_final):
    dest, tile_expert, tile_active = _moe_routing(route)
    xs = _moe_scatter(xp, dest)
    ys = _moe_group(xs, tile_expert, tile_active, w_in, w_out)
    return _moe_combine(ys, dest, route, x, mods, norm_final)


def _hgrn_lower_bounds(lb_param):
    sm = jax.nn.softmax(lb_param.astype(F32), axis=0)
    return jnp.cumsum(sm, axis=0) - sm[0:1]


def kernel(x_prompt, x_sample, cache_k, cache_v, state_hgrn, c, c_ctx, w_ada, b_ada, norm_mix, norm_ffn,
           w_in, hgrn_lb, hgrn_norm, pool_w, pool_scale, rpb, w_branch_a, w_branch_b, w_branch_c, w_out,
           ffn_w_in, ffn_w_out, router, moe_w_in, moe_w_out, norm_final):
    x = jnp.concatenate([x_prompt.reshape(N_CTX, D), x_sample.reshape(N_LAT, D)], axis=0)
    cond8 = jnp.concatenate([c_ctx[None], c, jnp.zeros((5, D), F32)], axis=0)
    mods = _adaln(cond8, w_ada, b_ada)[:, :3].reshape(DEPTH, 3, 6, D)
    lbs = _hgrn_lower_bounds(hgrn_lb)
    ck = cache_k.reshape(N_LAT_B, DEPTH, PAST, WIDTH)
    cv = cache_v.reshape(N_LAT_B, DEPTH, PAST, WIDTH)
    router_pad = jnp.pad(router[0], ((0, 0), (0, 128 - N_EXP)))
    r_hi = router_pad.astype(BF16)
    r_lo = (router_pad - r_hi.astype(F32)).astype(BF16)

    ks, vs, ss = [], [], []
    for l in range(DEPTH):
        z = _inproj(x, norm_mix[l], mods[l], w_in, l)
        of_c, ob_c, s_ctx = _hgrn(z, lbs[l], None, 0, N_CTX_B, CTX_T)
        of_l, ob_l, _ = _hgrn(z, lbs[l], state_hgrn[:, l], N_CTX, N_LAT_B, LAT_T)
        o_pool = _pool(z, pool_w[l].astype(BF16), pool_scale[l])
        oc_c = _ctx_attn(z)
        oc_l = _na_attn(z, ck, cv, _na_bias_table(rpb[l]), l)
        merged = _merge((of_c, ob_c, oc_c), (of_l, ob_l, oc_l), z, o_pool, x, mods[l], hgrn_norm[l], norm_ffn[l],
                        w_branch_a[l].astype(BF16), w_branch_b[l].astype(BF16),
                        w_branch_c[l].astype(BF16), w_out[l].astype(BF16),
                        (r_hi, r_lo) if l % 2 == 1 else None)
        if l % 2 == 0:
            x, h2 = merged
            x = _ffn(h2, x, mods[l], ffn_w_in, ffn_w_out)
        else:
            x, xp, route = merged
            x = _moe(xp, route, x, mods[l], moe_w_in, moe_w_out, norm_final)
        ks.append(z[:N_CTX, COL_CK * WIDTH:(COL_CK + 1) * WIDTH].reshape(N_CTX_B, CTX_T, NA_H, NA_D))
        vs.append(z[:N_CTX, COL_CV * WIDTH:(COL_CV + 1) * WIDTH].reshape(N_CTX_B, CTX_T, NA_H, NA_D))
        ss.append(s_ctx)
    y_prompt = x[:N_CTX].reshape(N_CTX_B, CTX_T, D)
    y_sample = x[N_CTX:].reshape(N_LAT_B, LAT_T, D)
    return (y_prompt, y_sample, jnp.stack(ks, axis=1), jnp.stack(vs, axis=1), jnp.stack(ss, axis=1))
```

```python
import functools

import numpy as np
import jax
import jax.numpy as jnp
from jax import lax
from jax.experimental import pallas as pl
from jax.experimental.pallas import tpu as pltpu

F32 = jnp.float32
BF16 = jnp.bfloat16

D = 1024
N_CTX_B, CTX_T = 32, 256
N_LAT_B, LAT_T = 2, 4096
N_CTX = N_CTX_B * CTX_T
N_LAT = N_LAT_B * LAT_T
N_TOK = N_CTX + N_LAT
DEPTH = 2
GRID_W = 64
GRID_ROWS = LAT_T // GRID_W
PAST = 512
HG_H, HG_D = 4, 128
WIDTH = 512
NA_H, NA_D = 8, 64
WIN_ROWS, WIN_COLS = 8, 16
POOL_WINDOWS = (2, 4, 8, 16)
IN_COLS = 7680
F_DENSE = 2816
N_EXP = 8
F_EXP = 3584
GATE_CLIP = 30.0
EPS = 1e-6
NEG_BIG = -1e30

COL_AQ, COL_AFF, COL_AFB, COL_AI, COL_AG, COL_BU, COL_CQ, COL_CK, COL_CV = range(9)
GL_COL0 = 9 * WIDTH

ROUTE_I1, ROUTE_I2, ROUTE_W1, ROUTE_W2 = 8, 9, 10, 11

VMEM_LIMIT = 56 * 1024 * 1024


def _cparams(sem):
    return pltpu.CompilerParams(dimension_semantics=sem, vmem_limit_bytes=VMEM_LIMIT)


def _sigmoid(x):
    return 1.0 / (1.0 + jnp.exp(-x))


def _silu(x):
    return x / (1.0 + jnp.exp(-x))


def _group_of_rows(row0):
    return jnp.maximum(row0 - N_CTX + LAT_T, 0) // LAT_T


def _adaln_kernel(c_ref, w_ref, b_ref, o_ref):
    s = _silu(c_ref[...]).astype(BF16)
    o_ref[...] = jnp.dot(s, w_ref[...].astype(BF16), preferred_element_type=F32) + b_ref[...]


def _adaln(cond8, w_ada, b_ada):
    tn = 1536
    return pl.pallas_call(
        _adaln_kernel,
        grid=(DEPTH, 6 * D // tn),
        in_specs=[pl.BlockSpec((8, D), lambda l, j: (0, 0)),
                  pl.BlockSpec((None, D, tn), lambda l, j: (l, 0, j)),
                  pl.BlockSpec((None, 1, tn), lambda l, j: (l, 0, j))],
        out_specs=pl.BlockSpec((None, 8, tn), lambda l, j: (l, 0, j)),
        out_shape=jax.ShapeDtypeStruct((DEPTH, 8, 6 * D), F32),
        compiler_params=_cparams(("parallel", "parallel")),
        name="adaln",
    )(cond8, w_ada, b_ada.reshape(DEPTH, 1, 6 * D))


INPROJ_BM, INPROJ_BN = 1024, 1536
GATE_TILE = (COL_AFF * WIDTH) // INPROJ_BN
KV_TILE = (COL_CK * WIDTH) // INPROJ_BN


def _inproj_kernel(x_ref, nw_ref, m_ref, w_ref, z_ref, zg_ref, kv_ref, h_scr):
    i = pl.program_id(0)
    j = pl.program_id(1)

    @pl.when(j == 0)
    def _():
        x = x_ref[...]
        y = x * lax.rsqrt(jnp.mean(x * x, axis=-1, keepdims=True) + EPS) * nw_ref[...]
        h_scr[...] = (y * (1.0 + m_ref[1:2, :]) + m_ref[0:1, :]).astype(BF16)

    acc = jnp.dot(h_scr[...], w_ref[...], preferred_element_type=F32)
    z_ref[...] = acc.astype(BF16)

    @pl.when(j == GATE_TILE)
    def _():
        c0 = COL_AFF * WIDTH - GATE_TILE * INPROJ_BN
        zg_ref[...] = acc[:, c0:c0 + 2 * WIDTH]

    @pl.when((j == KV_TILE) & (i < N_CTX // INPROJ_BM))
    def _():
        c0 = COL_CK * WIDTH - KV_TILE * INPROJ_BN
        kv_ref[...] = acc[:, c0:c0 + 2 * WIDTH]


def _inproj(x, norm_w, mods, w_in16, layer):
    bm, bn = INPROJ_BM, INPROJ_BN
    nct = N_CTX // bm
    return pl.pallas_call(
        _inproj_kernel,
        grid=(N_TOK // bm, IN_COLS // bn),
        in_specs=[pl.BlockSpec((bm, D), lambda i, j: (i, 0)),
                  pl.BlockSpec((1, D), lambda i, j: (0, 0)),
                  pl.BlockSpec((None, 6, D), lambda i, j: (_group_of_rows(i * bm), 0, 0)),
                  pl.BlockSpec((None, D, bn), lambda i, j: (layer, 0, j))],
        out_specs=[pl.BlockSpec((bm, bn), lambda i, j: (i, j)),
                   pl.BlockSpec((bm, 2 * WIDTH), lambda i, j: (i, 0)),
                   pl.BlockSpec((bm, 2 * WIDTH), lambda i, j: (jnp.minimum(i, nct - 1), 0))],
        out_shape=[jax.ShapeDtypeStruct((N_TOK, IN_COLS), BF16),
                   jax.ShapeDtypeStruct((N_TOK, 2 * WIDTH), F32),
                   jax.ShapeDtypeStruct((N_CTX, 2 * WIDTH), F32)],
        scratch_shapes=[pltpu.VMEM((bm, D), BF16)],
        compiler_params=_cparams(("arbitrary", "arbitrary")),
        name="inproj",
    )(x, norm_w.reshape(1, D), mods, w_in16)


HG_C = 128
HG_LEVELS = (4, 8, 16, 32, 64, 128)


def _hgrn_level_ids(reverse):
    t = np.arange(HG_C)[:, None]
    s = np.arange(HG_C)[None, :]
    if reverse:
        t, s = s, t
    lev = np.full((HG_C, HG_C), -1, np.int32)
    lev[(t // 4 == s // 4) & (s <= t)] = 0
    for li, L in enumerate(HG_LEVELS[1:], start=1):
        m = (t // L == s // L) & (t % L >= L // 2) & (s % L < L // 2)
        lev[m] = li
    return lev


def _hgrn_ref_rows(b_scr, d, reverse):
    out = []
    r_lo, r_hi = (2, 6) if reverse else (1, 5)
    sub = lax.broadcasted_iota(jnp.int32, (8, WIDTH), 0)
    pieces = []
    for g in range(HG_C // 8):
        lo = jnp.broadcast_to(b_scr[d, 8 * g + r_lo:8 * g + r_lo + 1, :], (8, WIDTH))
        hi = jnp.broadcast_to(b_scr[d, 8 * g + r_hi:8 * g + r_hi + 1, :], (8, WIDTH))
        pieces.append(jnp.where(sub < 4, lo, hi))
    out.append(jnp.concatenate(pieces, axis=0))
    for L in HG_LEVELS[1:]:
        r = L // 2 - 1 if reverse else L // 2
        pieces = [jnp.broadcast_to(b_scr[d, L * g + r:L * g + r + 1, :], (L, WIDTH))
                  for g in range(HG_C // L)]
        out.append(pieces[0] if len(pieces) == 1 else jnp.concatenate(pieces, axis=0))
    return out


def _hgrn_kernel(*refs, has_init):
    if has_init:
        (qf_ref, ff_ref, vf_ref, qb_ref, fb_ref, vb_ref, lb_ref, levf_ref, levb_ref, s0_ref,
         of_ref, ob_ref, so_ref, st_scr, b_scr) = refs
    else:
        (qf_ref, ff_ref, vf_ref, qb_ref, fb_ref, vb_ref, lb_ref, levf_ref, levb_ref,
         of_ref, ob_ref, so_ref, st_scr, b_scr) = refs
    c = pl.program_id(1)
    C = HG_C

    @pl.when(c == 0)
    def _():
        for d in range(2):
            for h in range(HG_H):
                if has_init:
                    st_scr[d, h] = s0_ref[d, h].T
                else:
                    st_scr[d, h] = jnp.zeros((HG_D, HG_D), F32)

    row = lax.broadcasted_iota(jnp.int32, (C, C), 0)
    col = lax.broadcasted_iota(jnp.int32, (C, C), 1)
    dirs = ((qf_ref, ff_ref, vf_ref, levf_ref, of_ref), (qb_ref, fb_ref, vb_ref, levb_ref, ob_ref))
    for d, (q_ref, f_ref, v_ref, lev_ref, o_ref) in enumerate(dirs):
        reverse = d == 1
        tri = jnp.where((col >= row) if reverse else (col <= row), 1.0, 0.0).astype(BF16)
        q = _silu(q_ref[...].astype(F32))
        fx = jnp.clip(f_ref[...], -GATE_CLIP, GATE_CLIP)
        e = jnp.exp(-fx)
        sig_pos = 1.0 / (1.0 + e)
        sig_neg = e * sig_pos
        lb = lb_ref[d:d + 1, :]
        lf = jnp.log(lb + (1.0 - lb) * sig_pos)
        k = (1.0 - lb) * sig_neg
        hi = lf.astype(BF16)
        r1 = lf - hi.astype(F32)
        mid = r1.astype(BF16)
        lo = (r1 - mid.astype(F32)).astype(BF16)
        b = (jnp.dot(tri, hi, preferred_element_type=F32)
             + jnp.dot(tri, mid, preferred_element_type=F32)
             + jnp.dot(tri, lo, preferred_element_type=F32))
        b_scr[d] = b
        refs_m = _hgrn_ref_rows(b_scr, d, reverse)
        qs, ks = [], []
        for li, m in enumerate(refs_m):
            dlt = b - m
            cap = 80.0 if li == 0 else 0.0
            qs.append((q * jnp.exp(jnp.minimum(dlt, cap))).astype(BF16))
            ks.append((k * jnp.exp(jnp.minimum(-dlt, cap))).astype(BF16))
        lev = lev_ref[...]
        b_end = b[0:1, :] if reverse else b[C - 1:C, :]
        q_in = (q * jnp.exp(b)).astype(BF16)
        k_out = (k * jnp.exp(b_end - b)).astype(BF16)
        dec = jnp.exp(b_end)
        vb16 = v_ref[...]
        for h in range(HG_H):
            sl = slice(h * HG_D, (h + 1) * HG_D)
            a = jnp.zeros((C, C), F32)
            for li in range(len(HG_LEVELS)):
                p = lax.dot_general(qs[li][:, sl], ks[li][:, sl], (((1,), (1,)), ((), ())),
                                    preferred_element_type=F32)
                a = jnp.where(lev == li, p, a)
            vh = vb16[:, sl]
            st = st_scr[d, h]
            o = (jnp.dot(a.astype(BF16), vh, preferred_element_type=F32)
                 + lax.dot_general(q_in[:, sl], st.astype(BF16), (((1,), (1,)), ((), ())),
                                   preferred_element_type=F32))
            o_ref[:, sl] = o
            upd = lax.dot_general(vh, k_out[:, sl], (((0,), (0,)), ((), ())),
                                  preferred_element_type=F32)
            st_scr[d, h] = st * dec[:, sl] + upd

    @pl.when(c == pl.num_programs(1) - 1)
    def _():
        for d in range(2):
            for h in range(HG_H):
                so_ref[d, h] = st_scr[d, h].T


def _hgrn(z, zg, lb, s0, row_off, n_b, seq_t):
    C = HG_C
    nc = seq_t // C
    base = row_off // C
    has_init = s0 is not None

    def fwd(col):
        return pl.BlockSpec((C, WIDTH), lambda b, c: (base + b * nc + c, col))

    def bwd(col):
        return pl.BlockSpec((C, WIDTH), lambda b, c: (base + b * nc + nc - 1 - c, col))

    full = lambda shape: pl.BlockSpec(shape, lambda b, c: (0,) * len(shape))
    in_specs = [fwd(COL_AQ), fwd(0), fwd(COL_AI), bwd(COL_AQ), bwd(1), bwd(COL_AI),
                full((2, WIDTH)), full((C, C)), full((C, C))]
    args = [z, zg, z, z, zg, z, lb, jnp.asarray(_hgrn_level_ids(False)), jnp.asarray(_hgrn_level_ids(True))]
    if has_init:
        in_specs.append(pl.BlockSpec((None, 2, HG_H, HG_D, HG_D), lambda b, c: (b, 0, 0, 0, 0)))
        args.append(s0)
    return pl.pallas_call(
        functools.partial(_hgrn_kernel, has_init=has_init),
        grid=(n_b, nc),
        in_specs=in_specs,
        out_specs=[pl.BlockSpec((C, WIDTH), lambda b, c: (b * nc + c, 0)),
                   pl.BlockSpec((C, WIDTH), lambda b, c: (b * nc + nc - 1 - c, 0)),
                   pl.BlockSpec((None, 2, HG_H, HG_D, HG_D), lambda b, c: (b, 0, 0, 0, 0))],
        out_shape=[jax.ShapeDtypeStruct((n_b * seq_t, WIDTH), F32),
                   jax.ShapeDtypeStruct((n_b * seq_t, WIDTH), F32),
                   jax.ShapeDtypeStruct((n_b, 2, HG_H, HG_D, HG_D), F32)],
        scratch_shapes=[pltpu.VMEM((2, HG_H, HG_D, HG_D), F32),
                        pltpu.VMEM((2, C, WIDTH), F32)],
        compiler_params=_cparams(("parallel", "arbitrary")),
        name="hgrn_lat" if has_init else "hgrn_ctx",
    )(*args)


POOL_T = 256
POOL_HALO = 16


def _pool_kernel(u_ref, up_ref, un_ref, w_ref, sc_ref, o_ref):
    i = pl.program_id(0)
    is_lat = i >= N_CTX // POOL_T
    seq_t = jnp.where(is_lat, LAT_T, CTX_T)
    t0 = jnp.where(is_lat, ((i - N_CTX // POOL_T) % (LAT_T // POOL_T)) * POOL_T, 0)
    TT, HL = POOL_T, POOL_HALO
    diff = lax.broadcasted_iota(jnp.int32, (TT, TT), 1) - lax.broadcasted_iota(jnp.int32, (TT, TT), 0)
    diff_h = lax.broadcasted_iota(jnp.int32, (TT, HL), 1) - lax.broadcasted_iota(jnp.int32, (TT, HL), 0)
    t_glob = t0 + lax.broadcasted_iota(jnp.int32, (TT, 128), 0)
    has_prev = t0 > 0
    has_next = t0 + TT < seq_t
    for g, w in enumerate(POOL_WINDOWS):
        half = w // 2
        sl = slice(g * 128, (g + 1) * 128)
        band = jnp.where((diff >= -half) & (diff < half), 1.0, 0.0).astype(BF16)
        dp = diff_h - HL
        band_p = jnp.where((dp >= -half) & (dp < half) & has_prev, 1.0, 0.0).astype(BF16)
        dn = diff_h + TT
        band_n = jnp.where((dn >= -half) & (dn < half) & has_next, 1.0, 0.0).astype(BF16)
        u = u_ref[:, sl]
        s = (jnp.dot(band, u, preferred_element_type=F32)
             + jnp.dot(band_p, up_ref[:, sl], preferred_element_type=F32)
             + jnp.dot(band_n, un_ref[:, sl], preferred_element_type=F32))
        cnt = (jnp.minimum(t_glob + half, seq_t) - jnp.maximum(t_glob - half, 0)).astype(F32)
        dd = s / cnt - u.astype(F32)
        y = jnp.dot(dd.astype(BF16), w_ref[g], preferred_element_type=F32)
        o_ref[:, sl] = y * sc_ref[:, sl]


def _pool(z, pool_w16, pool_scale):
    nt = N_TOK // POOL_T
    per = POOL_T // POOL_HALO
    return pl.pallas_call(
        _pool_kernel,
        grid=(nt,),
        in_specs=[pl.BlockSpec((POOL_T, WIDTH), lambda i: (i, COL_BU)),
                  pl.BlockSpec((POOL_HALO, WIDTH), lambda i: (jnp.maximum(i * per - 1, 0), COL_BU)),
                  pl.BlockSpec((POOL_HALO, WIDTH), lambda i: (jnp.minimum((i + 1) * per, nt * per - 1), COL_BU)),
                  pl.BlockSpec((4, 128, 128), lambda i: (0, 0, 0)),
                  pl.BlockSpec((1, WIDTH), lambda i: (0, 0))],
        out_specs=pl.BlockSpec((POOL_T, WIDTH), lambda i: (i, 0)),
        out_shape=jax.ShapeDtypeStruct((N_TOK, WIDTH), F32),
        compiler_params=_cparams(("parallel",)),
        name="pool",
    )(z, z, z, pool_w16, pool_scale.reshape(1, WIDTH))


def _head_mask(hh):
    lane = lax.broadcasted_iota(jnp.int32, (1, 128), 1)
    in_head = (lane >= hh * NA_D) & (lane < (hh + 1) * NA_D)
    return jnp.where(in_head, NA_D ** -0.5, 0.0).astype(BF16)


def _ctx_attn_kernel(q_ref, k_ref, v_ref, o_ref):
    lane = lax.broadcasted_iota(jnp.int32, (CTX_T, 128), 1)
    for j in range(NA_H // 2):
        sl = slice(j * 128, (j + 1) * 128)
        q = q_ref[:, sl]
        kt = k_ref[:, sl]
        vt = v_ref[:, sl]
        outs = []
        for hh in range(2):
            s = lax.dot_general(q * _head_mask(hh), kt, (((1,), (1,)), ((), ())), preferred_element_type=F32)
            m = jnp.max(s, axis=-1, keepdims=True)
            p = jnp.exp(s - m)
            l = jnp.sum(p, axis=-1, keepdims=True)
            outs.append(jnp.dot(p.astype(BF16), vt, preferred_element_type=F32) / l)
        o_ref[:, sl] = jnp.where(lane < NA_D, outs[0], outs[1])


def _ctx_attn(z):
    spec = lambda col: pl.BlockSpec((CTX_T, WIDTH), lambda b: (b, col))
    return pl.pallas_call(
        _ctx_attn_kernel,
        grid=(N_CTX_B,),
        in_specs=[spec(COL_CQ), spec(COL_CK), spec(COL_CV)],
        out_specs=pl.BlockSpec((CTX_T, WIDTH), lambda b: (b, 0)),
        out_shape=jax.ShapeDtypeStruct((N_CTX, WIDTH), F32),
        compiler_params=_cparams(("parallel",)),
        name="ctx_attn",
    )(z, z, z)


NA_QR = 4
NA_KR = 12
NA_NQ = NA_QR * GRID_W
NA_NK = NA_KR * GRID_W
NA_BLOCKS = GRID_ROWS // NA_QR


def _na_key_row0(blk):
    return jnp.clip(blk * NA_QR - WIN_ROWS // 2, 0, GRID_ROWS - NA_KR)


def _na_geometry():
    patterns, var_of_block = [], []
    for blk in range(NA_BLOCKS):
        r = blk * NA_QR + np.arange(NA_QR)[:, None]
        kr = int(np.clip(blk * NA_QR - WIN_ROWS // 2, 0, GRID_ROWS - NA_KR)) + np.arange(NA_KR)[None, :]
        rs = np.clip(r - WIN_ROWS // 2, 0, GRID_ROWS - WIN_ROWS)
        valid = (kr >= rs) & (kr < rs + WIN_ROWS)
        assert (valid.sum(axis=1) == WIN_ROWS).all(), "key rows must cover every query row's window"
        drow = np.where(valid, kr - r + WIN_ROWS - 1, 0)
        key = (drow.tobytes(), valid.tobytes())
        ids = [i for i, (k_, _, _) in enumerate(patterns) if k_ == key]
        if not ids:
            patterns.append((key, drow, valid))
            ids = [len(patterns) - 1]
        var_of_block.append(ids[0])
    drow = np.stack([p[1] for p in patterns])
    valid = np.stack([p[2] for p in patterns])
    return np.asarray(var_of_block, np.int32), drow, valid


def _na_bias_table(rpb_l):
    _, drow, valid = _na_geometry()
    qc = np.arange(GRID_W)[:, None]
    kc = np.arange(GRID_W)[None, :]
    q0 = np.clip(qc - WIN_COLS // 2, 0, GRID_W - WIN_COLS)
    col_in = (kc >= q0) & (kc < q0 + WIN_COLS)
    dcol = np.clip(kc - qc, -(WIN_COLS - 1), WIN_COLS - 1) + WIN_COLS - 1
    oh_col = (dcol[None] == np.arange(2 * WIN_COLS - 1)[:, None, None]).astype(np.float32)
    oh_row = ((drow[None] == np.arange(2 * WIN_ROWS - 1)[:, None, None, None]) & valid[None]).astype(np.float32)
    tbl = jnp.einsum('hdc,cqk,dvrs->vhrqsk', rpb_l, jnp.asarray(oh_col), jnp.asarray(oh_row),
                     precision=lax.Precision.HIGHEST)
    keep = valid[:, None, :, None, :, None] & col_in[None, None, None, :, None, :]
    tbl = jnp.where(keep, tbl, NEG_BIG)
    return tbl.reshape(valid.shape[0], NA_H, NA_NQ, NA_NK)


def _na_attn_kernel(var_ref, q_ref, k_ref, v_ref, kc_ref, vc_ref, tbl_ref, o_ref):
    del var_ref
    blk = pl.program_id(1)
    k0 = pl.multiple_of(_na_key_row0(blk) * GRID_W, GRID_W)
    lane = lax.broadcasted_iota(jnp.int32, (NA_NQ, 128), 1)
    for j in range(NA_H // 2):
        sl = slice(j * 128, (j + 1) * 128)
        q = q_ref[:, sl]
        kt = k_ref[pl.ds(k0, NA_NK), sl]
        vt = v_ref[pl.ds(k0, NA_NK), sl]
        kct = kc_ref[:, sl]
        vct = vc_ref[:, sl]
        outs = []
        for hh in range(2):
            qm = q * _head_mask(hh)
            s_loc = lax.dot_general(qm, kt, (((1,), (1,)), ((), ())), preferred_element_type=F32)
            s_ctx = lax.dot_general(qm, kct, (((1,), (1,)), ((), ())), preferred_element_type=F32)
            tb = tbl_ref[2 * j + hh]
            s_loc = jnp.where(tb > 0.5 * NEG_BIG, s_loc + tb, NEG_BIG)
            m = jnp.maximum(jnp.max(s_loc, axis=-1, keepdims=True), jnp.max(s_ctx, axis=-1, keepdims=True))
            p_loc = jnp.exp(s_loc - m)
            p_ctx = jnp.exp(s_ctx - m)
            l = jnp.sum(p_loc, axis=-1, keepdims=True) + jnp.sum(p_ctx, axis=-1, keepdims=True)
            o = (jnp.dot(p_loc.astype(BF16), vt, preferred_element_type=F32)
                 + jnp.dot(p_ctx.astype(BF16), vct, preferred_element_type=F32))
            outs.append(o / l)
        o_ref[:, sl] = jnp.where(lane < NA_D, outs[0], outs[1])


def _na_attn(z, cache_k16, cache_v16, tbl, layer):
    base_q = N_CTX // NA_NQ
    base_t = N_CTX // LAT_T
    ctx_spec = pl.BlockSpec((None, None, PAST, WIDTH), lambda b, r, var: (b, layer, 0, 0))
    grid_spec = pltpu.PrefetchScalarGridSpec(
        num_scalar_prefetch=1,
        grid=(N_LAT_B, NA_BLOCKS),
        in_specs=[pl.BlockSpec((NA_NQ, WIDTH), lambda b, r, var: (base_q + b * NA_BLOCKS + r, COL_CQ)),
                  pl.BlockSpec((LAT_T, WIDTH), lambda b, r, var: (base_t + b, COL_CK)),
                  pl.BlockSpec((LAT_T, WIDTH), lambda b, r, var: (base_t + b, COL_CV)),
                  ctx_spec, ctx_spec,
                  pl.BlockSpec((None, NA_H, NA_NQ, NA_NK), lambda b, r, var: (var[r], 0, 0, 0))],
        out_specs=pl.BlockSpec((NA_NQ, WIDTH), lambda b, r, var: (b * NA_BLOCKS + r, 0)))
    return pl.pallas_call(
        _na_attn_kernel,
        grid_spec=grid_spec,
        out_shape=jax.ShapeDtypeStruct((N_LAT, WIDTH), F32),
        compiler_params=_cparams(("parallel", "arbitrary")),
        name="na_attn",
    )(jnp.asarray(_na_geometry()[0]), z, z, z, cache_k16, cache_v16, tbl)


MERGE_BM = 512


def _merge_kernel(*refs, with_router):
    (ofc_ref, obc_ref, occ_ref, ofl_ref, obl_ref, ocl_ref, ag_ref, op_ref, gl0_ref, gl1_ref, x_ref, m_ref,
     hn_ref, nf_ref, wa_ref, wb_ref, wc_ref, wo_ref) = refs[:18]
    if with_router:
        rhi_ref, rlo_ref, xn_ref, h2_ref, route_ref = refs[18:]
    else:
        xn_ref, h2_ref = refs[18:]
    is_ctx = pl.program_id(0) < N_CTX // MERGE_BM
    o = jnp.where(is_ctx, ofc_ref[...] + obc_ref[...], ofl_ref[...] + obl_ref[...])
    oc = jnp.where(is_ctx, occ_ref[...], ocl_ref[...])
    parts = []
    for h in range(HG_H):
        oh = o[:, h * HG_D:(h + 1) * HG_D]
        parts.append(oh * lax.rsqrt(jnp.mean(oh * oh, axis=-1, keepdims=True) + EPS))
    oa = jnp.concatenate(parts, axis=1) * hn_ref[...] * _silu(ag_ref[...].astype(F32))
    gl0 = gl0_ref[...].astype(F32)
    gl1 = gl1_ref[...].astype(F32)
    gla = gl0[:, :D]
    glb = jnp.concatenate([gl0[:, D:], gl1[:, :WIDTH]], axis=1)
    glc = gl1[:, WIDTH:]
    mix = (_sigmoid(gla) * jnp.dot(oa.astype(BF16), wa_ref[...], preferred_element_type=F32)
           + _sigmoid(glb) * jnp.dot(op_ref[...].astype(BF16), wb_ref[...], preferred_element_type=F32)
           + _sigmoid(glc) * jnp.dot(oc.astype(BF16), wc_ref[...], preferred_element_type=F32))
    xn = x_ref[...] + m_ref[2:3, :] * jnp.dot(mix.astype(BF16), wo_ref[...], preferred_element_type=F32)
    xn_ref[...] = xn
    y = xn * lax.rsqrt(jnp.mean(xn * xn, axis=-1, keepdims=True) + EPS) * nf_ref[...]
    h2 = y * (1.0 + m_ref[4:5, :]) + m_ref[3:4, :]
    h2_ref[...] = h2.astype(h2_ref.dtype)
    if not with_router:
        return
    hhi = h2.astype(BF16)
    hlo = (h2 - hhi.astype(F32)).astype(BF16)
    logits = (jnp.dot(hhi, rhi_ref[...], preferred_element_type=F32)
              + jnp.dot(hhi, rlo_ref[...], preferred_element_type=F32)
              + jnp.dot(hlo, rhi_ref[...], preferred_element_type=F32))
    lane = lax.broadcasted_iota(jnp.int32, logits.shape, 1).astype(F32)
    lg = jnp.where(lane < N_EXP, logits, -jnp.inf)
    m1 = jnp.max(lg, axis=-1, keepdims=True)
    i1 = jnp.min(jnp.where(lg == m1, lane, 128.0), axis=-1, keepdims=True)
    lg2 = jnp.where(lane == i1, -jnp.inf, lg)
    m2 = jnp.max(lg2, axis=-1, keepdims=True)
    i2 = jnp.min(jnp.where(lg2 == m2, lane, 128.0), axis=-1, keepdims=True)
    e = jnp.exp(m2 - m1)
    w1 = 1.0 / (1.0 + e)
    route_ref[...] = (jnp.where(lane == ROUTE_I1, i1, 0.0) + jnp.where(lane == ROUTE_I2, i2, 0.0)
                      + jnp.where(lane == ROUTE_W1, w1, 0.0) + jnp.where(lane == ROUTE_W2, e * w1, 0.0))


def _merge(ctx_parts, lat_parts, z, o_pool, x, mods, hgrn_norm_l, norm_ffn_l, wa, wb, wc, wo, router_split):
    bm = MERGE_BM
    glw = 1536
    nct = N_CTX // bm
    with_router = router_split is not None
    row = lambda w, col=0: pl.BlockSpec((bm, w), lambda i: (i, col))
    ctx_row = pl.BlockSpec((bm, WIDTH), lambda i: (jnp.minimum(i, nct - 1), 0))
    lat_row = pl.BlockSpec((bm, WIDTH), lambda i: (jnp.maximum(i - nct, 0), 0))
    const = lambda shape: pl.BlockSpec(shape, lambda i: (0,) * len(shape))
    in_specs = [ctx_row] * 3 + [lat_row] * 3 + [
        row(WIDTH, COL_AG), row(WIDTH),
        row(glw, GL_COL0 // glw), row(glw, GL_COL0 // glw + 1), row(D),
        pl.BlockSpec((None, 6, D), lambda i: (_group_of_rows(i * bm), 0, 0)),
        const((1, WIDTH)), const((1, D)),
        const((WIDTH, D)), const((WIDTH, D)), const((WIDTH, D)), const((D, D))]
    args = list(ctx_parts) + list(lat_parts) + [
        z, o_pool, z, z, x, mods, jnp.tile(hgrn_norm_l, HG_H).reshape(1, WIDTH),
        norm_ffn_l.reshape(1, D), wa, wb, wc, wo]
    out_specs = [row(D), row(D)]
    out_shape = [jax.ShapeDtypeStruct((N_TOK, D), F32),
                 jax.ShapeDtypeStruct((N_TOK, D), F32 if with_router else BF16)]
    if with_router:
        in_specs += [const((D, 128)), const((D, 128))]
        args += list(router_split)
        out_specs.append(row(128))
        out_shape.append(jax.ShapeDtypeStruct((N_TOK, 128), F32))
    return pl.pallas_call(
        functools.partial(_merge_kernel, with_router=with_router),
        grid=(N_TOK // bm,),
        in_specs=in_specs,
        out_specs=out_specs,
        out_shape=out_shape,
        compiler_params=_cparams(("parallel",)),
        name="merge_route" if with_router else "merge",
    )(*args)


def _ffn_kernel(h_ref, x_ref, m_ref, wa_ref, wb_ref, wo_ref, o_ref, acc_scr):
    f = pl.program_id(1)

    @pl.when(f == 0)
    def _():
        acc_scr[...] = jnp.zeros(acc_scr.shape, F32)

    h = h_ref[...]
    a = jnp.dot(h, wa_ref[...].astype(BF16), preferred_element_type=F32)
    b = jnp.dot(h, wb_ref[...].astype(BF16), preferred_element_type=F32)
    g = (_silu(a) * b).astype(BF16)
    acc_scr[...] += jnp.dot(g, wo_ref[...].astype(BF16), preferred_element_type=F32)

    @pl.when(f == pl.num_programs(1) - 1)
    def _():
        o_ref[...] = x_ref[...] + m_ref[5:6, :] * acc_scr[...]


def _ffn(h2, x, mods, w_in, w_out):
    bm, tf = 1024, 256
    nf = F_DENSE // tf
    return pl.pallas_call(
        _ffn_kernel,
        grid=(N_TOK // bm, nf),
        in_specs=[pl.BlockSpec((bm, D), lambda i, f: (i, 0)),
                  pl.BlockSpec((bm, D), lambda i, f: (i, 0)),
                  pl.BlockSpec((None, 6, D), lambda i, f: (_group_of_rows(i * bm), 0, 0)),
                  pl.BlockSpec((None, D, tf), lambda i, f: (0, 0, f)),
                  pl.BlockSpec((None, D, tf), lambda i, f: (0, 0, nf + f)),
                  pl.BlockSpec((None, tf, D), lambda i, f: (0, f, 0))],
        out_specs=pl.BlockSpec((bm, D), lambda i, f: (i, 0)),
        out_shape=jax.ShapeDtypeStruct((N_TOK, D), F32),
        scratch_shapes=[pltpu.VMEM((bm, D), F32)],
        compiler_params=_cparams(("parallel", "arbitrary")),
        name="ffn",
    )(h2, x, mods, w_in, w_in, w_out)


MOE_BM = 1024
MOE_TF = 512
MOE_ROWS = 2 * N_TOK + N_EXP * MOE_BM
MOE_TILES = MOE_ROWS // MOE_BM


def _moe_routing(route):
    i1 = route[:, ROUTE_I1].astype(jnp.int32)
    i2 = route[:, ROUTE_I2].astype(jnp.int32)
    ep = jnp.stack([i1, i2], axis=1).reshape(-1)
    onehot = (ep[:, None] == jnp.arange(N_EXP, dtype=jnp.int32)[None, :]).astype(jnp.int32)
    csum = jnp.cumsum(onehot, axis=0)
    counts = csum[-1]
    padded = ((counts + MOE_BM - 1) // MOE_BM) * MOE_BM
    ends = jnp.cumsum(padded)
    starts = ends - padded
    dest = jnp.sum(onehot * (csum - 1 + starts[None, :]), axis=1)
    tile_row0 = jnp.arange(MOE_TILES, dtype=jnp.int32) * MOE_BM
    tile_active = (tile_row0 < ends[-1]).astype(jnp.int32)
    last_row0 = jnp.maximum(ends[-1] - MOE_BM, 0)
    tile_expert = jnp.sum((jnp.minimum(tile_row0, last_row0)[:, None] >= ends[None, :]).astype(jnp.int32), axis=1)
    return dest, jnp.minimum(tile_expert, N_EXP - 1), tile_active


def _row(ref, r):
    return ref.at[pl.ds(r, 1), :]


def _moe_scatter_kernel(dest_ref, x_ref, _init, xs_ref, sem):
    n = x_ref.shape[0]

    def issue(j, carry):
        for k in range(2):
            pltpu.make_async_copy(_row(x_ref, j), _row(xs_ref, dest_ref[0, 2 * j + k]), sem).start()
        return carry

    lax.fori_loop(0, n, issue, 0, unroll=8)

    def drain(j, carry):
        for k in range(2):
            pltpu.make_async_copy(_row(x_ref, 0), _row(xs_ref, 0), sem).wait()
        return carry

    lax.fori_loop(0, n, drain, 0, unroll=8)


def _moe_scatter(h2, dest):
    bm = 512
    nt = N_TOK // bm
    return pl.pallas_call(
        _moe_scatter_kernel,
        grid=(nt,),
        in_specs=[pl.BlockSpec((None, 1, 2 * bm), lambda i: (i, 0, 0), memory_space=pltpu.SMEM),
                  pl.BlockSpec((bm, D), lambda i: (i, 0)),
                  pl.BlockSpec(memory_space=pl.ANY)],
        out_specs=pl.BlockSpec(memory_space=pl.ANY),
        out_shape=jax.ShapeDtypeStruct((MOE_ROWS, D), F32),
        scratch_shapes=[pltpu.SemaphoreType.DMA(())],
        input_output_aliases={2: 0},
        compiler_params=_cparams(("arbitrary",)),
        name="moe_scatter",
    )(dest.reshape(nt, 1, 2 * bm), h2, jnp.zeros((MOE_ROWS, D), F32))


def _moe_group_kernel(te_ref, ta_ref, xs_ref, wa_ref, wb_ref, wo_ref, y_ref, h_scr):
    del te_ref
    i = pl.program_id(0)
    f = pl.program_id(1)
    active = ta_ref[i] == 1

    @pl.when(f == 0)
    def _():
        h_scr[...] = xs_ref[...].astype(BF16)
        y_ref[...] = jnp.zeros(y_ref.shape, F32)

    @pl.when(active)
    def _():
        h = h_scr[...]
        a = jnp.dot(h, wa_ref[...].astype(BF16), preferred_element_type=F32)
        b = jnp.dot(h, wb_ref[...].astype(BF16), preferred_element_type=F32)
        g = (_silu(a) * b).astype(BF16)
        y_ref[...] += jnp.dot(g, wo_ref[...].astype(BF16), preferred_element_type=F32)


def _moe_group(xs, tile_expert, tile_active, w_in, w_out):
    nf = F_EXP // MOE_TF

    def fblk(i, f, ta):
        return jnp.where(ta[i] == 1, f, nf - 1)

    grid_spec = pltpu.PrefetchScalarGridSpec(
        num_scalar_prefetch=2,
        grid=(MOE_TILES, nf),
        in_specs=[pl.BlockSpec((MOE_BM, D), lambda i, f, te, ta: (i, 0)),
                  pl.BlockSpec((None, None, D, MOE_TF), lambda i, f, te, ta: (0, te[i], 0, fblk(i, f, ta))),
                  pl.BlockSpec((None, None, D, MOE_TF), lambda i, f, te, ta: (0, te[i], 0, nf + fblk(i, f, ta))),
                  pl.BlockSpec((None, None, MOE_TF, D), lambda i, f, te, ta: (0, te[i], fblk(i, f, ta), 0))],
        out_specs=pl.BlockSpec((MOE_BM, D), lambda i, f, te, ta: (i, 0)),
        scratch_shapes=[pltpu.VMEM((MOE_BM, D), BF16)])
    return pl.pallas_call(
        _moe_group_kernel,
        grid_spec=grid_spec,
        out_shape=jax.ShapeDtypeStruct((MOE_ROWS, D), F32),
        compiler_params=_cparams(("parallel", "arbitrary")),
        name="moe_group",
    )(tile_expert, tile_active, xs, w_in, w_in, w_out)


def _moe_combine_kernel(dest_ref, route_ref, x_ref, m_ref, nf_ref, ys_ref, out_ref, ybuf, sem):
    n = x_ref.shape[0]

    def issue(j, carry):
        for k in range(2):
            pltpu.make_async_copy(_row(ys_ref, dest_ref[0, 2 * j + k]), _row(ybuf, k * n + j), sem).start()
        return carry

    lax.fori_loop(0, n, issue, 0, unroll=8)

    def drain(j, carry):
        for k in range(2):
            pltpu.make_async_copy(_row(ys_ref, 0), _row(ybuf, 0), sem).wait()
        return carry

    lax.fori_loop(0, n, drain, 0, unroll=8)

    route = route_ref[...]
    f = (ybuf[0:n, :] * route[:, ROUTE_W1:ROUTE_W1 + 1]
         + ybuf[n:2 * n, :] * route[:, ROUTE_W2:ROUTE_W2 + 1])
    xn = x_ref[...] + m_ref[5:6, :] * f
    out_ref[...] = xn * lax.rsqrt(jnp.mean(xn * xn, axis=-1, keepdims=True) + EPS) * nf_ref[...]


def _moe_combine(ys, dest, route, x, mods, norm_final):
    bm = 256
    nt = N_TOK // bm
    return pl.pallas_call(
        _moe_combine_kernel,
        grid=(nt,),
        in_specs=[pl.BlockSpec((None, 1, 2 * bm), lambda i: (i, 0, 0), memory_space=pltpu.SMEM),
                  pl.BlockSpec((bm, 128), lambda i: (i, 0)),
                  pl.BlockSpec((bm, D), lambda i: (i, 0)),
                  pl.BlockSpec((None, 6, D), lambda i: (_group_of_rows(i * bm), 0, 0)),
                  pl.BlockSpec((1, D), lambda i: (0, 0)),
                  pl.BlockSpec(memory_space=pl.ANY)],
        out_specs=pl.BlockSpec((bm, D), lambda i: (i, 0)),
        out_shape=jax.ShapeDtypeStruct((N_TOK, D), F32),
        scratch_shapes=[pltpu.VMEM((2 * bm, D), F32), pltpu.SemaphoreType.DMA(())],
        compiler_params=_cparams(("arbitrary",)),
        name="moe_combine",
    )(dest.reshape(nt, 1, 2 * bm), route, x, mods, norm_final.reshape(1, D), ys)


def _moe(h2, route, x, mods, w_in, w_out, norm_final):
    dest, tile_expert, tile_active = _moe_routing(route)
    xs = _moe_scatter(h2, dest)
    ys = _moe_group(xs, tile_expert, tile_active, w_in, w_out)
    return _moe_combine(ys, dest, route, x, mods, norm_final)


def _hgrn_lower_bounds(lb_param):
    sm = jax.nn.softmax(lb_param.astype(F32), axis=0)
    return jnp.cumsum(sm, axis=0) - sm[0:1]


def kernel(x_prompt, x_sample, cache_k, cache_v, state_hgrn, c, c_ctx, w_ada, b_ada, norm_mix, norm_ffn,
           w_in, hgrn_lb, hgrn_norm, pool_w, pool_scale, rpb, w_branch_a, w_branch_b, w_branch_c, w_out,
           ffn_w_in, ffn_w_out, router, moe_w_in, moe_w_out, norm_final):
    x = jnp.concatenate([x_prompt.reshape(N_CTX, D), x_sample.reshape(N_LAT, D)], axis=0)
    cond8 = jnp.concatenate([c_ctx[None], c, jnp.zeros((5, D), F32)], axis=0)
    mods = _adaln(cond8, w_ada, b_ada)[:, :3].reshape(DEPTH, 3, 6, D)
    lbs = _hgrn_lower_bounds(hgrn_lb)
    ck = cache_k.reshape(N_LAT_B, DEPTH, PAST, WIDTH).astype(BF16)
    cv = cache_v.reshape(N_LAT_B, DEPTH, PAST, WIDTH).astype(BF16)
    w_in16 = w_in.astype(BF16)
    router_pad = jnp.pad(router[0], ((0, 0), (0, 128 - N_EXP)))
    r_hi = router_pad.astype(BF16)
    r_lo = (router_pad - r_hi.astype(F32)).astype(BF16)

    kvs, ss = [], []
    for l in range(DEPTH):
        z, zg, kv = _inproj(x, norm_mix[l], mods[l], w_in16, l)
        of_c, ob_c, s_ctx = _hgrn(z, zg, lbs[l], None, 0, N_CTX_B, CTX_T)
        of_l, ob_l, _ = _hgrn(z, zg, lbs[l], state_hgrn[:, l], N_CTX, N_LAT_B, LAT_T)
        o_pool = _pool(z, pool_w[l].astype(BF16), pool_scale[l])
        oc_c = _ctx_attn(z)
        oc_l = _na_attn(z, ck, cv, _na_bias_table(rpb[l]), l)
        merged = _merge((of_c, ob_c, oc_c), (of_l, ob_l, oc_l), z, o_pool, x, mods[l], hgrn_norm[l], norm_ffn[l],
                        w_branch_a[l].astype(BF16), w_branch_b[l].astype(BF16),
                        w_branch_c[l].astype(BF16), w_out[l].astype(BF16),
                        (r_hi, r_lo) if l % 2 == 1 else None)
        if l % 2 == 0:
            x, h2 = merged
            x = _ffn(h2, x, mods[l], ffn_w_in, ffn_w_out)
        else:
            x, h2, route = merged
            x = _moe(h2, route, x, mods[l], moe_w_in, moe_w_out, norm_final)
        kvs.append(kv.reshape(N_CTX_B, CTX_T, 2, NA_H, NA_D))
        ss.append(s_ctx)
    kv = jnp.stack(kvs, axis=1)
    y_prompt = x[:N_CTX].reshape(N_CTX_B, CTX_T, D)
    y_sample = x[N_CTX:].reshape(N_LAT_B, LAT_T, D)
    return (y_prompt, y_sample, kv[:, :, :, 0], kv[:, :, :, 1], jnp.stack(ss, axis=1))
```

```python
import functools

import numpy as np
import jax
import jax.numpy as jnp
from jax import lax
from jax.experimental import pallas as pl
from jax.experimental.pallas import tpu as pltpu

F32 = jnp.float32
BF16 = jnp.bfloat16

D = 1024
N_CTX_B, CTX_T = 32, 256
N_LAT_B, LAT_T = 2, 4096
N_CTX = N_CTX_B * CTX_T
N_LAT = N_LAT_B * LAT_T
N_TOK = N_CTX + N_LAT
DEPTH = 2
GRID_W = 64
GRID_ROWS = LAT_T // GRID_W
PAST = 512
HG_H, HG_D = 4, 128
WIDTH = 512
NA_H, NA_D = 8, 64
WIN_ROWS, WIN_COLS = 8, 16
POOL_WINDOWS = (2, 4, 8, 16)
IN_COLS = 7680
F_DENSE = 2816
N_EXP = 8
F_EXP = 3584
GATE_CLIP = 30.0
EPS = 1e-6
NEG_BIG = -1e30

COL_AQ, COL_AFF, COL_AFB, COL_AI, COL_AG, COL_BU, COL_CQ, COL_CK, COL_CV = range(9)
GL_COL0 = 9 * WIDTH

ROUTE_I1, ROUTE_I2, ROUTE_W1, ROUTE_W2 = 8, 9, 10, 11

VMEM_LIMIT = 56 * 1024 * 1024


def _cparams(sem):
    return pltpu.CompilerParams(dimension_semantics=sem, vmem_limit_bytes=VMEM_LIMIT)


def _sigmoid(x):
    return 1.0 / (1.0 + jnp.exp(-x))


def _silu(x):
    return x / (1.0 + jnp.exp(-x))


def _group_of_rows(row0):
    return jnp.maximum(row0 - N_CTX + LAT_T, 0) // LAT_T


def _adaln_kernel(c_ref, w_ref, b_ref, o_ref):
    s = _silu(c_ref[...]).astype(BF16)
    o_ref[...] = jnp.dot(s, w_ref[...].astype(BF16), preferred_element_type=F32) + b_ref[...]


def _adaln(cond8, w_ada, b_ada):
    tn = 1536
    return pl.pallas_call(
        _adaln_kernel,
        grid=(DEPTH, 6 * D // tn),
        in_specs=[pl.BlockSpec((8, D), lambda l, j: (0, 0)),
                  pl.BlockSpec((None, D, tn), lambda l, j: (l, 0, j)),
                  pl.BlockSpec((None, 1, tn), lambda l, j: (l, 0, j))],
        out_specs=pl.BlockSpec((None, 8, tn), lambda l, j: (l, 0, j)),
        out_shape=jax.ShapeDtypeStruct((DEPTH, 8, 6 * D), F32),
        compiler_params=_cparams(("parallel", "parallel")),
        name="adaln",
    )(cond8, w_ada, b_ada.reshape(DEPTH, 1, 6 * D))


INPROJ_BM, INPROJ_BN = 1024, 1536
GATE_TILE = (COL_AFF * WIDTH) // INPROJ_BN
KV_TILE = (COL_CK * WIDTH) // INPROJ_BN


def _stream_rows(x, bm):
    if not isinstance(x, tuple):
        specs = [pl.BlockSpec((bm, D), lambda i, *_: (i, 0))]
        return specs, [x]
    nct = N_CTX // bm
    specs = [pl.BlockSpec((bm, D), lambda i, *_: (jnp.minimum(i, nct - 1), 0)),
             pl.BlockSpec((bm, D), lambda i, *_: (jnp.maximum(i - nct, 0), 0))]
    return specs, list(x)


def _read_stream_rows(x_refs, bm):
    if len(x_refs) == 1:
        return x_refs[0][...]
    return jnp.where(pl.program_id(0) < N_CTX // bm, x_refs[0][...], x_refs[1][...])


def _inproj_kernel(*refs):
    nw_ref, m_ref, w_ref, z_ref, zg_ref, kv_ref, h_scr = refs[-7:]
    x_refs = refs[:-7]
    i = pl.program_id(0)
    j = pl.program_id(1)

    @pl.when(j == 0)
    def _():
        x = _read_stream_rows(x_refs, INPROJ_BM)
        y = x * lax.rsqrt(jnp.mean(x * x, axis=-1, keepdims=True) + EPS) * nw_ref[...]
        h_scr[...] = (y * (1.0 + m_ref[1:2, :]) + m_ref[0:1, :]).astype(BF16)

    acc = jnp.dot(h_scr[...], w_ref[...], preferred_element_type=F32)
    z_ref[...] = acc.astype(BF16)

    @pl.when(j == GATE_TILE)
    def _():
        c0 = COL_AFF * WIDTH - GATE_TILE * INPROJ_BN
        zg_ref[...] = acc[:, c0:c0 + 2 * WIDTH]

    @pl.when((j == KV_TILE) & (i < N_CTX // INPROJ_BM))
    def _():
        c0 = COL_CK * WIDTH - KV_TILE * INPROJ_BN
        kv_ref[...] = acc[:, c0:c0 + 2 * WIDTH]


def _inproj(x, norm_w, mods, w_in16, layer):
    bm, bn = INPROJ_BM, INPROJ_BN
    nct = N_CTX // bm
    x_specs, x_args = _stream_rows(x, bm)
    return pl.pallas_call(
        _inproj_kernel,
        grid=(N_TOK // bm, IN_COLS // bn),
        in_specs=x_specs + [
            pl.BlockSpec((1, D), lambda i, j: (0, 0)),
            pl.BlockSpec((None, 6, D), lambda i, j: (_group_of_rows(i * bm), 0, 0)),
            pl.BlockSpec((None, D, bn), lambda i, j: (layer, 0, j))],
        out_specs=[pl.BlockSpec((bm, bn), lambda i, j: (i, j)),
                   pl.BlockSpec((bm, 2 * WIDTH), lambda i, j: (i, 0)),
                   pl.BlockSpec((bm, 2 * WIDTH), lambda i, j: (jnp.minimum(i, nct - 1), 0))],
        out_shape=[jax.ShapeDtypeStruct((N_TOK, IN_COLS), BF16),
                   jax.ShapeDtypeStruct((N_TOK, 2 * WIDTH), F32),
                   jax.ShapeDtypeStruct((N_CTX, 2 * WIDTH), F32)],
        scratch_shapes=[pltpu.VMEM((bm, D), BF16)],
        compiler_params=_cparams(("arbitrary", "arbitrary")),
        name="inproj",
    )(*x_args, norm_w.reshape(1, D), mods, w_in16)


HG_C = 128
HG_LEVELS = (4, 8, 16, 32, 64, 128)
LOG2_E = 1.4426950408889634
EXP2_CAP = 120.0


def _hgrn_level_ids(reverse):
    t = np.arange(HG_C)[:, None]
    s = np.arange(HG_C)[None, :]
    if reverse:
        t, s = s, t
    lev = np.full((HG_C, HG_C), -1, np.int32)
    lev[(t // 4 == s // 4) & (s <= t)] = 0
    for li, L in enumerate(HG_LEVELS[1:], start=1):
        m = (t // L == s // L) & (t % L >= L // 2) & (s % L < L // 2)
        lev[m] = li
    return lev


def _hgrn_ref_rows(b_scr, d, reverse):
    out = []
    r_lo, r_hi = (2, 6) if reverse else (1, 5)
    sub = lax.broadcasted_iota(jnp.int32, (8, WIDTH), 0)
    pieces = []
    for g in range(HG_C // 8):
        lo = jnp.broadcast_to(b_scr[d, 8 * g + r_lo:8 * g + r_lo + 1, :], (8, WIDTH))
        hi = jnp.broadcast_to(b_scr[d, 8 * g + r_hi:8 * g + r_hi + 1, :], (8, WIDTH))
        pieces.append(jnp.where(sub < 4, lo, hi))
    out.append(jnp.concatenate(pieces, axis=0))
    for L in HG_LEVELS[1:]:
        r = L // 2 - 1 if reverse else L // 2
        pieces = [jnp.broadcast_to(b_scr[d, L * g + r:L * g + r + 1, :], (L, WIDTH))
                  for g in range(HG_C // L)]
        out.append(pieces[0] if len(pieces) == 1 else jnp.concatenate(pieces, axis=0))
    return out


def _hgrn_kernel(*refs, has_init):
    if has_init:
        (qf_ref, ff_ref, vf_ref, qb_ref, fb_ref, vb_ref, lb_ref, levf_ref, levb_ref, s0_ref,
         of_ref, ob_ref, so_ref, st_scr, b_scr) = refs
    else:
        (qf_ref, ff_ref, vf_ref, qb_ref, fb_ref, vb_ref, lb_ref, levf_ref, levb_ref,
         of_ref, ob_ref, so_ref, st_scr, b_scr) = refs
    c = pl.program_id(1)
    C = HG_C

    @pl.when(c == 0)
    def _():
        for d in range(2):
            for h in range(HG_H):
                if has_init:
                    st_scr[d, h] = s0_ref[d, h].T
                else:
                    st_scr[d, h] = jnp.zeros((HG_D, HG_D), F32)

    row = lax.broadcasted_iota(jnp.int32, (C, C), 0)
    col = lax.broadcasted_iota(jnp.int32, (C, C), 1)
    dirs = ((qf_ref, ff_ref, vf_ref, levf_ref, of_ref), (qb_ref, fb_ref, vb_ref, levb_ref, ob_ref))
    for d, (q_ref, f_ref, v_ref, lev_ref, o_ref) in enumerate(dirs):
        reverse = d == 1
        tri = jnp.where((col >= row) if reverse else (col <= row), 1.0, 0.0).astype(BF16)
        q = _silu(q_ref[...].astype(F32))
        fx = jnp.clip(f_ref[...], -GATE_CLIP, GATE_CLIP)
        e = jnp.exp(-fx)
        sig_pos = 1.0 / (1.0 + e)
        sig_neg = e * sig_pos
        lb = lb_ref[d:d + 1, :]
        lf = jnp.log(lb + (1.0 - lb) * sig_pos) * LOG2_E
        k = (1.0 - lb) * sig_neg
        q16 = q.astype(BF16)
        k16 = k.astype(BF16)
        hi = lf.astype(BF16)
        r1 = lf - hi.astype(F32)
        mid = r1.astype(BF16)
        lo = (r1 - mid.astype(F32)).astype(BF16)
        b = (jnp.dot(tri, hi, preferred_element_type=F32)
             + jnp.dot(tri, mid, preferred_element_type=F32)
             + jnp.dot(tri, lo, preferred_element_type=F32))
        b_scr[d] = b
        refs_m = _hgrn_ref_rows(b_scr, d, reverse)
        qs, ks = [], []
        for li, m in enumerate(refs_m):
            dlt = b - m
            if li == 0:
                qs.append(q16 * jnp.exp2(jnp.minimum(dlt, EXP2_CAP)).astype(BF16))
                ks.append(k16 * jnp.exp2(jnp.minimum(-dlt, EXP2_CAP)).astype(BF16))
            else:
                fac = jnp.exp2(-jnp.abs(dlt)).astype(BF16)
                qs.append(q16 * fac)
                ks.append(k16 * fac)
        lev = lev_ref[...]
        b_end = b[0:1, :] if reverse else b[C - 1:C, :]
        q_in = q16 * jnp.exp2(b).astype(BF16)
        k_out = k16 * jnp.exp2(b_end - b).astype(BF16)
        dec = jnp.exp2(b_end)
        vb16 = v_ref[...]
        for h in range(HG_H):
            sl = slice(h * HG_D, (h + 1) * HG_D)
            a = jnp.zeros((C, C), F32)
            for li in range(len(HG_LEVELS)):
                p = lax.dot_general(qs[li][:, sl], ks[li][:, sl], (((1,), (1,)), ((), ())),
                                    preferred_element_type=F32)
                a = jnp.where(lev == li, p, a)
            vh = vb16[:, sl]
            st = st_scr[d, h]
            o = (jnp.dot(a.astype(BF16), vh, preferred_element_type=F32)
                 + lax.dot_general(q_in[:, sl], st.astype(BF16), (((1,), (1,)), ((), ())),
                                   preferred_element_type=F32))
            o_ref[:, sl] = o
            upd = lax.dot_general(vh, k_out[:, sl], (((0,), (0,)), ((), ())),
                                  preferred_element_type=F32)
            st_scr[d, h] = st * dec[:, sl] + upd

    @pl.when(c == pl.num_programs(1) - 1)
    def _():
        for d in range(2):
            for h in range(HG_H):
                so_ref[d, h] = st_scr[d, h].T


def _hgrn(z, zg, lb, s0, row_off, n_b, seq_t):
    C = HG_C
    nc = seq_t // C
    base = row_off // C
    has_init = s0 is not None

    def fwd(col):
        return pl.BlockSpec((C, WIDTH), lambda b, c: (base + b * nc + c, col))

    def bwd(col):
        return pl.BlockSpec((C, WIDTH), lambda b, c: (base + b * nc + nc - 1 - c, col))

    full = lambda shape: pl.BlockSpec(shape, lambda b, c: (0,) * len(shape))
    in_specs = [fwd(COL_AQ), fwd(0), fwd(COL_AI), bwd(COL_AQ), bwd(1), bwd(COL_AI),
                full((2, WIDTH)), full((C, C)), full((C, C))]
    args = [z, zg, z, z, zg, z, lb, jnp.asarray(_hgrn_level_ids(False)), jnp.asarray(_hgrn_level_ids(True))]
    if has_init:
        in_specs.append(pl.BlockSpec((None, 2, HG_H, HG_D, HG_D), lambda b, c: (b, 0, 0, 0, 0)))
        args.append(s0)
    return pl.pallas_call(
        functools.partial(_hgrn_kernel, has_init=has_init),
        grid=(n_b, nc),
        in_specs=in_specs,
        out_specs=[pl.BlockSpec((C, WIDTH), lambda b, c: (b * nc + c, 0)),
                   pl.BlockSpec((C, WIDTH), lambda b, c: (b * nc + nc - 1 - c, 0)),
                   pl.BlockSpec((None, 2, HG_H, HG_D, HG_D), lambda b, c: (b, 0, 0, 0, 0))],
        out_shape=[jax.ShapeDtypeStruct((n_b * seq_t, WIDTH), F32),
                   jax.ShapeDtypeStruct((n_b * seq_t, WIDTH), F32),
                   jax.ShapeDtypeStruct((n_b, 2, HG_H, HG_D, HG_D), F32)],
        scratch_shapes=[pltpu.VMEM((2, HG_H, HG_D, HG_D), F32),
                        pltpu.VMEM((2, C, WIDTH), F32)],
        compiler_params=_cparams(("parallel", "arbitrary")),
        name="hgrn_lat" if has_init else "hgrn_ctx",
    )(*args)


POOL_T = 256
POOL_HALO = 16


def _pool_kernel(u_ref, up_ref, un_ref, w_ref, sc_ref, o_ref):
    i = pl.program_id(0)
    is_lat = i >= N_CTX // POOL_T
    seq_t = jnp.where(is_lat, LAT_T, CTX_T)
    t0 = jnp.where(is_lat, ((i - N_CTX // POOL_T) % (LAT_T // POOL_T)) * POOL_T, 0)
    TT, HL = POOL_T, POOL_HALO
    diff = lax.broadcasted_iota(jnp.int32, (TT, TT), 1) - lax.broadcasted_iota(jnp.int32, (TT, TT), 0)
    diff_h = lax.broadcasted_iota(jnp.int32, (TT, HL), 1) - lax.broadcasted_iota(jnp.int32, (TT, HL), 0)
    t_glob = t0 + lax.broadcasted_iota(jnp.int32, (TT, 128), 0)
    has_prev = t0 > 0
    has_next = t0 + TT < seq_t
    for g, w in enumerate(POOL_WINDOWS):
        half = w // 2
        sl = slice(g * 128, (g + 1) * 128)
        band = jnp.where((diff >= -half) & (diff < half), 1.0, 0.0).astype(BF16)
        dp = diff_h - HL
        band_p = jnp.where((dp >= -half) & (dp < half) & has_prev, 1.0, 0.0).astype(BF16)
        dn = diff_h + TT
        band_n = jnp.where((dn >= -half) & (dn < half) & has_next, 1.0, 0.0).astype(BF16)
        u = u_ref[:, sl]
        s = (jnp.dot(band, u, preferred_element_type=F32)
             + jnp.dot(band_p, up_ref[:, sl], preferred_element_type=F32)
             + jnp.dot(band_n, un_ref[:, sl], preferred_element_type=F32))
        cnt = (jnp.minimum(t_glob + half, seq_t) - jnp.maximum(t_glob - half, 0)).astype(F32)
        dd = s / cnt - u.astype(F32)
        y = jnp.dot(dd.astype(BF16), w_ref[g], preferred_element_type=F32)
        o_ref[:, sl] = y * sc_ref[:, sl]


def _pool(z, pool_w16, pool_scale):
    nt = N_TOK // POOL_T
    per = POOL_T // POOL_HALO
    return pl.pallas_call(
        _pool_kernel,
        grid=(nt,),
        in_specs=[pl.BlockSpec((POOL_T, WIDTH), lambda i: (i, COL_BU)),
                  pl.BlockSpec((POOL_HALO, WIDTH), lambda i: (jnp.maximum(i * per - 1, 0), COL_BU)),
                  pl.BlockSpec((POOL_HALO, WIDTH), lambda i: (jnp.minimum((i + 1) * per, nt * per - 1), COL_BU)),
                  pl.BlockSpec((4, 128, 128), lambda i: (0, 0, 0)),
                  pl.BlockSpec((1, WIDTH), lambda i: (0, 0))],
        out_specs=pl.BlockSpec((POOL_T, WIDTH), lambda i: (i, 0)),
        out_shape=jax.ShapeDtypeStruct((N_TOK, WIDTH), F32),
        compiler_params=_cparams(("parallel",)),
        name="pool",
    )(z, z, z, pool_w16, pool_scale.reshape(1, WIDTH))


def _head_mask(hh):
    lane = lax.broadcasted_iota(jnp.int32, (1, 128), 1)
    in_head = (lane >= hh * NA_D) & (lane < (hh + 1) * NA_D)
    return jnp.where(in_head, NA_D ** -0.5, 0.0).astype(BF16)


def _ctx_attn_kernel(q_ref, k_ref, v_ref, o_ref):
    lane = lax.broadcasted_iota(jnp.int32, (CTX_T, 128), 1)
    for j in range(NA_H // 2):
        sl = slice(j * 128, (j + 1) * 128)
        q = q_ref[:, sl]
        kt = k_ref[:, sl]
        vt = v_ref[:, sl]
        outs = []
        for hh in range(2):
            s = lax.dot_general(q * _head_mask(hh), kt, (((1,), (1,)), ((), ())), preferred_element_type=F32)
            m = jnp.max(s, axis=-1, keepdims=True)
            p = jnp.exp(s - m)
            l = jnp.sum(p, axis=-1, keepdims=True)
            outs.append(jnp.dot(p.astype(BF16), vt, preferred_element_type=F32) / l)
        o_ref[:, sl] = jnp.where(lane < NA_D, outs[0], outs[1])


def _ctx_attn(z):
    spec = lambda col: pl.BlockSpec((CTX_T, WIDTH), lambda b: (b, col))
    return pl.pallas_call(
        _ctx_attn_kernel,
        grid=(N_CTX_B,),
        in_specs=[spec(COL_CQ), spec(COL_CK), spec(COL_CV)],
        out_specs=pl.BlockSpec((CTX_T, WIDTH), lambda b: (b, 0)),
        out_shape=jax.ShapeDtypeStruct((N_CTX, WIDTH), F32),
        compiler_params=_cparams(("parallel",)),
        name="ctx_attn",
    )(z, z, z)


NA_QR = 4
NA_KR = 12
NA_NQ = NA_QR * GRID_W
NA_NK = NA_KR * GRID_W
NA_BLOCKS = GRID_ROWS // NA_QR


def _na_key_row0(blk):
    return jnp.clip(blk * NA_QR - WIN_ROWS // 2, 0, GRID_ROWS - NA_KR)


def _na_geometry():
    patterns, var_of_block = [], []
    for blk in range(NA_BLOCKS):
        r = blk * NA_QR + np.arange(NA_QR)[:, None]
        kr = int(np.clip(blk * NA_QR - WIN_ROWS // 2, 0, GRID_ROWS - NA_KR)) + np.arange(NA_KR)[None, :]
        rs = np.clip(r - WIN_ROWS // 2, 0, GRID_ROWS - WIN_ROWS)
        valid = (kr >= rs) & (kr < rs + WIN_ROWS)
        assert (valid.sum(axis=1) == WIN_ROWS).all(), "key rows must cover every query row's window"
        drow = np.where(valid, kr - r + WIN_ROWS - 1, 0)
        key = (drow.tobytes(), valid.tobytes())
        ids = [i for i, (k_, _, _) in enumerate(patterns) if k_ == key]
        if not ids:
            patterns.append((key, drow, valid))
            ids = [len(patterns) - 1]
        var_of_block.append(ids[0])
    drow = np.stack([p[1] for p in patterns])
    valid = np.stack([p[2] for p in patterns])
    return np.asarray(var_of_block, np.int32), drow, valid


NA_DROWS = 2 * WIN_ROWS - 1


def _na_bias_kernel(didx_ref, t2_ref, o_ref):
    v = pl.program_id(0)
    for qr in range(NA_QR):
        for kp in range(NA_KR // 2):
            base = (v * NA_QR + qr) * NA_KR + 2 * kp
            pair = jnp.concatenate([t2_ref[didx_ref[base]], t2_ref[didx_ref[base + 1]]], axis=1)
            o_ref[qr * GRID_W:(qr + 1) * GRID_W, kp * 128:(kp + 1) * 128] = pair


def _na_bias_table(rpb_l):
    _, drow, valid = _na_geometry()
    n_var = valid.shape[0]
    qc = np.arange(GRID_W)[:, None]
    kc = np.arange(GRID_W)[None, :]
    q0 = np.clip(qc - WIN_COLS // 2, 0, GRID_W - WIN_COLS)
    col_in = (kc >= q0) & (kc < q0 + WIN_COLS)
    dcol = np.clip(kc - qc, -(WIN_COLS - 1), WIN_COLS - 1) + WIN_COLS - 1
    oh_col = (dcol[None] == np.arange(2 * WIN_COLS - 1)[:, None, None]).astype(np.float32)
    t2 = jnp.einsum('hdc,cqk->hdqk', rpb_l, jnp.asarray(oh_col), precision=lax.Precision.HIGHEST)
    t2 = jnp.where(col_in[None, None], t2, NEG_BIG)
    t2 = jnp.concatenate([t2, jnp.full((NA_H, 1, GRID_W, GRID_W), NEG_BIG, F32)], axis=1)
    didx = np.where(valid, drow, NA_DROWS).astype(np.int32).reshape(-1)
    grid_spec = pltpu.PrefetchScalarGridSpec(
        num_scalar_prefetch=1,
        grid=(n_var, NA_H),
        in_specs=[pl.BlockSpec((None, NA_DROWS + 1, GRID_W, GRID_W), lambda v, h, didx: (h, 0, 0, 0))],
        out_specs=pl.BlockSpec((None, None, NA_NQ, NA_NK), lambda v, h, didx: (v, h, 0, 0)))
    return pl.pallas_call(
        _na_bias_kernel,
        grid_spec=grid_spec,
        out_shape=jax.ShapeDtypeStruct((n_var, NA_H, NA_NQ, NA_NK), F32),
        compiler_params=_cparams(("parallel", "parallel")),
        name="na_bias",
    )(jnp.asarray(didx), t2)


def _na_attn_kernel(var_ref, q_ref, k_ref, v_ref, kc_ref, vc_ref, tbl_ref, o_ref):
    del var_ref
    blk = pl.program_id(1)
    k0 = pl.multiple_of(_na_key_row0(blk) * GRID_W, GRID_W)
    lane = lax.broadcasted_iota(jnp.int32, (NA_NQ, 128), 1)
    for j in range(NA_H // 2):
        sl = slice(j * 128, (j + 1) * 128)
        q = q_ref[:, sl]
        kt = k_ref[pl.ds(k0, NA_NK), sl]
        vt = v_ref[pl.ds(k0, NA_NK), sl]
        kct = kc_ref[:, sl]
        vct = vc_ref[:, sl]
        outs = []
        for hh in range(2):
            qm = q * _head_mask(hh)
            s_loc = lax.dot_general(qm, kt, (((1,), (1,)), ((), ())), preferred_element_type=F32)
            s_ctx = lax.dot_general(qm, kct, (((1,), (1,)), ((), ())), preferred_element_type=F32)
            tb = tbl_ref[2 * j + hh]
            s_loc = jnp.where(tb > 0.5 * NEG_BIG, s_loc + tb, NEG_BIG)
            m = jnp.maximum(jnp.max(s_loc, axis=-1, keepdims=True), jnp.max(s_ctx, axis=-1, keepdims=True))
            p_loc = jnp.exp(s_loc - m)
            p_ctx = jnp.exp(s_ctx - m)
            l = jnp.sum(p_loc, axis=-1, keepdims=True) + jnp.sum(p_ctx, axis=-1, keepdims=True)
            o = (jnp.dot(p_loc.astype(BF16), vt, preferred_element_type=F32)
                 + jnp.dot(p_ctx.astype(BF16), vct, preferred_element_type=F32))
            outs.append(o / l)
        o_ref[:, sl] = jnp.where(lane < NA_D, outs[0], outs[1])


def _na_attn(z, cache_k16, cache_v16, tbl, layer):
    base_q = N_CTX // NA_NQ
    base_t = N_CTX // LAT_T
    ctx_spec = pl.BlockSpec((None, None, PAST, WIDTH), lambda b, r, var: (b, layer, 0, 0))
    grid_spec = pltpu.PrefetchScalarGridSpec(
        num_scalar_prefetch=1,
        grid=(N_LAT_B, NA_BLOCKS),
        in_specs=[pl.BlockSpec((NA_NQ, WIDTH), lambda b, r, var: (base_q + b * NA_BLOCKS + r, COL_CQ)),
                  pl.BlockSpec((LAT_T, WIDTH), lambda b, r, var: (base_t + b, COL_CK)),
                  pl.BlockSpec((LAT_T, WIDTH), lambda b, r, var: (base_t + b, COL_CV)),
                  ctx_spec, ctx_spec,
                  pl.BlockSpec((None, NA_H, NA_NQ, NA_NK), lambda b, r, var: (var[r], 0, 0, 0))],
        out_specs=pl.BlockSpec((NA_NQ, WIDTH), lambda b, r, var: (b * NA_BLOCKS + r, 0)))
    return pl.pallas_call(
        _na_attn_kernel,
        grid_spec=grid_spec,
        out_shape=jax.ShapeDtypeStruct((N_LAT, WIDTH), F32),
        compiler_params=_cparams(("parallel", "arbitrary")),
        name="na_attn",
    )(jnp.asarray(_na_geometry()[0]), z, z, z, cache_k16, cache_v16, tbl)


MERGE_BM = 512


def _merge_kernel(*refs, with_router, n_x):
    x_refs, refs = refs[:n_x], refs[n_x:]
    (ofc_ref, obc_ref, occ_ref, ofl_ref, obl_ref, ocl_ref, ag_ref, op_ref, gl0_ref, gl1_ref, m_ref,
     hn_ref, nf_ref, wa_ref, wb_ref, wc_ref, wo_ref) = refs[:17]
    if with_router:
        rhi_ref, rlo_ref, xn_ref, h2_ref, route_ref = refs[17:]
    else:
        xn_ref, h2_ref = refs[17:]
    is_ctx = pl.program_id(0) < N_CTX // MERGE_BM
    o = jnp.where(is_ctx, ofc_ref[...] + obc_ref[...], ofl_ref[...] + obl_ref[...])
    oc = jnp.where(is_ctx, occ_ref[...], ocl_ref[...])
    parts = []
    for h in range(HG_H):
        oh = o[:, h * HG_D:(h + 1) * HG_D]
        parts.append(oh * lax.rsqrt(jnp.mean(oh * oh, axis=-1, keepdims=True) + EPS))
    oa = jnp.concatenate(parts, axis=1) * hn_ref[...] * _silu(ag_ref[...].astype(F32))
    sg0 = _sigmoid(gl0_ref[...]).astype(F32)
    sg1 = _sigmoid(gl1_ref[...]).astype(F32)
    sga = sg0[:, :D]
    sgb = jnp.concatenate([sg0[:, D:], sg1[:, :WIDTH]], axis=1)
    sgc = sg1[:, WIDTH:]
    mix = (sga * jnp.dot(oa.astype(BF16), wa_ref[...], preferred_element_type=F32)
           + sgb * jnp.dot(op_ref[...].astype(BF16), wb_ref[...], preferred_element_type=F32)
           + sgc * jnp.dot(oc.astype(BF16), wc_ref[...], preferred_element_type=F32))
    xn = (_read_stream_rows(x_refs, MERGE_BM)
          + m_ref[2:3, :] * jnp.dot(mix.astype(BF16), wo_ref[...], preferred_element_type=F32))
    xn_ref[...] = xn
    y = xn * lax.rsqrt(jnp.mean(xn * xn, axis=-1, keepdims=True) + EPS) * nf_ref[...]
    h2 = y * (1.0 + m_ref[4:5, :]) + m_ref[3:4, :]
    h2_ref[...] = h2.astype(h2_ref.dtype)
    if not with_router:
        return
    hhi = h2.astype(BF16)
    hlo = (h2 - hhi.astype(F32)).astype(BF16)
    logits = (jnp.dot(hhi, rhi_ref[...], preferred_element_type=F32)
              + jnp.dot(hhi, rlo_ref[...], preferred_element_type=F32)
              + jnp.dot(hlo, rhi_ref[...], preferred_element_type=F32))
    lane = lax.broadcasted_iota(jnp.int32, logits.shape, 1).astype(F32)
    lg = jnp.where(lane < N_EXP, logits, -jnp.inf)
    m1 = jnp.max(lg, axis=-1, keepdims=True)
    i1 = jnp.min(jnp.where(lg == m1, lane, 128.0), axis=-1, keepdims=True)
    lg2 = jnp.where(lane == i1, -jnp.inf, lg)
    m2 = jnp.max(lg2, axis=-1, keepdims=True)
    i2 = jnp.min(jnp.where(lg2 == m2, lane, 128.0), axis=-1, keepdims=True)
    e = jnp.exp(m2 - m1)
    w1 = 1.0 / (1.0 + e)
    route_ref[...] = (jnp.where(lane == ROUTE_I1, i1, 0.0) + jnp.where(lane == ROUTE_I2, i2, 0.0)
                      + jnp.where(lane == ROUTE_W1, w1, 0.0) + jnp.where(lane == ROUTE_W2, e * w1, 0.0))


def _merge(ctx_parts, lat_parts, z, o_pool, x, mods, hgrn_norm_l, norm_ffn_l, wa, wb, wc, wo, router_split):
    bm = MERGE_BM
    glw = 1536
    nct = N_CTX // bm
    with_router = router_split is not None
    row = lambda w, col=0: pl.BlockSpec((bm, w), lambda i: (i, col))
    ctx_row = pl.BlockSpec((bm, WIDTH), lambda i: (jnp.minimum(i, nct - 1), 0))
    lat_row = pl.BlockSpec((bm, WIDTH), lambda i: (jnp.maximum(i - nct, 0), 0))
    const = lambda shape: pl.BlockSpec(shape, lambda i: (0,) * len(shape))
    x_specs, x_args = _stream_rows(x, bm)
    in_specs = x_specs + [ctx_row] * 3 + [lat_row] * 3 + [
        row(WIDTH, COL_AG), row(WIDTH),
        row(glw, GL_COL0 // glw), row(glw, GL_COL0 // glw + 1),
        pl.BlockSpec((None, 6, D), lambda i: (_group_of_rows(i * bm), 0, 0)),
        const((1, WIDTH)), const((1, D)),
        const((WIDTH, D)), const((WIDTH, D)), const((WIDTH, D)), const((D, D))]
    args = x_args + list(ctx_parts) + list(lat_parts) + [
        z, o_pool, z, z, mods, jnp.tile(hgrn_norm_l, HG_H).reshape(1, WIDTH),
        norm_ffn_l.reshape(1, D), wa, wb, wc, wo]
    out_specs = [row(D), row(D)]
    out_shape = [jax.ShapeDtypeStruct((N_TOK, D), F32),
                 jax.ShapeDtypeStruct((N_TOK, D), F32 if with_router else BF16)]
    if with_router:
        in_specs += [const((D, 128)), const((D, 128))]
        args += list(router_split)
        out_specs.append(row(128))
        out_shape.append(jax.ShapeDtypeStruct((N_TOK, 128), F32))
    return pl.pallas_call(
        functools.partial(_merge_kernel, with_router=with_router, n_x=len(x_args)),
        grid=(N_TOK // bm,),
        in_specs=in_specs,
        out_specs=out_specs,
        out_shape=out_shape,
        compiler_params=_cparams(("parallel",)),
        name="merge_route" if with_router else "merge",
    )(*args)


def _ffn_kernel(h_ref, x_ref, m_ref, wa_ref, wb_ref, wo_ref, o_ref, acc_scr):
    f = pl.program_id(1)

    @pl.when(f == 0)
    def _():
        acc_scr[...] = jnp.zeros(acc_scr.shape, F32)

    h = h_ref[...]
    a = jnp.dot(h, wa_ref[...].astype(BF16), preferred_element_type=F32)
    b = jnp.dot(h, wb_ref[...].astype(BF16), preferred_element_type=F32)
    g = (_silu(a) * b).astype(BF16)
    acc_scr[...] += jnp.dot(g, wo_ref[...].astype(BF16), preferred_element_type=F32)

    @pl.when(f == pl.num_programs(1) - 1)
    def _():
        o_ref[...] = x_ref[...] + m_ref[5:6, :] * acc_scr[...]


def _ffn(h2, x, mods, w_in, w_out):
    bm, tf = 1024, 256
    nf = F_DENSE // tf
    return pl.pallas_call(
        _ffn_kernel,
        grid=(N_TOK // bm, nf),
        in_specs=[pl.BlockSpec((bm, D), lambda i, f: (i, 0)),
                  pl.BlockSpec((bm, D), lambda i, f: (i, 0)),
                  pl.BlockSpec((None, 6, D), lambda i, f: (_group_of_rows(i * bm), 0, 0)),
                  pl.BlockSpec((None, D, tf), lambda i, f: (0, 0, f)),
                  pl.BlockSpec((None, D, tf), lambda i, f: (0, 0, nf + f)),
                  pl.BlockSpec((None, tf, D), lambda i, f: (0, f, 0))],
        out_specs=pl.BlockSpec((bm, D), lambda i, f: (i, 0)),
        out_shape=jax.ShapeDtypeStruct((N_TOK, D), F32),
        scratch_shapes=[pltpu.VMEM((bm, D), F32)],
        compiler_params=_cparams(("parallel", "arbitrary")),
        name="ffn",
    )(h2, x, mods, w_in, w_in, w_out)


MOE_BM = 1024
MOE_TF = 512
MOE_ROWS = 2 * N_TOK + N_EXP * MOE_BM
MOE_TILES = MOE_ROWS // MOE_BM


def _moe_routing(route):
    i1 = route[:, ROUTE_I1].astype(jnp.int32)
    i2 = route[:, ROUTE_I2].astype(jnp.int32)
    ep = jnp.stack([i1, i2], axis=1).reshape(-1)
    onehot = (ep[:, None] == jnp.arange(N_EXP, dtype=jnp.int32)[None, :]).astype(jnp.int32)
    csum = jnp.cumsum(onehot, axis=0)
    counts = csum[-1]
    padded = ((counts + MOE_BM - 1) // MOE_BM) * MOE_BM
    ends = jnp.cumsum(padded)
    starts = ends - padded
    dest = jnp.sum(onehot * (csum - 1 + starts[None, :]), axis=1)
    tile_row0 = jnp.arange(MOE_TILES, dtype=jnp.int32) * MOE_BM
    tile_active = (tile_row0 < ends[-1]).astype(jnp.int32)
    last_row0 = jnp.maximum(ends[-1] - MOE_BM, 0)
    tile_expert = jnp.sum((jnp.minimum(tile_row0, last_row0)[:, None] >= ends[None, :]).astype(jnp.int32), axis=1)
    return dest, jnp.minimum(tile_expert, N_EXP - 1), tile_active


def _row(ref, r):
    return ref.at[pl.ds(r, 1), :]


def _moe_scatter_kernel(dest_ref, x_ref, _init, xs_ref, sem):
    n = x_ref.shape[0]

    def issue(j, carry):
        for k in range(2):
            pltpu.make_async_copy(_row(x_ref, j), _row(xs_ref, dest_ref[0, 2 * j + k]), sem).start()
        return carry

    lax.fori_loop(0, n, issue, 0, unroll=8)

    def drain(j, carry):
        for k in range(2):
            pltpu.make_async_copy(_row(x_ref, 0), _row(xs_ref, 0), sem).wait()
        return carry

    lax.fori_loop(0, n, drain, 0, unroll=8)


def _moe_scatter(h2, dest):
    bm = 512
    nt = N_TOK // bm
    return pl.pallas_call(
        _moe_scatter_kernel,
        grid=(nt,),
        in_specs=[pl.BlockSpec((None, 1, 2 * bm), lambda i: (i, 0, 0), memory_space=pltpu.SMEM),
                  pl.BlockSpec((bm, D), lambda i: (i, 0)),
                  pl.BlockSpec(memory_space=pl.ANY)],
        out_specs=pl.BlockSpec(memory_space=pl.ANY),
        out_shape=jax.ShapeDtypeStruct((MOE_ROWS, D), F32),
        scratch_shapes=[pltpu.SemaphoreType.DMA(())],
        input_output_aliases={2: 0},
        compiler_params=_cparams(("arbitrary",)),
        name="moe_scatter",
    )(dest.reshape(nt, 1, 2 * bm), h2, jnp.zeros((MOE_ROWS, D), F32))


def _moe_group_kernel(te_ref, ta_ref, xs_ref, wa_ref, wb_ref, wo_ref, y_ref, h_scr):
    del te_ref
    i = pl.program_id(0)
    f = pl.program_id(1)
    active = ta_ref[i] == 1

    @pl.when(f == 0)
    def _():
        h_scr[...] = xs_ref[...].astype(BF16)
        y_ref[...] = jnp.zeros(y_ref.shape, F32)

    @pl.when(active)
    def _():
        h = h_scr[...]
        a = jnp.dot(h, wa_ref[...].astype(BF16), preferred_element_type=F32)
        b = jnp.dot(h, wb_ref[...].astype(BF16), preferred_element_type=F32)
        g = (_silu(a) * b).astype(BF16)
        y_ref[...] += jnp.dot(g, wo_ref[...].astype(BF16), preferred_element_type=F32)


def _moe_group(xs, tile_expert, tile_active, w_in, w_out):
    nf = F_EXP // MOE_TF

    def fblk(i, f, ta):
        return jnp.where(ta[i] == 1, f, nf - 1)

    grid_spec = pltpu.PrefetchScalarGridSpec(
        num_scalar_prefetch=2,
        grid=(MOE_TILES, nf),
        in_specs=[pl.BlockSpec((MOE_BM, D), lambda i, f, te, ta: (i, 0)),
                  pl.BlockSpec((None, None, D, MOE_TF), lambda i, f, te, ta: (0, te[i], 0, fblk(i, f, ta))),
                  pl.BlockSpec((None, None, D, MOE_TF), lambda i, f, te, ta: (0, te[i], 0, nf + fblk(i, f, ta))),
                  pl.BlockSpec((None, None, MOE_TF, D), lambda i, f, te, ta: (0, te[i], fblk(i, f, ta), 0))],
        out_specs=pl.BlockSpec((MOE_BM, D), lambda i, f, te, ta: (i, 0)),
        scratch_shapes=[pltpu.VMEM((MOE_BM, D), BF16)])
    return pl.pallas_call(
        _moe_group_kernel,
        grid_spec=grid_spec,
        out_shape=jax.ShapeDtypeStruct((MOE_ROWS, D), F32),
        compiler_params=_cparams(("parallel", "arbitrary")),
        name="moe_group",
    )(tile_expert, tile_active, xs, w_in, w_in, w_out)


MOE_COMBINE_BM = 256


def _moe_combine_kernel(dest_ref, route_ref, x_ref, m_ref, nf_ref, ys_ref, outc_ref, outl_ref, ybuf, sem):
    n = x_ref.shape[0]

    def issue(j, carry):
        for k in range(2):
            pltpu.make_async_copy(_row(ys_ref, dest_ref[0, 2 * j + k]), _row(ybuf, k * n + j), sem).start()
        return carry

    lax.fori_loop(0, n, issue, 0, unroll=8)

    def drain(j, carry):
        for k in range(2):
            pltpu.make_async_copy(_row(ys_ref, 0), _row(ybuf, 0), sem).wait()
        return carry

    lax.fori_loop(0, n, drain, 0, unroll=8)

    route = route_ref[...]
    f = (ybuf[0:n, :] * route[:, ROUTE_W1:ROUTE_W1 + 1]
         + ybuf[n:2 * n, :] * route[:, ROUTE_W2:ROUTE_W2 + 1])
    xn = x_ref[...] + m_ref[5:6, :] * f
    y = xn * lax.rsqrt(jnp.mean(xn * xn, axis=-1, keepdims=True) + EPS) * nf_ref[...]
    is_ctx = pl.program_id(0) < N_CTX // MOE_COMBINE_BM

    @pl.when(is_ctx)
    def _():
        outc_ref[...] = y

    @pl.when(jnp.logical_not(is_ctx))
    def _():
        outl_ref[...] = y


def _moe_combine(ys, dest, route, x, mods, norm_final):
    bm = MOE_COMBINE_BM
    nt = N_TOK // bm
    nct = N_CTX // bm
    return pl.pallas_call(
        _moe_combine_kernel,
        grid=(nt,),
        in_specs=[pl.BlockSpec((None, 1, 2 * bm), lambda i: (i, 0, 0), memory_space=pltpu.SMEM),
                  pl.BlockSpec((bm, 128), lambda i: (i, 0)),
                  pl.BlockSpec((bm, D), lambda i: (i, 0)),
                  pl.BlockSpec((None, 6, D), lambda i: (_group_of_rows(i * bm), 0, 0)),
                  pl.BlockSpec((1, D), lambda i: (0, 0)),
                  pl.BlockSpec(memory_space=pl.ANY)],
        out_specs=[pl.BlockSpec((bm, D), lambda i: (jnp.minimum(i, nct - 1), 0)),
                   pl.BlockSpec((bm, D), lambda i: (jnp.maximum(i - nct, 0), 0))],
        out_shape=[jax.ShapeDtypeStruct((N_CTX, D), F32), jax.ShapeDtypeStruct((N_LAT, D), F32)],
        scratch_shapes=[pltpu.VMEM((2 * bm, D), F32), pltpu.SemaphoreType.DMA(())],
        compiler_params=_cparams(("arbitrary",)),
        name="moe_combine",
    )(dest.reshape(nt, 1, 2 * bm), route, x, mods, norm_final.reshape(1, D), ys)


def _moe(h2, route, x, mods, w_in, w_out, norm_final):
    dest, tile_expert, tile_active = _moe_routing(route)
    xs = _moe_scatter(h2, dest)
    ys = _moe_group(xs, tile_expert, tile_active, w_in, w_out)
    return _moe_combine(ys, dest, route, x, mods, norm_final)


def _hgrn_lower_bounds(lb_param):
    sm = jax.nn.softmax(lb_param.astype(F32), axis=0)
    return jnp.cumsum(sm, axis=0) - sm[0:1]


def kernel(x_prompt, x_sample, cache_k, cache_v, state_hgrn, c, c_ctx, w_ada, b_ada, norm_mix, norm_ffn,
           w_in, hgrn_lb, hgrn_norm, pool_w, pool_scale, rpb, w_branch_a, w_branch_b, w_branch_c, w_out,
           ffn_w_in, ffn_w_out, router, moe_w_in, moe_w_out, norm_final):
    x = (x_prompt.reshape(N_CTX, D), x_sample.reshape(N_LAT, D))
    cond8 = jnp.concatenate([c_ctx[None], c, jnp.zeros((5, D), F32)], axis=0)
    mods = _adaln(cond8, w_ada, b_ada)[:, :3].reshape(DEPTH, 3, 6, D)
    lbs = _hgrn_lower_bounds(hgrn_lb)
    ck = cache_k.reshape(N_LAT_B, DEPTH, PAST, WIDTH).astype(BF16)
    cv = cache_v.reshape(N_LAT_B, DEPTH, PAST, WIDTH).astype(BF16)
    w_in16 = w_in.astype(BF16)
    router_pad = jnp.pad(router[0], ((0, 0), (0, 128 - N_EXP)))
    r_hi = router_pad.astype(BF16)
    r_lo = (router_pad - r_hi.astype(F32)).astype(BF16)

    kvs, ss = [], []
    for l in range(DEPTH):
        z, zg, kv = _inproj(x, norm_mix[l], mods[l], w_in16, l)
        of_c, ob_c, s_ctx = _hgrn(z, zg, lbs[l], None, 0, N_CTX_B, CTX_T)
        of_l, ob_l, _ = _hgrn(z, zg, lbs[l], state_hgrn[:, l], N_CTX, N_LAT_B, LAT_T)
        o_pool = _pool(z, pool_w[l].astype(BF16), pool_scale[l])
        oc_c = _ctx_attn(z)
        oc_l = _na_attn(z, ck, cv, _na_bias_table(rpb[l]), l)
        merged = _merge((of_c, ob_c, oc_c), (of_l, ob_l, oc_l), z, o_pool, x, mods[l], hgrn_norm[l], norm_ffn[l],
                        w_branch_a[l].astype(BF16), w_branch_b[l].astype(BF16),
                        w_branch_c[l].astype(BF16), w_out[l].astype(BF16),
                        (r_hi, r_lo) if l % 2 == 1 else None)
        if l % 2 == 0:
            x, h2 = merged
            x = _ffn(h2, x, mods[l], ffn_w_in, ffn_w_out)
        else:
            x, h2, route = merged
            y_ctx, y_lat = _moe(h2, route, x, mods[l], moe_w_in, moe_w_out, norm_final)
        kvs.append(kv.reshape(N_CTX_B, CTX_T, 2, NA_H, NA_D))
        ss.append(s_ctx)
    kv = jnp.stack(kvs, axis=1)
    y_prompt = y_ctx.reshape(N_CTX_B, CTX_T, D)
    y_sample = y_lat.reshape(N_LAT_B, LAT_T, D)
    return (y_prompt, y_sample, kv[:, :, :, 0], kv[:, :, :, 1], jnp.stack(ss, axis=1))
```

```python
import functools

import numpy as np
import jax
import jax.numpy as jnp
from jax import lax
from jax.experimental import pallas as pl
from jax.experimental.pallas import tpu as pltpu

F32 = jnp.float32
BF16 = jnp.bfloat16

D = 1024
N_CTX_B, CTX_T = 32, 256
N_LAT_B, LAT_T = 2, 4096
N_CTX = N_CTX_B * CTX_T
N_LAT = N_LAT_B * LAT_T
N_TOK = N_CTX + N_LAT
DEPTH = 2
GRID_W = 64
GRID_ROWS = LAT_T // GRID_W
PAST = 512
HG_H, HG_D = 4, 128
WIDTH = 512
NA_H, NA_D = 8, 64
WIN_ROWS, WIN_COLS = 8, 16
POOL_WINDOWS = (2, 4, 8, 16)
IN_COLS = 7680
F_DENSE = 2816
N_EXP = 8
F_EXP = 3584
GATE_CLIP = 30.0
EPS = 1e-6
NEG_BIG = -1e30

COL_AQ, COL_AFF, COL_AFB, COL_AI, COL_AG, COL_BU, COL_CQ, COL_CK, COL_CV = range(9)
GL_COL0 = 9 * WIDTH

ROUTE_I1, ROUTE_I2, ROUTE_W1, ROUTE_W2 = 8, 9, 10, 11

VMEM_LIMIT = 56 * 1024 * 1024


def _cparams(sem):
    return pltpu.CompilerParams(dimension_semantics=sem, vmem_limit_bytes=VMEM_LIMIT)


def _sigmoid(x):
    return 1.0 / (1.0 + jnp.exp(-x))


def _silu(x):
    return x / (1.0 + jnp.exp(-x))


def _group_of_rows(row0):
    return jnp.maximum(row0 - N_CTX + LAT_T, 0) // LAT_T


def _adaln_kernel(c_ref, w_ref, b_ref, o_ref):
    s = _silu(c_ref[...]).astype(BF16)
    o_ref[...] = jnp.dot(s, w_ref[...].astype(BF16), preferred_element_type=F32) + b_ref[...]


def _adaln(cond8, w_ada, b_ada):
    tn = 1536
    return pl.pallas_call(
        _adaln_kernel,
        grid=(DEPTH, 6 * D // tn),
        in_specs=[pl.BlockSpec((8, D), lambda l, j: (0, 0)),
                  pl.BlockSpec((None, D, tn), lambda l, j: (l, 0, j)),
                  pl.BlockSpec((None, 1, tn), lambda l, j: (l, 0, j))],
        out_specs=pl.BlockSpec((None, 8, tn), lambda l, j: (l, 0, j)),
        out_shape=jax.ShapeDtypeStruct((DEPTH, 8, 6 * D), F32),
        compiler_params=_cparams(("parallel", "parallel")),
        name="adaln",
    )(cond8, w_ada, b_ada.reshape(DEPTH, 1, 6 * D))


INPROJ_BM, INPROJ_BN = 1024, 1536
GATE_TILE = (COL_AFF * WIDTH) // INPROJ_BN
KV_TILE = (COL_CK * WIDTH) // INPROJ_BN


def _stream_rows(x, bm):
    if not isinstance(x, tuple):
        specs = [pl.BlockSpec((bm, D), lambda i, *_: (i, 0))]
        return specs, [x]
    nct = N_CTX // bm
    specs = [pl.BlockSpec((bm, D), lambda i, *_: (jnp.minimum(i, nct - 1), 0)),
             pl.BlockSpec((bm, D), lambda i, *_: (jnp.maximum(i - nct, 0), 0))]
    return specs, list(x)


def _read_stream_rows(x_refs, bm):
    if len(x_refs) == 1:
        return x_refs[0][...]
    return jnp.where(pl.program_id(0) < N_CTX // bm, x_refs[0][...], x_refs[1][...])


def _inproj_kernel(*refs):
    nw_ref, m_ref, w_ref, z_ref, zg_ref, kv_ref, h_scr = refs[-7:]
    x_refs = refs[:-7]
    i = pl.program_id(0)
    j = pl.program_id(1)

    @pl.when(j == 0)
    def _():
        x = _read_stream_rows(x_refs, INPROJ_BM)
        y = x * lax.rsqrt(jnp.mean(x * x, axis=-1, keepdims=True) + EPS) * nw_ref[...]
        h_scr[...] = (y * (1.0 + m_ref[1:2, :]) + m_ref[0:1, :]).astype(BF16)

    acc = jnp.dot(h_scr[...], w_ref[...], preferred_element_type=F32)
    z_ref[...] = acc.astype(BF16)

    @pl.when(j == GATE_TILE)
    def _():
        c0 = COL_AFF * WIDTH - GATE_TILE * INPROJ_BN
        zg_ref[...] = acc[:, c0:c0 + 2 * WIDTH]

    @pl.when((j == KV_TILE) & (i < N_CTX // INPROJ_BM))
    def _():
        c0 = COL_CK * WIDTH - KV_TILE * INPROJ_BN
        kv_ref[...] = acc[:, c0:c0 + 2 * WIDTH]


def _inproj(x, norm_w, mods, w_in16, layer):
    bm, bn = INPROJ_BM, INPROJ_BN
    nct = N_CTX // bm
    x_specs, x_args = _stream_rows(x, bm)
    return pl.pallas_call(
        _inproj_kernel,
        grid=(N_TOK // bm, IN_COLS // bn),
        in_specs=x_specs + [
            pl.BlockSpec((1, D), lambda i, j: (0, 0)),
            pl.BlockSpec((None, 6, D), lambda i, j: (_group_of_rows(i * bm), 0, 0)),
            pl.BlockSpec((None, D, bn), lambda i, j: (layer, 0, j))],
        out_specs=[pl.BlockSpec((bm, bn), lambda i, j: (i, j)),
                   pl.BlockSpec((bm, 2 * WIDTH), lambda i, j: (i, 0)),
                   pl.BlockSpec((bm, 2 * WIDTH), lambda i, j: (jnp.minimum(i, nct - 1), 0))],
        out_shape=[jax.ShapeDtypeStruct((N_TOK, IN_COLS), BF16),
                   jax.ShapeDtypeStruct((N_TOK, 2 * WIDTH), F32),
                   jax.ShapeDtypeStruct((N_CTX, 2 * WIDTH), F32)],
        scratch_shapes=[pltpu.VMEM((bm, D), BF16)],
        compiler_params=_cparams(("arbitrary", "arbitrary")),
        name="inproj",
    )(*x_args, norm_w.reshape(1, D), mods, w_in16)


HG_C = 128
HG_LEVELS = (4, 8, 16, 32, 64, 128)
LOG2_E = 1.4426950408889634
EXP2_CAP = 120.0


def _hgrn_level_ids(reverse):
    t = np.arange(HG_C)[:, None]
    s = np.arange(HG_C)[None, :]
    if reverse:
        t, s = s, t
    lev = np.full((HG_C, HG_C), -1, np.int32)
    lev[(t // 4 == s // 4) & (s <= t)] = 0
    for li, L in enumerate(HG_LEVELS[1:], start=1):
        m = (t // L == s // L) & (t % L >= L // 2) & (s % L < L // 2)
        lev[m] = li
    return lev


def _hgrn_ref_rows(b_scr, d, reverse):
    out = []
    r_lo, r_hi = (2, 6) if reverse else (1, 5)
    sub = lax.broadcasted_iota(jnp.int32, (8, WIDTH), 0)
    pieces = []
    for g in range(HG_C // 8):
        lo = jnp.broadcast_to(b_scr[d, 8 * g + r_lo:8 * g + r_lo + 1, :], (8, WIDTH))
        hi = jnp.broadcast_to(b_scr[d, 8 * g + r_hi:8 * g + r_hi + 1, :], (8, WIDTH))
        pieces.append(jnp.where(sub < 4, lo, hi))
    out.append(jnp.concatenate(pieces, axis=0))
    for L in HG_LEVELS[1:]:
        r = L // 2 - 1 if reverse else L // 2
        pieces = [jnp.broadcast_to(b_scr[d, L * g + r:L * g + r + 1, :], (L, WIDTH))
                  for g in range(HG_C // L)]
        out.append(pieces[0] if len(pieces) == 1 else jnp.concatenate(pieces, axis=0))
    return out


def _hgrn_kernel(*refs, has_init):
    if has_init:
        (qf_ref, ff_ref, vf_ref, qb_ref, fb_ref, vb_ref, lb_ref, levf_ref, levb_ref, s0_ref,
         of_ref, ob_ref, so_ref, st_scr, b_scr) = refs
    else:
        (qf_ref, ff_ref, vf_ref, qb_ref, fb_ref, vb_ref, lb_ref, levf_ref, levb_ref,
         of_ref, ob_ref, so_ref, st_scr, b_scr) = refs
    c = pl.program_id(1)
    C = HG_C

    @pl.when(c == 0)
    def _():
        for d in range(2):
            for h in range(HG_H):
                if has_init:
                    st_scr[d, h] = s0_ref[d, h].T
                else:
                    st_scr[d, h] = jnp.zeros((HG_D, HG_D), F32)

    row = lax.broadcasted_iota(jnp.int32, (C, C), 0)
    col = lax.broadcasted_iota(jnp.int32, (C, C), 1)
    dirs = ((qf_ref, ff_ref, vf_ref, levf_ref, of_ref), (qb_ref, fb_ref, vb_ref, levb_ref, ob_ref))
    for d, (q_ref, f_ref, v_ref, lev_ref, o_ref) in enumerate(dirs):
        reverse = d == 1
        tri = jnp.where((col >= row) if reverse else (col <= row), 1.0, 0.0).astype(BF16)
        q = _silu(q_ref[...].astype(F32))
        fx = jnp.clip(f_ref[...], -GATE_CLIP, GATE_CLIP)
        e = jnp.exp(-fx)
        sig_pos = 1.0 / (1.0 + e)
        sig_neg = e * sig_pos
        lb = lb_ref[d:d + 1, :]
        lf = jnp.log(lb + (1.0 - lb) * sig_pos) * LOG2_E
        k = (1.0 - lb) * sig_neg
        q16 = q.astype(BF16)
        k16 = k.astype(BF16)
        hi = lf.astype(BF16)
        r1 = lf - hi.astype(F32)
        mid = r1.astype(BF16)
        lo = (r1 - mid.astype(F32)).astype(BF16)
        b = (jnp.dot(tri, hi, preferred_element_type=F32)
             + jnp.dot(tri, mid, preferred_element_type=F32)
             + jnp.dot(tri, lo, preferred_element_type=F32))
        b_scr[d] = b
        refs_m = _hgrn_ref_rows(b_scr, d, reverse)
        qs, ks = [], []
        for li, m in enumerate(refs_m):
            dlt = b - m
            if li == 0:
                qs.append(q16 * jnp.exp2(jnp.minimum(dlt, EXP2_CAP)).astype(BF16))
                ks.append(k16 * jnp.exp2(jnp.minimum(-dlt, EXP2_CAP)).astype(BF16))
            else:
                fac = jnp.exp2(-jnp.abs(dlt)).astype(BF16)
                qs.append(q16 * fac)
                ks.append(k16 * fac)
        lev = lev_ref[...]
        b_end = b[0:1, :] if reverse else b[C - 1:C, :]
        q_in = q16 * jnp.exp2(b).astype(BF16)
        k_out = k16 * jnp.exp2(b_end - b).astype(BF16)
        dec = jnp.exp2(b_end)
        vb16 = v_ref[...]
        for h in range(HG_H):
            sl = slice(h * HG_D, (h + 1) * HG_D)
            a = jnp.zeros((C, C), F32)
            for li in range(len(HG_LEVELS)):
                p = lax.dot_general(qs[li][:, sl], ks[li][:, sl], (((1,), (1,)), ((), ())),
                                    preferred_element_type=F32)
                a = jnp.where(lev == li, p, a)
            vh = vb16[:, sl]
            st = st_scr[d, h]
            o = (jnp.dot(a.astype(BF16), vh, preferred_element_type=F32)
                 + lax.dot_general(q_in[:, sl], st.astype(BF16), (((1,), (1,)), ((), ())),
                                   preferred_element_type=F32))
            o_ref[:, sl] = o
            upd = lax.dot_general(vh, k_out[:, sl], (((0,), (0,)), ((), ())),
                                  preferred_element_type=F32)
            st_scr[d, h] = st * dec[:, sl] + upd

    @pl.when(c == pl.num_programs(1) - 1)
    def _():
        for d in range(2):
            for h in range(HG_H):
                so_ref[d, h] = st_scr[d, h].T


def _hgrn(z, zg, lb, s0, row_off, n_b, seq_t):
    C = HG_C
    nc = seq_t // C
    base = row_off // C
    has_init = s0 is not None

    def fwd(col):
        return pl.BlockSpec((C, WIDTH), lambda b, c: (base + b * nc + c, col))

    def bwd(col):
        return pl.BlockSpec((C, WIDTH), lambda b, c: (base + b * nc + nc - 1 - c, col))

    full = lambda shape: pl.BlockSpec(shape, lambda b, c: (0,) * len(shape))
    in_specs = [fwd(COL_AQ), fwd(0), fwd(COL_AI), bwd(COL_AQ), bwd(1), bwd(COL_AI),
                full((2, WIDTH)), full((C, C)), full((C, C))]
    args = [z, zg, z, z, zg, z, lb, jnp.asarray(_hgrn_level_ids(False)), jnp.asarray(_hgrn_level_ids(True))]
    if has_init:
        in_specs.append(pl.BlockSpec((None, 2, HG_H, HG_D, HG_D), lambda b, c: (b, 0, 0, 0, 0)))
        args.append(s0)
    return pl.pallas_call(
        functools.partial(_hgrn_kernel, has_init=has_init),
        grid=(n_b, nc),
        in_specs=in_specs,
        out_specs=[pl.BlockSpec((C, WIDTH), lambda b, c: (b * nc + c, 0)),
                   pl.BlockSpec((C, WIDTH), lambda b, c: (b * nc + nc - 1 - c, 0)),
                   pl.BlockSpec((None, 2, HG_H, HG_D, HG_D), lambda b, c: (b, 0, 0, 0, 0))],
        out_shape=[jax.ShapeDtypeStruct((n_b * seq_t, WIDTH), F32),
                   jax.ShapeDtypeStruct((n_b * seq_t, WIDTH), F32),
                   jax.ShapeDtypeStruct((n_b, 2, HG_H, HG_D, HG_D), F32)],
        scratch_shapes=[pltpu.VMEM((2, HG_H, HG_D, HG_D), F32),
                        pltpu.VMEM((2, C, WIDTH), F32)],
        compiler_params=_cparams(("parallel", "arbitrary")),
        name="hgrn_lat" if has_init else "hgrn_ctx",
    )(*args)


POOL_T = 256
POOL_HALO = 16


def _pool_kernel(u_ref, up_ref, un_ref, w_ref, sc_ref, o_ref):
    i = pl.program_id(0)
    is_lat = i >= N_CTX // POOL_T
    seq_t = jnp.where(is_lat, LAT_T, CTX_T)
    t0 = jnp.where(is_lat, ((i - N_CTX // POOL_T) % (LAT_T // POOL_T)) * POOL_T, 0)
    TT, HL = POOL_T, POOL_HALO
    diff = lax.broadcasted_iota(jnp.int32, (TT, TT), 1) - lax.broadcasted_iota(jnp.int32, (TT, TT), 0)
    diff_h = lax.broadcasted_iota(jnp.int32, (TT, HL), 1) - lax.broadcasted_iota(jnp.int32, (TT, HL), 0)
    t_glob = t0 + lax.broadcasted_iota(jnp.int32, (TT, 128), 0)
    has_prev = t0 > 0
    has_next = t0 + TT < seq_t
    for g, w in enumerate(POOL_WINDOWS):
        half = w // 2
        sl = slice(g * 128, (g + 1) * 128)
        band = jnp.where((diff >= -half) & (diff < half), 1.0, 0.0).astype(BF16)
        dp = diff_h - HL
        band_p = jnp.where((dp >= -half) & (dp < half) & has_prev, 1.0, 0.0).astype(BF16)
        dn = diff_h + TT
        band_n = jnp.where((dn >= -half) & (dn < half) & has_next, 1.0, 0.0).astype(BF16)
        u = u_ref[:, sl]
        s = (jnp.dot(band, u, preferred_element_type=F32)
             + jnp.dot(band_p, up_ref[:, sl], preferred_element_type=F32)
             + jnp.dot(band_n, un_ref[:, sl], preferred_element_type=F32))
        cnt = (jnp.minimum(t_glob + half, seq_t) - jnp.maximum(t_glob - half, 0)).astype(F32)
        dd = s / cnt - u.astype(F32)
        y = jnp.dot(dd.astype(BF16), w_ref[g], preferred_element_type=F32)
        o_ref[:, sl] = y * sc_ref[:, sl]


def _pool(z, pool_w16, pool_scale):
    nt = N_TOK // POOL_T
    per = POOL_T // POOL_HALO
    return pl.pallas_call(
        _pool_kernel,
        grid=(nt,),
        in_specs=[pl.BlockSpec((POOL_T, WIDTH), lambda i: (i, COL_BU)),
                  pl.BlockSpec((POOL_HALO, WIDTH), lambda i: (jnp.maximum(i * per - 1, 0), COL_BU)),
                  pl.BlockSpec((POOL_HALO, WIDTH), lambda i: (jnp.minimum((i + 1) * per, nt * per - 1), COL_BU)),
                  pl.BlockSpec((4, 128, 128), lambda i: (0, 0, 0)),
                  pl.BlockSpec((1, WIDTH), lambda i: (0, 0))],
        out_specs=pl.BlockSpec((POOL_T, WIDTH), lambda i: (i, 0)),
        out_shape=jax.ShapeDtypeStruct((N_TOK, WIDTH), F32),
        compiler_params=_cparams(("parallel",)),
        name="pool",
    )(z, z, z, pool_w16, pool_scale.reshape(1, WIDTH))


def _head_mask(hh):
    lane = lax.broadcasted_iota(jnp.int32, (1, 128), 1)
    in_head = (lane >= hh * NA_D) & (lane < (hh + 1) * NA_D)
    return jnp.where(in_head, NA_D ** -0.5, 0.0).astype(BF16)


def _ctx_attn_kernel(q_ref, k_ref, v_ref, o_ref):
    lane = lax.broadcasted_iota(jnp.int32, (CTX_T, 128), 1)
    for j in range(NA_H // 2):
        sl = slice(j * 128, (j + 1) * 128)
        q = q_ref[:, sl]
        kt = k_ref[:, sl]
        vt = v_ref[:, sl]
        outs = []
        for hh in range(2):
            s = lax.dot_general(q * _head_mask(hh), kt, (((1,), (1,)), ((), ())), preferred_element_type=F32)
            m = jnp.max(s, axis=-1, keepdims=True)
            p = jnp.exp(s - m)
            l = jnp.sum(p, axis=-1, keepdims=True)
            outs.append(jnp.dot(p.astype(BF16), vt, preferred_element_type=F32) / l)
        o_ref[:, sl] = jnp.where(lane < NA_D, outs[0], outs[1])


def _ctx_attn(z):
    spec = lambda col: pl.BlockSpec((CTX_T, WIDTH), lambda b: (b, col))
    return pl.pallas_call(
        _ctx_attn_kernel,
        grid=(N_CTX_B,),
        in_specs=[spec(COL_CQ), spec(COL_CK), spec(COL_CV)],
        out_specs=pl.BlockSpec((CTX_T, WIDTH), lambda b: (b, 0)),
        out_shape=jax.ShapeDtypeStruct((N_CTX, WIDTH), F32),
        compiler_params=_cparams(("parallel",)),
        name="ctx_attn",
    )(z, z, z)


NA_QR = 4
NA_KR = 12
NA_NQ = NA_QR * GRID_W
NA_NK = NA_KR * GRID_W
NA_BLOCKS = GRID_ROWS // NA_QR


def _na_key_row0(blk):
    return jnp.clip(blk * NA_QR - WIN_ROWS // 2, 0, GRID_ROWS - NA_KR)


def _na_geometry():
    patterns, var_of_block = [], []
    for blk in range(NA_BLOCKS):
        r = blk * NA_QR + np.arange(NA_QR)[:, None]
        kr = int(np.clip(blk * NA_QR - WIN_ROWS // 2, 0, GRID_ROWS - NA_KR)) + np.arange(NA_KR)[None, :]
        rs = np.clip(r - WIN_ROWS // 2, 0, GRID_ROWS - WIN_ROWS)
        valid = (kr >= rs) & (kr < rs + WIN_ROWS)
        assert (valid.sum(axis=1) == WIN_ROWS).all(), "key rows must cover every query row's window"
        drow = np.where(valid, kr - r + WIN_ROWS - 1, 0)
        key = (drow.tobytes(), valid.tobytes())
        ids = [i for i, (k_, _, _) in enumerate(patterns) if k_ == key]
        if not ids:
            patterns.append((key, drow, valid))
            ids = [len(patterns) - 1]
        var_of_block.append(ids[0])
    drow = np.stack([p[1] for p in patterns])
    valid = np.stack([p[2] for p in patterns])
    return np.asarray(var_of_block, np.int32), drow, valid


NA_DROWS = 2 * WIN_ROWS - 1


def _na_bias_kernel(didx_ref, t2_ref, o_ref):
    v = pl.program_id(0)
    for qr in range(NA_QR):
        for kp in range(NA_KR // 2):
            base = (v * NA_QR + qr) * NA_KR + 2 * kp
            pair = jnp.concatenate([t2_ref[didx_ref[base]], t2_ref[didx_ref[base + 1]]], axis=1)
            o_ref[qr * GRID_W:(qr + 1) * GRID_W, kp * 128:(kp + 1) * 128] = pair


def _na_bias_table(rpb_l):
    _, drow, valid = _na_geometry()
    n_var = valid.shape[0]
    qc = np.arange(GRID_W)[:, None]
    kc = np.arange(GRID_W)[None, :]
    q0 = np.clip(qc - WIN_COLS // 2, 0, GRID_W - WIN_COLS)
    col_in = (kc >= q0) & (kc < q0 + WIN_COLS)
    dcol = np.clip(kc - qc, -(WIN_COLS - 1), WIN_COLS - 1) + WIN_COLS - 1
    oh_col = (dcol[None] == np.arange(2 * WIN_COLS - 1)[:, None, None]).astype(np.float32)
    t2 = jnp.einsum('hdc,cqk->hdqk', rpb_l, jnp.asarray(oh_col), precision=lax.Precision.HIGHEST)
    t2 = jnp.where(col_in[None, None], t2, NEG_BIG)
    t2 = jnp.concatenate([t2, jnp.full((NA_H, 1, GRID_W, GRID_W), NEG_BIG, F32)], axis=1)
    didx = np.where(valid, drow, NA_DROWS).astype(np.int32).reshape(-1)
    grid_spec = pltpu.PrefetchScalarGridSpec(
        num_scalar_prefetch=1,
        grid=(n_var, NA_H),
        in_specs=[pl.BlockSpec((None, NA_DROWS + 1, GRID_W, GRID_W), lambda v, h, didx: (h, 0, 0, 0))],
        out_specs=pl.BlockSpec((None, None, NA_NQ, NA_NK), lambda v, h, didx: (v, h, 0, 0)))
    return pl.pallas_call(
        _na_bias_kernel,
        grid_spec=grid_spec,
        out_shape=jax.ShapeDtypeStruct((n_var, NA_H, NA_NQ, NA_NK), F32),
        compiler_params=_cparams(("parallel", "parallel")),
        name="na_bias",
    )(jnp.asarray(didx), t2)


def _na_attn_kernel(var_ref, q_ref, k_ref, v_ref, kc_ref, vc_ref, tbl_ref, o_ref):
    del var_ref
    blk = pl.program_id(1)
    k0 = pl.multiple_of(_na_key_row0(blk) * GRID_W, GRID_W)
    lane = lax.broadcasted_iota(jnp.int32, (NA_NQ, 128), 1)
    for j in range(NA_H // 2):
        sl = slice(j * 128, (j + 1) * 128)
        q = q_ref[:, sl]
        kt = k_ref[pl.ds(k0, NA_NK), sl]
        vt = v_ref[pl.ds(k0, NA_NK), sl]
        kct = kc_ref[:, sl]
        vct = vc_ref[:, sl]
        outs = []
        for hh in range(2):
            qm = q * _head_mask(hh)
            s_loc = lax.dot_general(qm, kt, (((1,), (1,)), ((), ())), preferred_element_type=F32)
            s_ctx = lax.dot_general(qm, kct, (((1,), (1,)), ((), ())), preferred_element_type=F32)
            tb = tbl_ref[2 * j + hh]
            s_loc = jnp.where(tb > 0.5 * NEG_BIG, s_loc + tb, NEG_BIG)
            m = jnp.maximum(jnp.max(s_loc, axis=-1, keepdims=True), jnp.max(s_ctx, axis=-1, keepdims=True))
            p_loc = jnp.exp(s_loc - m)
            p_ctx = jnp.exp(s_ctx - m)
            l = jnp.sum(p_loc, axis=-1, keepdims=True) + jnp.sum(p_ctx, axis=-1, keepdims=True)
            o = (jnp.dot(p_loc.astype(BF16), vt, preferred_element_type=F32)
                 + jnp.dot(p_ctx.astype(BF16), vct, preferred_element_type=F32))
            outs.append(o / l)
        o_ref[:, sl] = jnp.where(lane < NA_D, outs[0], outs[1])


def _na_attn(z, cache_k16, cache_v16, tbl, layer):
    base_q = N_CTX // NA_NQ
    base_t = N_CTX // LAT_T
    ctx_spec = pl.BlockSpec((None, None, PAST, WIDTH), lambda b, r, var: (b, layer, 0, 0))
    grid_spec = pltpu.PrefetchScalarGridSpec(
        num_scalar_prefetch=1,
        grid=(N_LAT_B, NA_BLOCKS),
        in_specs=[pl.BlockSpec((NA_NQ, WIDTH), lambda b, r, var: (base_q + b * NA_BLOCKS + r, COL_CQ)),
                  pl.BlockSpec((LAT_T, WIDTH), lambda b, r, var: (base_t + b, COL_CK)),
                  pl.BlockSpec((LAT_T, WIDTH), lambda b, r, var: (base_t + b, COL_CV)),
                  ctx_spec, ctx_spec,
                  pl.BlockSpec((None, NA_H, NA_NQ, NA_NK), lambda b, r, var: (var[r], 0, 0, 0))],
        out_specs=pl.BlockSpec((NA_NQ, WIDTH), lambda b, r, var: (b * NA_BLOCKS + r, 0)))
    return pl.pallas_call(
        _na_attn_kernel,
        grid_spec=grid_spec,
        out_shape=jax.ShapeDtypeStruct((N_LAT, WIDTH), F32),
        compiler_params=_cparams(("parallel", "arbitrary")),
        name="na_attn",
    )(jnp.asarray(_na_geometry()[0]), z, z, z, cache_k16, cache_v16, tbl)


MERGE_BM = 512


def _merge_kernel(*refs, with_router, n_x):
    x_refs, refs = refs[:n_x], refs[n_x:]
    (ofc_ref, obc_ref, occ_ref, ofl_ref, obl_ref, ocl_ref, ag_ref, op_ref, gl0_ref, gl1_ref, m_ref,
     hn_ref, nf_ref, wa_ref, wb_ref, wc_ref, wo_ref) = refs[:17]
    if with_router:
        rhi_ref, rlo_ref, xn_ref, h2_ref, route_ref = refs[17:]
    else:
        xn_ref, h2_ref = refs[17:]
    is_ctx = pl.program_id(0) < N_CTX // MERGE_BM
    o = jnp.where(is_ctx, ofc_ref[...] + obc_ref[...], ofl_ref[...] + obl_ref[...])
    oc = jnp.where(is_ctx, occ_ref[...], ocl_ref[...])
    parts = []
    for h in range(HG_H):
        oh = o[:, h * HG_D:(h + 1) * HG_D]
        parts.append(oh * lax.rsqrt(jnp.mean(oh * oh, axis=-1, keepdims=True) + EPS))
    oa = jnp.concatenate(parts, axis=1) * hn_ref[...] * _silu(ag_ref[...].astype(F32))
    sg0 = _sigmoid(gl0_ref[...]).astype(F32)
    sg1 = _sigmoid(gl1_ref[...]).astype(F32)
    sga = sg0[:, :D]
    sgb = jnp.concatenate([sg0[:, D:], sg1[:, :WIDTH]], axis=1)
    sgc = sg1[:, WIDTH:]
    mix = (sga * jnp.dot(oa.astype(BF16), wa_ref[...], preferred_element_type=F32)
           + sgb * jnp.dot(op_ref[...].astype(BF16), wb_ref[...], preferred_element_type=F32)
           + sgc * jnp.dot(oc.astype(BF16), wc_ref[...], preferred_element_type=F32))
    xn = (_read_stream_rows(x_refs, MERGE_BM)
          + m_ref[2:3, :] * jnp.dot(mix.astype(BF16), wo_ref[...], preferred_element_type=F32))
    xn_ref[...] = xn
    y = xn * lax.rsqrt(jnp.mean(xn * xn, axis=-1, keepdims=True) + EPS) * nf_ref[...]
    h2 = y * (1.0 + m_ref[4:5, :]) + m_ref[3:4, :]
    h2_ref[...] = h2.astype(h2_ref.dtype)
    if not with_router:
        return
    hhi = h2.astype(BF16)
    hlo = (h2 - hhi.astype(F32)).astype(BF16)
    logits = (jnp.dot(hhi, rhi_ref[...], preferred_element_type=F32)
              + jnp.dot(hhi, rlo_ref[...], preferred_element_type=F32)
              + jnp.dot(hlo, rhi_ref[...], preferred_element_type=F32))
    lane = lax.broadcasted_iota(jnp.int32, logits.shape, 1).astype(F32)
    lg = jnp.where(lane < N_EXP, logits, -jnp.inf)
    m1 = jnp.max(lg, axis=-1, keepdims=True)
    i1 = jnp.min(jnp.where(lg == m1, lane, 128.0), axis=-1, keepdims=True)
    lg2 = jnp.where(lane == i1, -jnp.inf, lg)
    m2 = jnp.max(lg2, axis=-1, keepdims=True)
    i2 = jnp.min(jnp.where(lg2 == m2, lane, 128.0), axis=-1, keepdims=True)
    e = jnp.exp(m2 - m1)
    w1 = 1.0 / (1.0 + e)
    route_ref[...] = (jnp.where(lane == ROUTE_I1, i1, 0.0) + jnp.where(lane == ROUTE_I2, i2, 0.0)
                      + jnp.where(lane == ROUTE_W1, w1, 0.0) + jnp.where(lane == ROUTE_W2, e * w1, 0.0))


def _merge(ctx_parts, lat_parts, z, o_pool, x, mods, hgrn_norm_l, norm_ffn_l, wa, wb, wc, wo, router_split):
    bm = MERGE_BM
    glw = 1536
    nct = N_CTX // bm
    with_router = router_split is not None
    row = lambda w, col=0: pl.BlockSpec((bm, w), lambda i: (i, col))
    ctx_row = pl.BlockSpec((bm, WIDTH), lambda i: (jnp.minimum(i, nct - 1), 0))
    lat_row = pl.BlockSpec((bm, WIDTH), lambda i: (jnp.maximum(i - nct, 0), 0))
    const = lambda shape: pl.BlockSpec(shape, lambda i: (0,) * len(shape))
    x_specs, x_args = _stream_rows(x, bm)
    in_specs = x_specs + [ctx_row] * 3 + [lat_row] * 3 + [
        row(WIDTH, COL_AG), row(WIDTH),
        row(glw, GL_COL0 // glw), row(glw, GL_COL0 // glw + 1),
        pl.BlockSpec((None, 6, D), lambda i: (_group_of_rows(i * bm), 0, 0)),
        const((1, WIDTH)), const((1, D)),
        const((WIDTH, D)), const((WIDTH, D)), const((WIDTH, D)), const((D, D))]
    args = x_args + list(ctx_parts) + list(lat_parts) + [
        z, o_pool, z, z, mods, jnp.tile(hgrn_norm_l, HG_H).reshape(1, WIDTH),
        norm_ffn_l.reshape(1, D), wa, wb, wc, wo]
    out_specs = [row(D), row(D)]
    out_shape = [jax.ShapeDtypeStruct((N_TOK, D), F32),
                 jax.ShapeDtypeStruct((N_TOK, D), F32 if with_router else BF16)]
    if with_router:
        in_specs += [const((D, 128)), const((D, 128))]
        args += list(router_split)
        out_specs.append(row(128))
        out_shape.append(jax.ShapeDtypeStruct((N_TOK, 128), F32))
    return pl.pallas_call(
        functools.partial(_merge_kernel, with_router=with_router, n_x=len(x_args)),
        grid=(N_TOK // bm,),
        in_specs=in_specs,
        out_specs=out_specs,
        out_shape=out_shape,
        compiler_params=_cparams(("parallel",)),
        name="merge_route" if with_router else "merge",
    )(*args)


def _ffn_kernel(h_ref, x_ref, m_ref, wa_ref, wb_ref, wo_ref, o_ref, acc_scr):
    f = pl.program_id(1)

    @pl.when(f == 0)
    def _():
        acc_scr[...] = jnp.zeros(acc_scr.shape, F32)

    h = h_ref[...]
    a = jnp.dot(h, wa_ref[...].astype(BF16), preferred_element_type=F32)
    b = jnp.dot(h, wb_ref[...].astype(BF16), preferred_element_type=F32)
    g = (_silu(a) * b).astype(BF16)
    acc_scr[...] += jnp.dot(g, wo_ref[...].astype(BF16), preferred_element_type=F32)

    @pl.when(f == pl.num_programs(1) - 1)
    def _():
        o_ref[...] = x_ref[...] + m_ref[5:6, :] * acc_scr[...]


def _ffn(h2, x, mods, w_in, w_out):
    bm, tf = 1024, 256
    nf = F_DENSE // tf
    return pl.pallas_call(
        _ffn_kernel,
        grid=(N_TOK // bm, nf),
        in_specs=[pl.BlockSpec((bm, D), lambda i, f: (i, 0)),
                  pl.BlockSpec((bm, D), lambda i, f: (i, 0)),
                  pl.BlockSpec((None, 6, D), lambda i, f: (_group_of_rows(i * bm), 0, 0)),
                  pl.BlockSpec((None, D, tf), lambda i, f: (0, 0, f)),
                  pl.BlockSpec((None, D, tf), lambda i, f: (0, 0, nf + f)),
                  pl.BlockSpec((None, tf, D), lambda i, f: (0, f, 0))],
        out_specs=pl.BlockSpec((bm, D), lambda i, f: (i, 0)),
        out_shape=jax.ShapeDtypeStruct((N_TOK, D), F32),
        scratch_shapes=[pltpu.VMEM((bm, D), F32)],
        compiler_params=_cparams(("parallel", "arbitrary")),
        name="ffn",
    )(h2, x, mods, w_in, w_in, w_out)


MOE_BM = 1024
MOE_TF = 512
MOE_NF = F_EXP // MOE_TF
MOE_PAIRS = 2 * N_TOK
MOE_ROWS = MOE_PAIRS + N_EXP * MOE_BM
MOE_TILES = MOE_ROWS // MOE_BM
MOE_CH = -(-MOE_BM // MOE_NF)
MOE_MOVED = MOE_CH * MOE_NF
MOE_BUF = -(-MOE_MOVED // 8) * 8


def _moe_routing(route):
    i1 = route[:, ROUTE_I1].astype(jnp.int32)
    i2 = route[:, ROUTE_I2].astype(jnp.int32)
    ep = jnp.stack([i1, i2], axis=1).reshape(-1)
    onehot = (ep[:, None] == jnp.arange(N_EXP, dtype=jnp.int32)[None, :]).astype(jnp.int32)
    counts = jnp.sum(onehot, axis=0)
    padded = ((counts + MOE_BM - 1) // MOE_BM) * MOE_BM
    ends = jnp.cumsum(padded)
    starts = ends - padded
    first = jnp.cumsum(counts) - counts
    order = jnp.sort(ep * MOE_PAIRS + jnp.arange(MOE_PAIRS, dtype=jnp.int32)) & (MOE_PAIRS - 1)
    tile_row0 = jnp.arange(MOE_TILES, dtype=jnp.int32) * MOE_BM
    tile_active = (tile_row0 < ends[-1]).astype(jnp.int32)
    last_row0 = jnp.maximum(ends[-1] - MOE_BM, 0)
    tile_expert = jnp.sum((jnp.minimum(tile_row0, last_row0)[:, None] >= ends[None, :]).astype(jnp.int32), axis=1)
    tile_expert = jnp.minimum(tile_expert, N_EXP - 1)
    row = jnp.arange(MOE_ROWS, dtype=jnp.int32)
    row_e = jnp.repeat(tile_expert, MOE_BM)
    rank = row - starts[row_e]
    valid = (rank < counts[row_e]) & (row < ends[-1])
    pair = jnp.where(valid, order[jnp.clip(first[row_e] + rank, 0, MOE_PAIRS - 1)], -1)
    pair = pair.reshape(MOE_TILES, MOE_BM)
    tile_count = jnp.sum((pair >= 0).astype(jnp.int32), axis=1)
    src_tok = jnp.pad(jnp.maximum(pair, 0) >> 1, ((0, 0), (0, MOE_BUF - MOE_BM)))
    dst_row = jnp.where(pair >= 0, (pair & 1) * N_TOK + (pair >> 1), -1)
    dst_row = jnp.pad(dst_row, ((1, 0), (0, MOE_BUF - MOE_BM)), constant_values=-1)
    return (src_tok.reshape(MOE_TILES, 1, MOE_BUF), dst_row.reshape(MOE_TILES + 1, 1, MOE_BUF),
            tile_expert, tile_active, tile_count)


def _row(ref, r):
    return ref.at[pl.ds(r, 1), :]


def _moe_group_kernel(te_ref, ta_ref, tc_ref, dprv_ref, dcur_ref, gcur_ref, gnxt_ref, h2_ref, wa_ref, wb_ref, wo_ref,
                      out_ref, xbuf, ybuf, h_scr, gsem, ssem):
    del te_ref
    i = pl.program_id(0)
    f = pl.program_id(1)
    nt = pl.num_programs(0)
    s = i % 2
    o = 1 - s
    active = ta_ref[i] == 1
    prv_active = (i >= 1) & (ta_ref[jnp.maximum(i - 1, 0)] == 1)

    def gather_row(idx_ref, k, slot):
        pltpu.make_async_copy(_row(h2_ref, idx_ref[0, k]), _row(xbuf.at[slot], k), gsem.at[slot]).start()

    def scatter_row(idx_ref, k, slot):
        dst = idx_ref[0, k]

        @pl.when(dst >= 0)
        def _():
            pltpu.make_async_copy(_row(ybuf.at[slot], k), _row(out_ref, dst), ssem.at[slot]).start()

    def wait_rows(sem, n):
        n = jnp.asarray(n, jnp.int32)
        n8 = pl.multiple_of((n >> 3) << 3, 8)

        @pl.when(n8 > 0)
        def _():
            pltpu.make_async_copy(xbuf.at[0, pl.ds(0, n8), :], ybuf.at[0, pl.ds(0, n8), :], sem).wait()

        def one(k, carry):
            pltpu.make_async_copy(_row(xbuf.at[0], 0), _row(ybuf.at[0], 0), sem).wait()
            return carry
        lax.fori_loop(0, n - n8, one, 0)

    def rows_loop(fn, n):
        def body(k, carry):
            fn(k)
            return carry
        lax.fori_loop(0, n, body, 0, unroll=8)

    @pl.when((i == 0) & (f == 0))
    def _():
        rows_loop(lambda k: gather_row(gcur_ref, k, 0), MOE_MOVED)

    @pl.when((f == 0) & (i >= 2))
    def _():
        wait_rows(ssem.at[s], tc_ref[jnp.maximum(i - 2, 0)])

    @pl.when((f == 0) & ((i == 0) | prv_active))
    def _():
        wait_rows(gsem.at[s], MOE_MOVED)

    @pl.when((f == 0) & active)
    def _():
        h_scr[...] = xbuf[s, 0:MOE_BM, :].astype(BF16)
        ybuf[s, 0:MOE_BM, :] = jnp.zeros((MOE_BM, D), F32)

    @pl.when(active)
    def _():
        for u in range(MOE_CH):
            gather_row(gnxt_ref, f * MOE_CH + u, o)
        for u in range(MOE_CH):
            scatter_row(dprv_ref, f * MOE_CH + u, o)
        h = h_scr[...]
        a = jnp.dot(h, wa_ref[...].astype(BF16), preferred_element_type=F32)
        b = jnp.dot(h, wb_ref[...].astype(BF16), preferred_element_type=F32)
        g = (_silu(a) * b).astype(BF16)
        ybuf[s, 0:MOE_BM, :] += jnp.dot(g, wo_ref[...].astype(BF16), preferred_element_type=F32)

    @pl.when(jnp.logical_not(active) & prv_active)
    def _():
        rows_loop(lambda u: scatter_row(dprv_ref, f * MOE_CH + u, o), MOE_CH)

    @pl.when((i == nt - 1) & (f == pl.num_programs(1) - 1))
    def _():
        wait_rows(ssem.at[o], tc_ref[jnp.maximum(i - 1, 0)])
        rows_loop(lambda k: scatter_row(dcur_ref, k, s), MOE_BM)
        wait_rows(ssem.at[s], tc_ref[i])

        @pl.when(active)
        def _():
            wait_rows(gsem.at[o], MOE_MOVED)


def _moe_group(h2, src_tok, dst_row, tile_expert, tile_active, tile_count, w_in, w_out):
    nf = MOE_NF
    last = MOE_TILES - 1

    def fblk(i, f, ta):
        return jnp.where(ta[i] == 1, f, nf - 1)

    idx_spec = lambda off, hi: pl.BlockSpec((None, 1, MOE_BUF),
                                            lambda i, f, te, ta, tc: (jnp.minimum(i + off, hi), 0, 0),
                                            memory_space=pltpu.SMEM)
    grid_spec = pltpu.PrefetchScalarGridSpec(
        num_scalar_prefetch=3,
        grid=(MOE_TILES, nf),
        in_specs=[idx_spec(0, last + 1), idx_spec(1, last + 1),
                  idx_spec(0, last), idx_spec(1, last),
                  pl.BlockSpec(memory_space=pl.ANY),
                  pl.BlockSpec((None, None, D, MOE_TF), lambda i, f, te, ta, tc: (0, te[i], 0, fblk(i, f, ta))),
                  pl.BlockSpec((None, None, D, MOE_TF), lambda i, f, te, ta, tc: (0, te[i], 0, nf + fblk(i, f, ta))),
                  pl.BlockSpec((None, None, MOE_TF, D), lambda i, f, te, ta, tc: (0, te[i], fblk(i, f, ta), 0))],
        out_specs=pl.BlockSpec(memory_space=pl.ANY),
        scratch_shapes=[pltpu.VMEM((2, MOE_BUF, D), F32), pltpu.VMEM((2, MOE_BUF, D), F32),
                        pltpu.VMEM((MOE_BM, D), BF16),
                        pltpu.SemaphoreType.DMA((2,)), pltpu.SemaphoreType.DMA((2,))])
    return pl.pallas_call(
        _moe_group_kernel,
        grid_spec=grid_spec,
        out_shape=jax.ShapeDtypeStruct((MOE_PAIRS, D), F32),
        compiler_params=_cparams(("arbitrary", "arbitrary")),
        name="moe_group",
    )(tile_expert, tile_active, tile_count, dst_row, dst_row, src_tok, src_tok, h2, w_in, w_in, w_out)


MOE_COMBINE_BM = 512


def _moe_combine_kernel(route_ref, x_ref, m_ref, nf_ref, y1_ref, y2_ref, outc_ref, outl_ref):
    route = route_ref[...]
    f = (y1_ref[...] * route[:, ROUTE_W1:ROUTE_W1 + 1]
         + y2_ref[...] * route[:, ROUTE_W2:ROUTE_W2 + 1])
    xn = x_ref[...] + m_ref[5:6, :] * f
    y = xn * lax.rsqrt(jnp.mean(xn * xn, axis=-1, keepdims=True) + EPS) * nf_ref[...]
    is_ctx = pl.program_id(0) < N_CTX // MOE_COMBINE_BM

    @pl.when(is_ctx)
    def _():
        outc_ref[...] = y

    @pl.when(jnp.logical_not(is_ctx))
    def _():
        outl_ref[...] = y


def _moe_combine(ys, route, x, mods, norm_final):
    bm = MOE_COMBINE_BM
    nt = N_TOK // bm
    nct = N_CTX // bm
    return pl.pallas_call(
        _moe_combine_kernel,
        grid=(nt,),
        in_specs=[pl.BlockSpec((bm, 128), lambda i: (i, 0)),
                  pl.BlockSpec((bm, D), lambda i: (i, 0)),
                  pl.BlockSpec((None, 6, D), lambda i: (_group_of_rows(i * bm), 0, 0)),
                  pl.BlockSpec((1, D), lambda i: (0, 0)),
                  pl.BlockSpec((bm, D), lambda i: (i, 0)),
                  pl.BlockSpec((bm, D), lambda i: (nt + i, 0))],
        out_specs=[pl.BlockSpec((bm, D), lambda i: (jnp.minimum(i, nct - 1), 0)),
                   pl.BlockSpec((bm, D), lambda i: (jnp.maximum(i - nct, 0), 0))],
        out_shape=[jax.ShapeDtypeStruct((N_CTX, D), F32), jax.ShapeDtypeStruct((N_LAT, D), F32)],
        compiler_params=_cparams(("arbitrary",)),
        name="moe_combine",
    )(route, x, mods, norm_final.reshape(1, D), ys, ys)


def _moe(h2, route, x, mods, w_in, w_out, norm_final):
    src_tok, dst_row, tile_expert, tile_active, tile_count = _moe_routing(route)
    ys = _moe_group(h2, src_tok, dst_row, tile_expert, tile_active, tile_count, w_in, w_out)
    return _moe_combine(ys, route, x, mods, norm_final)


def _hgrn_lower_bounds(lb_param):
    sm = jax.nn.softmax(lb_param.astype(F32), axis=0)
    return jnp.cumsum(sm, axis=0) - sm[0:1]


def kernel(x_prompt, x_sample, cache_k, cache_v, state_hgrn, c, c_ctx, w_ada, b_ada, norm_mix, norm_ffn,
           w_in, hgrn_lb, hgrn_norm, pool_w, pool_scale, rpb, w_branch_a, w_branch_b, w_branch_c, w_out,
           ffn_w_in, ffn_w_out, router, moe_w_in, moe_w_out, norm_final):
    x = (x_prompt.reshape(N_CTX, D), x_sample.reshape(N_LAT, D))
    cond8 = jnp.concatenate([c_ctx[None], c, jnp.zeros((5, D), F32)], axis=0)
    mods = _adaln(cond8, w_ada, b_ada)[:, :3].reshape(DEPTH, 3, 6, D)
    lbs = _hgrn_lower_bounds(hgrn_lb)
    ck = cache_k.reshape(N_LAT_B, DEPTH, PAST, WIDTH).astype(BF16)
    cv = cache_v.reshape(N_LAT_B, DEPTH, PAST, WIDTH).astype(BF16)
    w_in16 = w_in.astype(BF16)
    router_pad = jnp.pad(router[0], ((0, 0), (0, 128 - N_EXP)))
    r_hi = router_pad.astype(BF16)
    r_lo = (router_pad - r_hi.astype(F32)).astype(BF16)

    kvs, ss = [], []
    for l in range(DEPTH):
        z, zg, kv = _inproj(x, norm_mix[l], mods[l], w_in16, l)
        of_c, ob_c, s_ctx = _hgrn(z, zg, lbs[l], None, 0, N_CTX_B, CTX_T)
        of_l, ob_l, _ = _hgrn(z, zg, lbs[l], state_hgrn[:, l], N_CTX, N_LAT_B, LAT_T)
        o_pool = _pool(z, pool_w[l].astype(BF16), pool_scale[l])
        oc_c = _ctx_attn(z)
        oc_l = _na_attn(z, ck, cv, _na_bias_table(rpb[l]), l)
        merged = _merge((of_c, ob_c, oc_c), (of_l, ob_l, oc_l), z, o_pool, x, mods[l], hgrn_norm[l], norm_ffn[l],
                        w_branch_a[l].astype(BF16), w_branch_b[l].astype(BF16),
                        w_branch_c[l].astype(BF16), w_out[l].astype(BF16),
                        (r_hi, r_lo) if l % 2 == 1 else None)
        if l % 2 == 0:
            x, h2 = merged
            x = _ffn(h2, x, mods[l], ffn_w_in.astype(BF16), ffn_w_out.astype(BF16))
        else:
            x, h2, route = merged
            y_ctx, y_lat = _moe(h2, route, x, mods[l], moe_w_in, moe_w_out, norm_final)
        kvs.append(kv.reshape(N_CTX_B, CTX_T, 2, NA_H, NA_D))
        ss.append(s_ctx)
    kv = jnp.stack(kvs, axis=1)
    y_prompt = y_ctx.reshape(N_CTX_B, CTX_T, D)
    y_sample = y_lat.reshape(N_LAT_B, LAT_T, D)
    return (y_prompt, y_sample, kv[:, :, :, 0], kv[:, :, :, 1], jnp.stack(ss, axis=1))
```

```python
import functools

import numpy as np
import jax
import jax.numpy as jnp
from jax import lax
from jax.experimental import pallas as pl
from jax.experimental.pallas import tpu as pltpu

F32 = jnp.float32
BF16 = jnp.bfloat16

D = 1024
N_CTX_B, CTX_T = 32, 256
N_LAT_B, LAT_T = 2, 4096
N_CTX = N_CTX_B * CTX_T
N_LAT = N_LAT_B * LAT_T
N_TOK = N_CTX + N_LAT
DEPTH = 2
GRID_W = 64
GRID_ROWS = LAT_T // GRID_W
PAST = 512
HG_H, HG_D = 4, 128
WIDTH = 512
NA_H, NA_D = 8, 64
WIN_ROWS, WIN_COLS = 8, 16
POOL_WINDOWS = (2, 4, 8, 16)
IN_COLS = 7680
F_DENSE = 2816
N_EXP = 8
F_EXP = 3584
GATE_CLIP = 30.0
EPS = 1e-6
NEG_BIG = -1e30

COL_AQ, COL_AFF, COL_AFB, COL_AI, COL_AG, COL_BU, COL_CQ, COL_CK, COL_CV = range(9)
GL_COL0 = 9 * WIDTH

ROUTE_I1, ROUTE_I2, ROUTE_W1, ROUTE_W2 = 8, 9, 10, 11

VMEM_LIMIT = 56 * 1024 * 1024


def _cparams(sem):
    return pltpu.CompilerParams(dimension_semantics=sem, vmem_limit_bytes=VMEM_LIMIT)


def _sigmoid(x):
    return 1.0 / (1.0 + jnp.exp(-x))


def _silu(x):
    return x / (1.0 + jnp.exp(-x))


def _group_of_rows(row0):
    return jnp.maximum(row0 - N_CTX + LAT_T, 0) // LAT_T


def _adaln_kernel(c_ref, w_ref, b_ref, o_ref):
    s = _silu(c_ref[...]).astype(BF16)
    o_ref[...] = jnp.dot(s, w_ref[...].astype(BF16), preferred_element_type=F32) + b_ref[...]


def _adaln(cond8, w_ada, b_ada):
    tn = 1536
    return pl.pallas_call(
        _adaln_kernel,
        grid=(DEPTH, 6 * D // tn),
        in_specs=[pl.BlockSpec((8, D), lambda l, j: (0, 0)),
                  pl.BlockSpec((None, D, tn), lambda l, j: (l, 0, j)),
                  pl.BlockSpec((None, 1, tn), lambda l, j: (l, 0, j))],
        out_specs=pl.BlockSpec((None, 8, tn), lambda l, j: (l, 0, j)),
        out_shape=jax.ShapeDtypeStruct((DEPTH, 8, 6 * D), F32),
        compiler_params=_cparams(("parallel", "parallel")),
        name="adaln",
    )(cond8, w_ada, b_ada.reshape(DEPTH, 1, 6 * D))


INPROJ_BM, INPROJ_BN = 1024, 3 * WIDTH


def _stream_rows(x, bm):
    if not isinstance(x, tuple):
        specs = [pl.BlockSpec((bm, D), lambda i, *_: (i, 0))]
        return specs, [x]
    nct = N_CTX // bm
    specs = [pl.BlockSpec((bm, D), lambda i, *_: (jnp.minimum(i, nct - 1), 0)),
             pl.BlockSpec((bm, D), lambda i, *_: (jnp.maximum(i - nct, 0), 0))]
    return specs, list(x)


def _read_stream_rows(x_refs, bm):
    if len(x_refs) == 1:
        return x_refs[0][...]
    return jnp.where(pl.program_id(0) < N_CTX // bm, x_refs[0][...], x_refs[1][...])


def _inproj_kernel(*refs):
    nw_ref, m_ref, w_ref, z_ref, zg_ref, kv_ref, h_scr = refs[-7:]
    x_refs = refs[:-7]
    i = pl.program_id(0)
    j = pl.program_id(1)

    @pl.when(j == 0)
    def _():
        x = _read_stream_rows(x_refs, INPROJ_BM)
        y = x * lax.rsqrt(jnp.mean(x * x, axis=-1, keepdims=True) + EPS) * nw_ref[...]
        h_scr[...] = (y * (1.0 + m_ref[1:2, :]) + m_ref[0:1, :]).astype(BF16)

    acc = jnp.dot(h_scr[...], w_ref[...], preferred_element_type=F32)

    z_ref[...] = acc.astype(BF16)

    @pl.when(j == 0)
    def _():
        zg_ref[...] = acc[:, WIDTH:3 * WIDTH]

    @pl.when((j == 2) & (i < N_CTX // INPROJ_BM))
    def _():
        kv_ref[...] = acc[:, WIDTH:3 * WIDTH]


def _inproj(x, norm_w, mods, w_in16, layer):
    bm, bn = INPROJ_BM, INPROJ_BN
    nct = N_CTX // bm
    x_specs, x_args = _stream_rows(x, bm)
    return pl.pallas_call(
        _inproj_kernel,
        grid=(N_TOK // bm, IN_COLS // bn),
        in_specs=x_specs + [
            pl.BlockSpec((1, D), lambda i, j: (0, 0)),
            pl.BlockSpec((None, 6, D), lambda i, j: (_group_of_rows(i * bm), 0, 0)),
            pl.BlockSpec((None, D, bn), lambda i, j: (layer, 0, j))],
        out_specs=[pl.BlockSpec((bm, bn), lambda i, j: (i, j)),
                   pl.BlockSpec((bm, 2 * WIDTH), lambda i, j: (i, 0)),
                   pl.BlockSpec((bm, 2 * WIDTH), lambda i, j: (jnp.minimum(i, nct - 1), 0))],
        out_shape=[jax.ShapeDtypeStruct((N_TOK, IN_COLS), BF16),
                   jax.ShapeDtypeStruct((N_TOK, 2 * WIDTH), F32),
                   jax.ShapeDtypeStruct((N_CTX, 2 * WIDTH), F32)],
        scratch_shapes=[pltpu.VMEM((bm, D), BF16)],
        compiler_params=_cparams(("arbitrary", "arbitrary")),
        name="inproj",
    )(*x_args, norm_w.reshape(1, D), mods, w_in16)


HG_C = 128
HG_LEVELS = (4, 8, 16, 32, 64, 128)
LOG2_E = 1.4426950408889634
EXP2_CAP = 120.0


def _hgrn_level_ids(reverse):
    t = np.arange(HG_C)[:, None]
    s = np.arange(HG_C)[None, :]
    if reverse:
        t, s = s, t
    lev = np.full((HG_C, HG_C), -1, np.int32)
    lev[(t // 4 == s // 4) & (s <= t)] = 0
    for li, L in enumerate(HG_LEVELS[1:], start=1):
        m = (t // L == s // L) & (t % L >= L // 2) & (s % L < L // 2)
        lev[m] = li
    return lev


def _hgrn_ref_rows(b_scr, d, reverse):
    out = []
    r_lo, r_hi = (2, 6) if reverse else (1, 5)
    sub = lax.broadcasted_iota(jnp.int32, (8, WIDTH), 0)
    pieces = []
    for g in range(HG_C // 8):
        lo = jnp.broadcast_to(b_scr[d, 8 * g + r_lo:8 * g + r_lo + 1, :], (8, WIDTH))
        hi = jnp.broadcast_to(b_scr[d, 8 * g + r_hi:8 * g + r_hi + 1, :], (8, WIDTH))
        pieces.append(jnp.where(sub < 4, lo, hi))
    out.append(jnp.concatenate(pieces, axis=0))
    for L in HG_LEVELS[1:]:
        r = L // 2 - 1 if reverse else L // 2
        pieces = [jnp.broadcast_to(b_scr[d, L * g + r:L * g + r + 1, :], (L, WIDTH))
                  for g in range(HG_C // L)]
        out.append(pieces[0] if len(pieces) == 1 else jnp.concatenate(pieces, axis=0))
    return out


def _hgrn_kernel(*refs, has_init):
    if has_init:
        (qf_ref, ff_ref, vf_ref, qb_ref, fb_ref, vb_ref, lb_ref, levf_ref, levb_ref, s0_ref,
         of_ref, ob_ref, so_ref, st_scr, b_scr) = refs
    else:
        (qf_ref, ff_ref, vf_ref, qb_ref, fb_ref, vb_ref, lb_ref, levf_ref, levb_ref,
         of_ref, ob_ref, so_ref, st_scr, b_scr) = refs
    c = pl.program_id(1)
    C = HG_C

    @pl.when(c == 0)
    def _():
        for d in range(2):
            for h in range(HG_H):
                if has_init:
                    st_scr[d, h] = s0_ref[d, h].T
                else:
                    st_scr[d, h] = jnp.zeros((HG_D, HG_D), F32)

    row = lax.broadcasted_iota(jnp.int32, (C, C), 0)
    col = lax.broadcasted_iota(jnp.int32, (C, C), 1)
    dirs = ((qf_ref, ff_ref, vf_ref, levf_ref, of_ref), (qb_ref, fb_ref, vb_ref, levb_ref, ob_ref))
    for d, (q_ref, f_ref, v_ref, lev_ref, o_ref) in enumerate(dirs):
        reverse = d == 1
        tri = jnp.where((col >= row) if reverse else (col <= row), 1.0, 0.0).astype(BF16)
        q = _silu(q_ref[...].astype(F32))
        fx = jnp.clip(f_ref[...], -GATE_CLIP, GATE_CLIP)
        e = jnp.exp(-fx)
        sig_pos = 1.0 / (1.0 + e)
        sig_neg = e * sig_pos
        lb = lb_ref[d:d + 1, :]
        lf = jnp.log(lb + (1.0 - lb) * sig_pos) * LOG2_E
        k = (1.0 - lb) * sig_neg
        q16 = q.astype(BF16)
        k16 = k.astype(BF16)
        hi = lf.astype(BF16)
        r1 = lf - hi.astype(F32)
        mid = r1.astype(BF16)
        lo = (r1 - mid.astype(F32)).astype(BF16)
        b = (jnp.dot(tri, hi, preferred_element_type=F32)
             + jnp.dot(tri, mid, preferred_element_type=F32)
             + jnp.dot(tri, lo, preferred_element_type=F32))
        b_scr[d] = b
        refs_m = _hgrn_ref_rows(b_scr, d, reverse)
        qs, ks = [], []
        for li, m in enumerate(refs_m):
            dlt = b - m
            if li == 0:
                qs.append(q16 * jnp.exp2(jnp.minimum(dlt, EXP2_CAP)).astype(BF16))
                ks.append(k16 * jnp.exp2(jnp.minimum(-dlt, EXP2_CAP)).astype(BF16))
            else:
                fac = jnp.exp2(-jnp.abs(dlt)).astype(BF16)
                qs.append(q16 * fac)
                ks.append(k16 * fac)
        lev = lev_ref[...]
        b_end = b[0:1, :] if reverse else b[C - 1:C, :]
        q_in = q16 * jnp.exp2(b).astype(BF16)
        k_out = k16 * jnp.exp2(b_end - b).astype(BF16)
        dec = jnp.exp2(b_end)
        vb16 = v_ref[...]
        for h in range(HG_H):
            sl = slice(h * HG_D, (h + 1) * HG_D)
            a = jnp.zeros((C, C), F32)
            for li in range(len(HG_LEVELS)):
                p = lax.dot_general(qs[li][:, sl], ks[li][:, sl], (((1,), (1,)), ((), ())),
                                    preferred_element_type=F32)
                a = jnp.where(lev == li, p, a)
            vh = vb16[:, sl]
            st = st_scr[d, h]
            o = (jnp.dot(a.astype(BF16), vh, preferred_element_type=F32)
                 + lax.dot_general(q_in[:, sl], st.astype(BF16), (((1,), (1,)), ((), ())),
                                   preferred_element_type=F32))
            o_ref[:, sl] = o
            upd = lax.dot_general(vh, k_out[:, sl], (((0,), (0,)), ((), ())),
                                  preferred_element_type=F32)
            st_scr[d, h] = st * dec[:, sl] + upd

    @pl.when(c == pl.num_programs(1) - 1)
    def _():
        for d in range(2):
            for h in range(HG_H):
                so_ref[d, h] = st_scr[d, h].T


def _hgrn(z, zg, lb, s0, row_off, n_b, seq_t):
    C = HG_C
    nc = seq_t // C
    base = row_off // C
    has_init = s0 is not None

    def fwd(col):
        return pl.BlockSpec((C, WIDTH), lambda b, c: (base + b * nc + c, col))

    def bwd(col):
        return pl.BlockSpec((C, WIDTH), lambda b, c: (base + b * nc + nc - 1 - c, col))

    full = lambda shape: pl.BlockSpec(shape, lambda b, c: (0,) * len(shape))
    in_specs = [fwd(COL_AQ), fwd(0), fwd(COL_AI), bwd(COL_AQ), bwd(1), bwd(COL_AI),
                full((2, WIDTH)), full((C, C)), full((C, C))]
    args = [z, zg, z, z, zg, z, lb, jnp.asarray(_hgrn_level_ids(False)), jnp.asarray(_hgrn_level_ids(True))]
    if has_init:
        in_specs.append(pl.BlockSpec((None, 2, HG_H, HG_D, HG_D), lambda b, c: (b, 0, 0, 0, 0)))
        args.append(s0)
    return pl.pallas_call(
        functools.partial(_hgrn_kernel, has_init=has_init),
        grid=(n_b, nc),
        in_specs=in_specs,
        out_specs=[pl.BlockSpec((C, WIDTH), lambda b, c: (b * nc + c, 0)),
                   pl.BlockSpec((C, WIDTH), lambda b, c: (b * nc + nc - 1 - c, 0)),
                   pl.BlockSpec((None, 2, HG_H, HG_D, HG_D), lambda b, c: (b, 0, 0, 0, 0))],
        out_shape=[jax.ShapeDtypeStruct((n_b * seq_t, WIDTH), F32),
                   jax.ShapeDtypeStruct((n_b * seq_t, WIDTH), F32),
                   jax.ShapeDtypeStruct((n_b, 2, HG_H, HG_D, HG_D), F32)],
        scratch_shapes=[pltpu.VMEM((2, HG_H, HG_D, HG_D), F32),
                        pltpu.VMEM((2, C, WIDTH), F32)],
        compiler_params=_cparams(("parallel", "arbitrary")),
        name="hgrn_lat" if has_init else "hgrn_ctx",
    )(*args)


POOL_T = 256
POOL_HALO = 16


def _pool_kernel(u_ref, up_ref, un_ref, w_ref, sc_ref, o_ref):
    i = pl.program_id(0)
    is_lat = i >= N_CTX // POOL_T
    seq_t = jnp.where(is_lat, LAT_T, CTX_T)
    t0 = jnp.where(is_lat, ((i - N_CTX // POOL_T) % (LAT_T // POOL_T)) * POOL_T, 0)
    TT, HL = POOL_T, POOL_HALO
    diff = lax.broadcasted_iota(jnp.int32, (TT, TT), 1) - lax.broadcasted_iota(jnp.int32, (TT, TT), 0)
    diff_h = lax.broadcasted_iota(jnp.int32, (TT, HL), 1) - lax.broadcasted_iota(jnp.int32, (TT, HL), 0)
    t_glob = t0 + lax.broadcasted_iota(jnp.int32, (TT, 128), 0)
    has_prev = t0 > 0
    has_next = t0 + TT < seq_t
    for g, w in enumerate(POOL_WINDOWS):
        half = w // 2
        sl = slice(g * 128, (g + 1) * 128)
        band = jnp.where((diff >= -half) & (diff < half), 1.0, 0.0).astype(BF16)
        dp = diff_h - HL
        band_p = jnp.where((dp >= -half) & (dp < half) & has_prev, 1.0, 0.0).astype(BF16)
        dn = diff_h + TT
        band_n = jnp.where((dn >= -half) & (dn < half) & has_next, 1.0, 0.0).astype(BF16)
        u = u_ref[:, sl]
        s = (jnp.dot(band, u, preferred_element_type=F32)
             + jnp.dot(band_p, up_ref[:, sl], preferred_element_type=F32)
             + jnp.dot(band_n, un_ref[:, sl], preferred_element_type=F32))
        cnt = (jnp.minimum(t_glob + half, seq_t) - jnp.maximum(t_glob - half, 0)).astype(F32)
        dd = s / cnt - u.astype(F32)
        y = jnp.dot(dd.astype(BF16), w_ref[g], preferred_element_type=F32)
        o_ref[:, sl] = y * sc_ref[:, sl]


def _pool(z, pool_w16, pool_scale):
    nt = N_TOK // POOL_T
    per = POOL_T // POOL_HALO
    return pl.pallas_call(
        _pool_kernel,
        grid=(nt,),
        in_specs=[pl.BlockSpec((POOL_T, WIDTH), lambda i: (i, COL_BU)),
                  pl.BlockSpec((POOL_HALO, WIDTH), lambda i: (jnp.maximum(i * per - 1, 0), COL_BU)),
                  pl.BlockSpec((POOL_HALO, WIDTH), lambda i: (jnp.minimum((i + 1) * per, nt * per - 1), COL_BU)),
                  pl.BlockSpec((4, 128, 128), lambda i: (0, 0, 0)),
                  pl.BlockSpec((1, WIDTH), lambda i: (0, 0))],
        out_specs=pl.BlockSpec((POOL_T, WIDTH), lambda i: (i, 0)),
        out_shape=jax.ShapeDtypeStruct((N_TOK, WIDTH), F32),
        compiler_params=_cparams(("parallel",)),
        name="pool",
    )(z, z, z, pool_w16, pool_scale.reshape(1, WIDTH))


def _head_mask(hh):
    lane = lax.broadcasted_iota(jnp.int32, (1, 128), 1)
    in_head = (lane >= hh * NA_D) & (lane < (hh + 1) * NA_D)
    return jnp.where(in_head, NA_D ** -0.5, 0.0).astype(BF16)


def _ctx_attn_kernel(q_ref, k_ref, v_ref, o_ref):
    lane = lax.broadcasted_iota(jnp.int32, (CTX_T, 128), 1)
    for j in range(NA_H // 2):
        sl = slice(j * 128, (j + 1) * 128)
        q = q_ref[:, sl]
        kt = k_ref[:, sl]
        vt = v_ref[:, sl]
        outs = []
        for hh in range(2):
            s = lax.dot_general(q * _head_mask(hh), kt, (((1,), (1,)), ((), ())), preferred_element_type=F32)
            m = jnp.max(s, axis=-1, keepdims=True)
            p = jnp.exp(s - m)
            l = jnp.sum(p, axis=-1, keepdims=True)
            outs.append(jnp.dot(p.astype(BF16), vt, preferred_element_type=F32) / l)
        o_ref[:, sl] = jnp.where(lane < NA_D, outs[0], outs[1])


def _ctx_attn(z):
    spec = lambda col: pl.BlockSpec((CTX_T, WIDTH), lambda b: (b, col))
    return pl.pallas_call(
        _ctx_attn_kernel,
        grid=(N_CTX_B,),
        in_specs=[spec(COL_CQ), spec(COL_CK), spec(COL_CV)],
        out_specs=pl.BlockSpec((CTX_T, WIDTH), lambda b: (b, 0)),
        out_shape=jax.ShapeDtypeStruct((N_CTX, WIDTH), F32),
        compiler_params=_cparams(("parallel",)),
        name="ctx_attn",
    )(z, z, z)


NA_QR = 4
NA_KR = 12
NA_NQ = NA_QR * GRID_W
NA_NK = NA_KR * GRID_W
NA_BLOCKS = GRID_ROWS // NA_QR


def _na_key_row0(blk):
    return jnp.clip(blk * NA_QR - WIN_ROWS // 2, 0, GRID_ROWS - NA_KR)


def _na_geometry():
    patterns, var_of_block = [], []
    for blk in range(NA_BLOCKS):
        r = blk * NA_QR + np.arange(NA_QR)[:, None]
        kr = int(np.clip(blk * NA_QR - WIN_ROWS // 2, 0, GRID_ROWS - NA_KR)) + np.arange(NA_KR)[None, :]
        rs = np.clip(r - WIN_ROWS // 2, 0, GRID_ROWS - WIN_ROWS)
        valid = (kr >= rs) & (kr < rs + WIN_ROWS)
        assert (valid.sum(axis=1) == WIN_ROWS).all(), "key rows must cover every query row's window"
        drow = np.where(valid, kr - r + WIN_ROWS - 1, 0)
        key = (drow.tobytes(), valid.tobytes())
        ids = [i for i, (k_, _, _) in enumerate(patterns) if k_ == key]
        if not ids:
            patterns.append((key, drow, valid))
            ids = [len(patterns) - 1]
        var_of_block.append(ids[0])
    drow = np.stack([p[1] for p in patterns])
    valid = np.stack([p[2] for p in patterns])
    return np.asarray(var_of_block, np.int32), drow, valid


NA_DROWS = 2 * WIN_ROWS - 1


def _na_bias_kernel(didx_ref, t2_ref, o_ref):
    v = pl.program_id(0)
    for qr in range(NA_QR):
        for kp in range(NA_KR // 2):
            base = (v * NA_QR + qr) * NA_KR + 2 * kp
            pair = jnp.concatenate([t2_ref[didx_ref[base]], t2_ref[didx_ref[base + 1]]], axis=1)
            o_ref[qr * GRID_W:(qr + 1) * GRID_W, kp * 128:(kp + 1) * 128] = pair


def _na_bias_table(rpb_l):
    _, drow, valid = _na_geometry()
    n_var = valid.shape[0]
    qc = np.arange(GRID_W)[:, None]
    kc = np.arange(GRID_W)[None, :]
    q0 = np.clip(qc - WIN_COLS // 2, 0, GRID_W - WIN_COLS)
    col_in = (kc >= q0) & (kc < q0 + WIN_COLS)
    dcol = np.clip(kc - qc, -(WIN_COLS - 1), WIN_COLS - 1) + WIN_COLS - 1
    oh_col = (dcol[None] == np.arange(2 * WIN_COLS - 1)[:, None, None]).astype(np.float32)
    t2 = jnp.einsum('hdc,cqk->hdqk', rpb_l, jnp.asarray(oh_col), precision=lax.Precision.HIGHEST)
    t2 = jnp.where(col_in[None, None], t2, NEG_BIG)
    t2 = jnp.concatenate([t2, jnp.full((NA_H, 1, GRID_W, GRID_W), NEG_BIG, F32)], axis=1)
    didx = np.where(valid, drow, NA_DROWS).astype(np.int32).reshape(-1)
    grid_spec = pltpu.PrefetchScalarGridSpec(
        num_scalar_prefetch=1,
        grid=(n_var, NA_H),
        in_specs=[pl.BlockSpec((None, NA_DROWS + 1, GRID_W, GRID_W), lambda v, h, didx: (h, 0, 0, 0))],
        out_specs=pl.BlockSpec((None, None, NA_NQ, NA_NK), lambda v, h, didx: (v, h, 0, 0)))
    return pl.pallas_call(
        _na_bias_kernel,
        grid_spec=grid_spec,
        out_shape=jax.ShapeDtypeStruct((n_var, NA_H, NA_NQ, NA_NK), F32),
        compiler_params=_cparams(("parallel", "parallel")),
        name="na_bias",
    )(jnp.asarray(didx), t2)


def _na_attn_kernel(var_ref, q_ref, k_ref, v_ref, kc_ref, vc_ref, tbl_ref, o_ref):
    del var_ref
    blk = pl.program_id(1)
    k0 = pl.multiple_of(_na_key_row0(blk) * GRID_W, GRID_W)
    lane = lax.broadcasted_iota(jnp.int32, (NA_NQ, 128), 1)
    for j in range(NA_H // 2):
        sl = slice(j * 128, (j + 1) * 128)
        q = q_ref[:, sl]
        kt = k_ref[pl.ds(k0, NA_NK), sl]
        vt = v_ref[pl.ds(k0, NA_NK), sl]
        kct = kc_ref[:, sl]
        vct = vc_ref[:, sl]
        outs = []
        for hh in range(2):
            qm = q * _head_mask(hh)
            s_loc = lax.dot_general(qm, kt, (((1,), (1,)), ((), ())), preferred_element_type=F32)
            s_ctx = lax.dot_general(qm, kct, (((1,), (1,)), ((), ())), preferred_element_type=F32)
            tb = tbl_ref[2 * j + hh]
            s_loc = jnp.where(tb > 0.5 * NEG_BIG, s_loc + tb, NEG_BIG)
            m = jnp.maximum(jnp.max(s_loc, axis=-1, keepdims=True), jnp.max(s_ctx, axis=-1, keepdims=True))
            p_loc = jnp.exp(s_loc - m)
            p_ctx = jnp.exp(s_ctx - m)
            l = jnp.sum(p_loc, axis=-1, keepdims=True) + jnp.sum(p_ctx, axis=-1, keepdims=True)
            o = (jnp.dot(p_loc.astype(BF16), vt, preferred_element_type=F32)
                 + jnp.dot(p_ctx.astype(BF16), vct, preferred_element_type=F32))
            outs.append(o / l)
        o_ref[:, sl] = jnp.where(lane < NA_D, outs[0], outs[1])


def _na_attn(z, cache_k16, cache_v16, tbl, layer):
    base_q = N_CTX // NA_NQ
    base_t = N_CTX // LAT_T
    ctx_spec = pl.BlockSpec((None, None, PAST, WIDTH), lambda b, r, var: (b, layer, 0, 0))
    grid_spec = pltpu.PrefetchScalarGridSpec(
        num_scalar_prefetch=1,
        grid=(N_LAT_B, NA_BLOCKS),
        in_specs=[pl.BlockSpec((NA_NQ, WIDTH), lambda b, r, var: (base_q + b * NA_BLOCKS + r, COL_CQ)),
                  pl.BlockSpec((LAT_T, WIDTH), lambda b, r, var: (base_t + b, COL_CK)),
                  pl.BlockSpec((LAT_T, WIDTH), lambda b, r, var: (base_t + b, COL_CV)),
                  ctx_spec, ctx_spec,
                  pl.BlockSpec((None, NA_H, NA_NQ, NA_NK), lambda b, r, var: (var[r], 0, 0, 0))],
        out_specs=pl.BlockSpec((NA_NQ, WIDTH), lambda b, r, var: (b * NA_BLOCKS + r, 0)))
    return pl.pallas_call(
        _na_attn_kernel,
        grid_spec=grid_spec,
        out_shape=jax.ShapeDtypeStruct((N_LAT, WIDTH), F32),
        compiler_params=_cparams(("parallel", "arbitrary")),
        name="na_attn",
    )(jnp.asarray(_na_geometry()[0]), z, z, z, cache_k16, cache_v16, tbl)


MERGE_BM = 512


def _merge_kernel(*refs, with_router, n_x):
    x_refs, refs = refs[:n_x], refs[n_x:]
    (ofc_ref, obc_ref, occ_ref, ofl_ref, obl_ref, ocl_ref, ag_ref, op_ref, gl0_ref, gl1_ref, m_ref,
     hn_ref, nf_ref, wa_ref, wb_ref, wc_ref, wo_ref) = refs[:17]
    if with_router:
        rhi_ref, rlo_ref, xn_ref, h2_ref, route_ref = refs[17:]
    else:
        xn_ref, h2_ref = refs[17:]
    is_ctx = pl.program_id(0) < N_CTX // MERGE_BM
    o = jnp.where(is_ctx, ofc_ref[...] + obc_ref[...], ofl_ref[...] + obl_ref[...])
    oc = jnp.where(is_ctx, occ_ref[...], ocl_ref[...])
    parts = []
    for h in range(HG_H):
        oh = o[:, h * HG_D:(h + 1) * HG_D]
        parts.append(oh * lax.rsqrt(jnp.mean(oh * oh, axis=-1, keepdims=True) + EPS))
    oa = jnp.concatenate(parts, axis=1) * hn_ref[...] * _silu(ag_ref[...].astype(F32))
    sg0 = _sigmoid(gl0_ref[...]).astype(F32)
    sg1 = _sigmoid(gl1_ref[...]).astype(F32)
    sga = sg0[:, :D]
    sgb = jnp.concatenate([sg0[:, D:], sg1[:, :WIDTH]], axis=1)
    sgc = sg1[:, WIDTH:]
    mix = (sga * jnp.dot(oa.astype(BF16), wa_ref[...], preferred_element_type=F32)
           + sgb * jnp.dot(op_ref[...].astype(BF16), wb_ref[...], preferred_element_type=F32)
           + sgc * jnp.dot(oc.astype(BF16), wc_ref[...], preferred_element_type=F32))
    xn = (_read_stream_rows(x_refs, MERGE_BM)
          + m_ref[2:3, :] * jnp.dot(mix.astype(BF16), wo_ref[...], preferred_element_type=F32))
    xn_ref[...] = xn
    y = xn * lax.rsqrt(jnp.mean(xn * xn, axis=-1, keepdims=True) + EPS) * nf_ref[...]
    h2 = y * (1.0 + m_ref[4:5, :]) + m_ref[3:4, :]
    h2_ref[...] = h2.astype(h2_ref.dtype)
    if not with_router:
        return
    hhi = h2.astype(BF16)
    hlo = (h2 - hhi.astype(F32)).astype(BF16)
    logits = (jnp.dot(hhi, rhi_ref[...], preferred_element_type=F32)
              + jnp.dot(hhi, rlo_ref[...], preferred_element_type=F32)
              + jnp.dot(hlo, rhi_ref[...], preferred_element_type=F32))
    lane = lax.broadcasted_iota(jnp.int32, logits.shape, 1).astype(F32)
    lg = jnp.where(lane < N_EXP, logits, -jnp.inf)
    m1 = jnp.max(lg, axis=-1, keepdims=True)
    i1 = jnp.min(jnp.where(lg == m1, lane, 128.0), axis=-1, keepdims=True)
    lg2 = jnp.where(lane == i1, -jnp.inf, lg)
    m2 = jnp.max(lg2, axis=-1, keepdims=True)
    i2 = jnp.min(jnp.where(lg2 == m2, lane, 128.0), axis=-1, keepdims=True)
    e = jnp.exp(m2 - m1)
    w1 = 1.0 / (1.0 + e)
    route_ref[...] = (jnp.where(lane == ROUTE_I1, i1, 0.0) + jnp.where(lane == ROUTE_I2, i2, 0.0)
                      + jnp.where(lane == ROUTE_W1, w1, 0.0) + jnp.where(lane == ROUTE_W2, e * w1, 0.0))


def _merge(ctx_parts, lat_parts, z, o_pool, x, mods, hgrn_norm_l, norm_ffn_l, wa, wb, wc, wo, router_split):
    bm = MERGE_BM
    glw = 1536
    nct = N_CTX // bm
    with_router = router_split is not None
    row = lambda w, col=0: pl.BlockSpec((bm, w), lambda i: (i, col))
    ctx_row = pl.BlockSpec((bm, WIDTH), lambda i: (jnp.minimum(i, nct - 1), 0))
    lat_row = pl.BlockSpec((bm, WIDTH), lambda i: (jnp.maximum(i - nct, 0), 0))
    const = lambda shape: pl.BlockSpec(shape, lambda i: (0,) * len(shape))
    x_specs, x_args = _stream_rows(x, bm)
    in_specs = x_specs + [ctx_row] * 3 + [lat_row] * 3 + [
        row(WIDTH, COL_AG), row(WIDTH),
        row(glw, GL_COL0 // glw), row(glw, GL_COL0 // glw + 1),
        pl.BlockSpec((None, 6, D), lambda i: (_group_of_rows(i * bm), 0, 0)),
        const((1, WIDTH)), const((1, D)),
        const((WIDTH, D)), const((WIDTH, D)), const((WIDTH, D)), const((D, D))]
    args = x_args + list(ctx_parts) + list(lat_parts) + [
        z, o_pool, z, z, mods, jnp.tile(hgrn_norm_l, HG_H).reshape(1, WIDTH),
        norm_ffn_l.reshape(1, D), wa, wb, wc, wo]
    out_specs = [row(D), row(D)]
    out_shape = [jax.ShapeDtypeStruct((N_TOK, D), F32),
                 jax.ShapeDtypeStruct((N_TOK, D), F32 if with_router else BF16)]
    if with_router:
        in_specs += [const((D, 128)), const((D, 128))]
        args += list(router_split)
        out_specs.append(row(128))
        out_shape.append(jax.ShapeDtypeStruct((N_TOK, 128), F32))
    return pl.pallas_call(
        functools.partial(_merge_kernel, with_router=with_router, n_x=len(x_args)),
        grid=(N_TOK // bm,),
        in_specs=in_specs,
        out_specs=out_specs,
        out_shape=out_shape,
        compiler_params=_cparams(("parallel",)),
        name="merge_route" if with_router else "merge",
    )(*args)


def _ffn_kernel(h_ref, x_ref, m_ref, wa_ref, wb_ref, wo_ref, o_ref, acc_scr):
    f = pl.program_id(1)

    @pl.when(f == 0)
    def _():
        acc_scr[...] = jnp.zeros(acc_scr.shape, F32)

    h = h_ref[...]
    a = jnp.dot(h, wa_ref[...].astype(BF16), preferred_element_type=F32)
    b = jnp.dot(h, wb_ref[...].astype(BF16), preferred_element_type=F32)
    g = (_silu(a) * b).astype(BF16)
    acc_scr[...] += jnp.dot(g, wo_ref[...].astype(BF16), preferred_element_type=F32)

    @pl.when(f == pl.num_programs(1) - 1)
    def _():
        o_ref[...] = x_ref[...] + m_ref[5:6, :] * acc_scr[...]


def _ffn(h2, x, mods, w_in, w_out):
    bm, tf = 1024, 256
    nf = F_DENSE // tf
    return pl.pallas_call(
        _ffn_kernel,
        grid=(N_TOK // bm, nf),
        in_specs=[pl.BlockSpec((bm, D), lambda i, f: (i, 0)),
                  pl.BlockSpec((bm, D), lambda i, f: (i, 0)),
                  pl.BlockSpec((None, 6, D), lambda i, f: (_group_of_rows(i * bm), 0, 0)),
                  pl.BlockSpec((None, D, tf), lambda i, f: (0, 0, f)),
                  pl.BlockSpec((None, D, tf), lambda i, f: (0, 0, nf + f)),
                  pl.BlockSpec((None, tf, D), lambda i, f: (0, f, 0))],
        out_specs=pl.BlockSpec((bm, D), lambda i, f: (i, 0)),
        out_shape=jax.ShapeDtypeStruct((N_TOK, D), F32),
        scratch_shapes=[pltpu.VMEM((bm, D), F32)],
        compiler_params=_cparams(("parallel", "arbitrary")),
        name="ffn",
    )(h2, x, mods, w_in, w_in, w_out)


MOE_BM = 1024
MOE_TF = 512
MOE_NF = F_EXP // MOE_TF
MOE_PAIRS = 2 * N_TOK
MOE_ROWS = MOE_PAIRS + N_EXP * MOE_BM
MOE_TILES = MOE_ROWS // MOE_BM
MOE_CH = -(-MOE_BM // MOE_NF)
MOE_MOVED = MOE_CH * MOE_NF
MOE_BUF = -(-MOE_MOVED // 8) * 8
ROW_DMA_PRIORITY = 1


def _moe_routing(route):
    i1 = route[:, ROUTE_I1].astype(jnp.int32)
    i2 = route[:, ROUTE_I2].astype(jnp.int32)
    ep = jnp.stack([i1, i2], axis=1).reshape(-1)
    onehot = (ep[:, None] == jnp.arange(N_EXP, dtype=jnp.int32)[None, :]).astype(jnp.int32)
    counts = jnp.sum(onehot, axis=0)
    padded = ((counts + MOE_BM - 1) // MOE_BM) * MOE_BM
    ends = jnp.cumsum(padded)
    starts = ends - padded
    first = jnp.cumsum(counts) - counts
    order = jnp.sort(ep * MOE_PAIRS + jnp.arange(MOE_PAIRS, dtype=jnp.int32)) & (MOE_PAIRS - 1)
    tile_row0 = jnp.arange(MOE_TILES, dtype=jnp.int32) * MOE_BM
    tile_active = (tile_row0 < ends[-1]).astype(jnp.int32)
    last_row0 = jnp.maximum(ends[-1] - MOE_BM, 0)
    tile_expert = jnp.sum((jnp.minimum(tile_row0, last_row0)[:, None] >= ends[None, :]).astype(jnp.int32), axis=1)
    tile_expert = jnp.minimum(tile_expert, N_EXP - 1)
    row = jnp.arange(MOE_ROWS, dtype=jnp.int32)
    row_e = jnp.repeat(tile_expert, MOE_BM)
    rank = row - starts[row_e]
    valid = (rank < counts[row_e]) & (row < ends[-1])
    pair = jnp.where(valid, order[jnp.clip(first[row_e] + rank, 0, MOE_PAIRS - 1)], -1)
    pair = pair.reshape(MOE_TILES, MOE_BM)
    tile_count = jnp.sum((pair >= 0).astype(jnp.int32), axis=1)
    src_tok = jnp.pad(jnp.maximum(pair, 0) >> 1, ((0, 0), (0, MOE_BUF - MOE_BM)))
    dst_row = jnp.where(pair >= 0, (pair & 1) * N_TOK + (pair >> 1), -1)
    dst_row = jnp.pad(dst_row, ((1, 0), (0, MOE_BUF - MOE_BM)), constant_values=-1)
    return (src_tok.reshape(MOE_TILES, 1, MOE_BUF), dst_row.reshape(MOE_TILES + 1, 1, MOE_BUF),
            tile_expert, tile_active, tile_count)


def _row(ref, r):
    return ref.at[pl.ds(r, 1), :]


def _moe_group_kernel(te_ref, ta_ref, tc_ref, dprv_ref, dcur_ref, gcur_ref, gnxt_ref, h2_ref, wa_ref, wb_ref, wo_ref,
                      out_ref, xbuf, ybuf, h_scr, gsem, ssem):
    del te_ref
    i = pl.program_id(0)
    f = pl.program_id(1)
    nt = pl.num_programs(0)
    s = i % 2
    o = 1 - s
    active = ta_ref[i] == 1
    prv_active = (i >= 1) & (ta_ref[jnp.maximum(i - 1, 0)] == 1)

    def gather_row(idx_ref, k, slot):
        pltpu.make_async_copy(_row(h2_ref, idx_ref[0, k]), _row(xbuf.at[slot], k),
                              gsem.at[slot]).start(priority=ROW_DMA_PRIORITY)

    def scatter_row(idx_ref, k, slot):
        dst = idx_ref[0, k]

        @pl.when(dst >= 0)
        def _():
            pltpu.make_async_copy(_row(ybuf.at[slot], k), _row(out_ref, dst),
                                  ssem.at[slot]).start(priority=ROW_DMA_PRIORITY)

    def wait_rows(sem, n):
        n = jnp.asarray(n, jnp.int32)
        n8 = pl.multiple_of((n >> 3) << 3, 8)

        @pl.when(n8 > 0)
        def _():
            pltpu.make_async_copy(xbuf.at[0, pl.ds(0, n8), :], ybuf.at[0, pl.ds(0, n8), :], sem).wait()

        def one(k, carry):
            pltpu.make_async_copy(_row(xbuf.at[0], 0), _row(ybuf.at[0], 0), sem).wait()
            return carry
        lax.fori_loop(0, n - n8, one, 0)

    def rows_loop(fn, n):
        def body(k, carry):
            fn(k)
            return carry
        lax.fori_loop(0, n, body, 0, unroll=8)

    @pl.when((i == 0) & (f == 0))
    def _():
        rows_loop(lambda k: gather_row(gcur_ref, k, 0), MOE_MOVED)

    @pl.when((f == 0) & (i >= 2))
    def _():
        wait_rows(ssem.at[s], tc_ref[jnp.maximum(i - 2, 0)])

    @pl.when((f == 0) & ((i == 0) | prv_active))
    def _():
        wait_rows(gsem.at[s], MOE_MOVED)

    @pl.when((f == 0) & active)
    def _():
        h_scr[...] = xbuf[s, 0:MOE_BM, :].astype(BF16)
        ybuf[s, 0:MOE_BM, :] = jnp.zeros((MOE_BM, D), F32)

    @pl.when(active)
    def _():
        for u in range(MOE_CH):
            gather_row(gnxt_ref, f * MOE_CH + u, o)
        for u in range(MOE_CH):
            scatter_row(dprv_ref, f * MOE_CH + u, o)
        h = h_scr[...]
        a = jnp.dot(h, wa_ref[...].astype(BF16), preferred_element_type=F32)
        b = jnp.dot(h, wb_ref[...].astype(BF16), preferred_element_type=F32)
        g = (_silu(a) * b).astype(BF16)
        ybuf[s, 0:MOE_BM, :] += jnp.dot(g, wo_ref[...].astype(BF16), preferred_element_type=F32)

    @pl.when(jnp.logical_not(active) & prv_active)
    def _():
        rows_loop(lambda u: scatter_row(dprv_ref, f * MOE_CH + u, o), MOE_CH)

    @pl.when((i == nt - 1) & (f == pl.num_programs(1) - 1))
    def _():
        wait_rows(ssem.at[o], tc_ref[jnp.maximum(i - 1, 0)])
        rows_loop(lambda k: scatter_row(dcur_ref, k, s), MOE_BM)
        wait_rows(ssem.at[s], tc_ref[i])

        @pl.when(active)
        def _():
            wait_rows(gsem.at[o], MOE_MOVED)


def _moe_group(h2, src_tok, dst_row, tile_expert, tile_active, tile_count, w_in, w_out):
    nf = MOE_NF
    last = MOE_TILES - 1

    def fblk(i, f, ta):
        return jnp.where(ta[i] == 1, f, nf - 1)

    idx_spec = lambda off, hi: pl.BlockSpec((None, 1, MOE_BUF),
                                            lambda i, f, te, ta, tc: (jnp.minimum(i + off, hi), 0, 0),
                                            memory_space=pltpu.SMEM)
    grid_spec = pltpu.PrefetchScalarGridSpec(
        num_scalar_prefetch=3,
        grid=(MOE_TILES, nf),
        in_specs=[idx_spec(0, last + 1), idx_spec(1, last + 1),
                  idx_spec(0, last), idx_spec(1, last),
                  pl.BlockSpec(memory_space=pl.ANY),
                  pl.BlockSpec((None, None, D, MOE_TF), lambda i, f, te, ta, tc: (0, te[i], 0, fblk(i, f, ta))),
                  pl.BlockSpec((None, None, D, MOE_TF), lambda i, f, te, ta, tc: (0, te[i], 0, nf + fblk(i, f, ta))),
                  pl.BlockSpec((None, None, MOE_TF, D), lambda i, f, te, ta, tc: (0, te[i], fblk(i, f, ta), 0))],
        out_specs=pl.BlockSpec(memory_space=pl.ANY),
        scratch_shapes=[pltpu.VMEM((2, MOE_BUF, D), F32), pltpu.VMEM((2, MOE_BUF, D), F32),
                        pltpu.VMEM((MOE_BM, D), BF16),
                        pltpu.SemaphoreType.DMA((2,)), pltpu.SemaphoreType.DMA((2,))])
    return pl.pallas_call(
        _moe_group_kernel,
        grid_spec=grid_spec,
        out_shape=jax.ShapeDtypeStruct((MOE_PAIRS, D), F32),
        compiler_params=_cparams(("arbitrary", "arbitrary")),
        name="moe_group",
    )(tile_expert, tile_active, tile_count, dst_row, dst_row, src_tok, src_tok, h2, w_in, w_in, w_out)


MOE_COMBINE_BM = 512


def _moe_combine_kernel(route_ref, x_ref, m_ref, nf_ref, y1_ref, y2_ref, outc_ref, outl_ref):
    route = route_ref[...]
    f = (y1_ref[...] * route[:, ROUTE_W1:ROUTE_W1 + 1]
         + y2_ref[...] * route[:, ROUTE_W2:ROUTE_W2 + 1])
    xn = x_ref[...] + m_ref[5:6, :] * f
    y = xn * lax.rsqrt(jnp.mean(xn * xn, axis=-1, keepdims=True) + EPS) * nf_ref[...]
    is_ctx = pl.program_id(0) < N_CTX // MOE_COMBINE_BM

    @pl.when(is_ctx)
    def _():
        outc_ref[...] = y

    @pl.when(jnp.logical_not(is_ctx))
    def _():
        outl_ref[...] = y


def _moe_combine(ys, route, x, mods, norm_final):
    bm = MOE_COMBINE_BM
    nt = N_TOK // bm
    nct = N_CTX // bm
    return pl.pallas_call(
        _moe_combine_kernel,
        grid=(nt,),
        in_specs=[pl.BlockSpec((bm, 128), lambda i: (i, 0)),
                  pl.BlockSpec((bm, D), lambda i: (i, 0)),
                  pl.BlockSpec((None, 6, D), lambda i: (_group_of_rows(i * bm), 0, 0)),
                  pl.BlockSpec((1, D), lambda i: (0, 0)),
                  pl.BlockSpec((bm, D), lambda i: (i, 0)),
                  pl.BlockSpec((bm, D), lambda i: (nt + i, 0))],
        out_specs=[pl.BlockSpec((bm, D), lambda i: (jnp.minimum(i, nct - 1), 0)),
                   pl.BlockSpec((bm, D), lambda i: (jnp.maximum(i - nct, 0), 0))],
        out_shape=[jax.ShapeDtypeStruct((N_CTX, D), F32), jax.ShapeDtypeStruct((N_LAT, D), F32)],
        compiler_params=_cparams(("arbitrary",)),
        name="moe_combine",
    )(route, x, mods, norm_final.reshape(1, D), ys, ys)


def _moe(h2, route, x, mods, w_in, w_out, norm_final):
    src_tok, dst_row, tile_expert, tile_active, tile_count = _moe_routing(route)
    ys = _moe_group(h2, src_tok, dst_row, tile_expert, tile_active, tile_count, w_in, w_out)
    return _moe_combine(ys, route, x, mods, norm_final)


def _hgrn_lower_bounds(lb_param):
    sm = jax.nn.softmax(lb_param.astype(F32), axis=0)
    return jnp.cumsum(sm, axis=0) - sm[0:1]


def kernel(x_prompt, x_sample, cache_k, cache_v, state_hgrn, c, c_ctx, w_ada, b_ada, norm_mix, norm_ffn,
           w_in, hgrn_lb, hgrn_norm, pool_w, pool_scale, rpb, w_branch_a, w_branch_b, w_branch_c, w_out,
           ffn_w_in, ffn_w_out, router, moe_w_in, moe_w_out, norm_final):
    x = (x_prompt.reshape(N_CTX, D), x_sample.reshape(N_LAT, D))
    cond8 = jnp.concatenate([c_ctx[None], c, jnp.zeros((5, D), F32)], axis=0)
    mods = _adaln(cond8, w_ada, b_ada)[:, :3].reshape(DEPTH, 3, 6, D)
    lbs = _hgrn_lower_bounds(hgrn_lb)
    ck = cache_k.reshape(N_LAT_B, DEPTH, PAST, WIDTH).astype(BF16)
    cv = cache_v.reshape(N_LAT_B, DEPTH, PAST, WIDTH).astype(BF16)
    w_in16 = w_in.astype(BF16)
    router_pad = jnp.pad(router[0], ((0, 0), (0, 128 - N_EXP)))
    r_hi = router_pad.astype(BF16)
    r_lo = (router_pad - r_hi.astype(F32)).astype(BF16)

    kvs, ss = [], []
    for l in range(DEPTH):
        z, zg, kv = _inproj(x, norm_mix[l], mods[l], w_in16, l)
        of_c, ob_c, s_ctx = _hgrn(z, zg, lbs[l], None, 0, N_CTX_B, CTX_T)
        of_l, ob_l, _ = _hgrn(z, zg, lbs[l], state_hgrn[:, l], N_CTX, N_LAT_B, LAT_T)
        o_pool = _pool(z, pool_w[l].astype(BF16), pool_scale[l])
        oc_c = _ctx_attn(z)
        oc_l = _na_attn(z, ck, cv, _na_bias_table(rpb[l]), l)
        merged = _merge((of_c, ob_c, oc_c), (of_l, ob_l, oc_l), z, o_pool, x, mods[l], hgrn_norm[l], norm_ffn[l],
                        w_branch_a[l].astype(BF16), w_branch_b[l].astype(BF16),
                        w_branch_c[l].astype(BF16), w_out[l].astype(BF16),
                        (r_hi, r_lo) if l % 2 == 1 else None)
        if l % 2 == 0:
            x, h2 = merged
            x = _ffn(h2, x, mods[l], ffn_w_in.astype(BF16), ffn_w_out.astype(BF16))
        else:
            x, h2, route = merged
            y_ctx, y_lat = _moe(h2, route, x, mods[l], moe_w_in, moe_w_out, norm_final)
        kvs.append(kv.reshape(N_CTX_B, CTX_T, 2, NA_H, NA_D))
        ss.append(s_ctx)
    kv = jnp.stack(kvs, axis=1)
    y_prompt = y_ctx.reshape(N_CTX_B, CTX_T, D)
    y_sample = y_lat.reshape(N_LAT_B, LAT_T, D)
    return (y_prompt, y_sample, kv[:, :, :, 0], kv[:, :, :, 1], jnp.stack(ss, axis=1))
```

```python
import functools

import numpy as np
import jax
import jax.numpy as jnp
from jax import lax
from jax.experimental import pallas as pl
from jax.experimental.pallas import tpu as pltpu

F32 = jnp.float32
BF16 = jnp.bfloat16

D = 1024
N_CTX_B, CTX_T = 32, 256
N_LAT_B, LAT_T = 2, 4096
N_CTX = N_CTX_B * CTX_T
N_LAT = N_LAT_B * LAT_T
N_TOK = N_CTX + N_LAT
DEPTH = 2
GRID_W = 64
GRID_ROWS = LAT_T // GRID_W
PAST = 512
HG_H, HG_D = 4, 128
WIDTH = 512
NA_H, NA_D = 8, 64
WIN_ROWS, WIN_COLS = 8, 16
POOL_WINDOWS = (2, 4, 8, 16)
IN_COLS = 7680
F_DENSE = 2816
N_EXP = 8
F_EXP = 3584
GATE_CLIP = 30.0
EPS = 1e-6
NEG_BIG = -1e30

COL_AQ, COL_AFF, COL_AFB, COL_AI, COL_AG, COL_BU, COL_CQ, COL_CK, COL_CV = range(9)
GL_COL0 = 9 * WIDTH

ROUTE_I1, ROUTE_I2, ROUTE_W1, ROUTE_W2 = 8, 9, 10, 11

VMEM_LIMIT = 56 * 1024 * 1024


def _cparams(sem):
    return pltpu.CompilerParams(dimension_semantics=sem, vmem_limit_bytes=VMEM_LIMIT)


def _sigmoid(x):
    return 1.0 / (1.0 + jnp.exp(-x))


def _silu(x):
    return x / (1.0 + jnp.exp(-x))


def _group_of_rows(row0):
    return jnp.maximum(row0 - N_CTX + LAT_T, 0) // LAT_T


def _adaln_kernel(c_ref, w_ref, b_ref, o_ref):
    s = _silu(c_ref[...]).astype(BF16)
    o_ref[...] = jnp.dot(s, w_ref[...].astype(BF16), preferred_element_type=F32) + b_ref[...]


def _adaln(cond8, w_ada, b_ada):
    tn = 1536
    return pl.pallas_call(
        _adaln_kernel,
        grid=(DEPTH, 6 * D // tn),
        in_specs=[pl.BlockSpec((8, D), lambda l, j: (0, 0)),
                  pl.BlockSpec((None, D, tn), lambda l, j: (l, 0, j)),
                  pl.BlockSpec((None, 1, tn), lambda l, j: (l, 0, j))],
        out_specs=pl.BlockSpec((None, 8, tn), lambda l, j: (l, 0, j)),
        out_shape=jax.ShapeDtypeStruct((DEPTH, 8, 6 * D), F32),
        compiler_params=_cparams(("parallel", "parallel")),
        name="adaln",
    )(cond8, w_ada, b_ada.reshape(DEPTH, 1, 6 * D))


INPROJ_BM, INPROJ_BN = 1024, 3 * WIDTH


def _stream_rows(x, bm):
    if not isinstance(x, tuple):
        specs = [pl.BlockSpec((bm, D), lambda i, *_: (i, 0))]
        return specs, [x]
    nct = N_CTX // bm
    specs = [pl.BlockSpec((bm, D), lambda i, *_: (jnp.minimum(i, nct - 1), 0)),
             pl.BlockSpec((bm, D), lambda i, *_: (jnp.maximum(i - nct, 0), 0))]
    return specs, list(x)


def _read_stream_rows(x_refs, bm):
    if len(x_refs) == 1:
        return x_refs[0][...]
    return jnp.where(pl.program_id(0) < N_CTX // bm, x_refs[0][...], x_refs[1][...])


def _inproj_kernel(*refs):
    nw_ref, m_ref, w_ref, z_ref, zg_ref, kv_ref, h_ref = refs[-7:]
    x_refs = refs[:-7]
    i = pl.program_id(0)
    j = pl.program_id(1)

    @pl.when(j == 0)
    def _():
        x = _read_stream_rows(x_refs, INPROJ_BM)
        y = x * lax.rsqrt(jnp.mean(x * x, axis=-1, keepdims=True) + EPS) * nw_ref[...]
        h_ref[...] = (y * (1.0 + m_ref[1:2, :]) + m_ref[0:1, :]).astype(BF16)

    acc = jnp.dot(h_ref[...], w_ref[...], preferred_element_type=F32)

    z_ref[...] = acc.astype(BF16)

    @pl.when(j == 0)
    def _():
        zg_ref[...] = acc[:, WIDTH:3 * WIDTH]

    @pl.when((j == 2) & (i < N_CTX // INPROJ_BM))
    def _():
        kv_ref[...] = acc[:, WIDTH:3 * WIDTH]


def _inproj(x, norm_w, mods, w_in16, layer):
    bm, bn = INPROJ_BM, INPROJ_BN
    nct = N_CTX // bm
    x_specs, x_args = _stream_rows(x, bm)
    return pl.pallas_call(
        _inproj_kernel,
        grid=(N_TOK // bm, GL_COL0 // bn),
        in_specs=x_specs + [
            pl.BlockSpec((1, D), lambda i, j: (0, 0)),
            pl.BlockSpec((None, 6, D), lambda i, j: (_group_of_rows(i * bm), 0, 0)),
            pl.BlockSpec((None, D, bn), lambda i, j: (layer, 0, j))],
        out_specs=[pl.BlockSpec((bm, bn), lambda i, j: (i, j)),
                   pl.BlockSpec((bm, 2 * WIDTH), lambda i, j: (i, 0)),
                   pl.BlockSpec((bm, 2 * WIDTH), lambda i, j: (jnp.minimum(i, nct - 1), 0)),
                   pl.BlockSpec((bm, D), lambda i, j: (i, 0))],
        out_shape=[jax.ShapeDtypeStruct((N_TOK, GL_COL0), BF16),
                   jax.ShapeDtypeStruct((N_TOK, 2 * WIDTH), F32),
                   jax.ShapeDtypeStruct((N_CTX, 2 * WIDTH), F32),
                   jax.ShapeDtypeStruct((N_TOK, D), BF16)],
        compiler_params=_cparams(("arbitrary", "arbitrary")),
        name="inproj",
    )(*x_args, norm_w.reshape(1, D), mods, w_in16)


HG_C = 128
HG_LEVELS = (4, 8, 16, 32, 64, 128)
LOG2_E = 1.4426950408889634
EXP2_CAP = 120.0


def _hgrn_level_ids(reverse):
    t = np.arange(HG_C)[:, None]
    s = np.arange(HG_C)[None, :]
    if reverse:
        t, s = s, t
    lev = np.full((HG_C, HG_C), -1, np.int32)
    lev[(t // 4 == s // 4) & (s <= t)] = 0
    for li, L in enumerate(HG_LEVELS[1:], start=1):
        m = (t // L == s // L) & (t % L >= L // 2) & (s % L < L // 2)
        lev[m] = li
    return lev


def _hgrn_ref_rows(b_scr, d, reverse):
    out = []
    r_lo, r_hi = (2, 6) if reverse else (1, 5)
    sub = lax.broadcasted_iota(jnp.int32, (8, WIDTH), 0)
    pieces = []
    for g in range(HG_C // 8):
        lo = jnp.broadcast_to(b_scr[d, 8 * g + r_lo:8 * g + r_lo + 1, :], (8, WIDTH))
        hi = jnp.broadcast_to(b_scr[d, 8 * g + r_hi:8 * g + r_hi + 1, :], (8, WIDTH))
        pieces.append(jnp.where(sub < 4, lo, hi))
    out.append(jnp.concatenate(pieces, axis=0))
    for L in HG_LEVELS[1:]:
        r = L // 2 - 1 if reverse else L // 2
        pieces = [jnp.broadcast_to(b_scr[d, L * g + r:L * g + r + 1, :], (L, WIDTH))
                  for g in range(HG_C // L)]
        out.append(pieces[0] if len(pieces) == 1 else jnp.concatenate(pieces, axis=0))
    return out


def _hgrn_kernel(*refs, has_init):
    if has_init:
        (qf_ref, ff_ref, vf_ref, qb_ref, fb_ref, vb_ref, lb_ref, levf_ref, levb_ref, s0_ref,
         of_ref, ob_ref, so_ref, st_scr, b_scr) = refs
    else:
        (qf_ref, ff_ref, vf_ref, qb_ref, fb_ref, vb_ref, lb_ref, levf_ref, levb_ref,
         of_ref, ob_ref, so_ref, st_scr, b_scr) = refs
    c = pl.program_id(1)
    C = HG_C

    @pl.when(c == 0)
    def _():
        for d in range(2):
            for h in range(HG_H):
                if has_init:
                    st_scr[d, h] = s0_ref[d, h].T
                else:
                    st_scr[d, h] = jnp.zeros((HG_D, HG_D), F32)

    row = lax.broadcasted_iota(jnp.int32, (C, C), 0)
    col = lax.broadcasted_iota(jnp.int32, (C, C), 1)
    dirs = ((qf_ref, ff_ref, vf_ref, levf_ref, of_ref), (qb_ref, fb_ref, vb_ref, levb_ref, ob_ref))
    for d, (q_ref, f_ref, v_ref, lev_ref, o_ref) in enumerate(dirs):
        reverse = d == 1
        tri = jnp.where((col >= row) if reverse else (col <= row), 1.0, 0.0).astype(BF16)
        q = _silu(q_ref[...].astype(F32))
        fx = jnp.clip(f_ref[...], -GATE_CLIP, GATE_CLIP)
        e = jnp.exp(-fx)
        sig_pos = 1.0 / (1.0 + e)
        sig_neg = e * sig_pos
        lb = lb_ref[d:d + 1, :]
        lf = jnp.log(lb + (1.0 - lb) * sig_pos) * LOG2_E
        k = (1.0 - lb) * sig_neg
        q16 = q.astype(BF16)
        k16 = k.astype(BF16)
        hi = lf.astype(BF16)
        lo = (lf - hi.astype(F32)).astype(BF16)
        b = (jnp.dot(tri, hi, preferred_element_type=F32)
             + jnp.dot(tri, lo, preferred_element_type=F32))
        b_scr[d] = b
        refs_m = _hgrn_ref_rows(b_scr, d, reverse)
        qs, ks = [], []
        for li, m in enumerate(refs_m):
            dlt = b - m
            if li == 0:
                qs.append(q16 * jnp.exp2(jnp.minimum(dlt, EXP2_CAP)).astype(BF16))
                ks.append(k16 * jnp.exp2(jnp.minimum(-dlt, EXP2_CAP)).astype(BF16))
            else:
                fac = jnp.exp2(-jnp.abs(dlt)).astype(BF16)
                qs.append(q16 * fac)
                ks.append(k16 * fac)
        lev = lev_ref[...]
        b_end = b[0:1, :] if reverse else b[C - 1:C, :]
        q_in = q16 * jnp.exp2(b).astype(BF16)
        k_out = k16 * jnp.exp2(b_end - b).astype(BF16)
        dec = jnp.exp2(b_end)
        vb16 = v_ref[...]
        for h in range(HG_H):
            sl = slice(h * HG_D, (h + 1) * HG_D)
            a = jnp.zeros((C, C), F32)
            for li in range(len(HG_LEVELS)):
                p = lax.dot_general(qs[li][:, sl], ks[li][:, sl], (((1,), (1,)), ((), ())),
                                    preferred_element_type=F32)
                a = jnp.where(lev == li, p, a)
            vh = vb16[:, sl]
            st = st_scr[d, h]
            o = (jnp.dot(a.astype(BF16), vh, preferred_element_type=F32)
                 + lax.dot_general(q_in[:, sl], st.astype(BF16), (((1,), (1,)), ((), ())),
                                   preferred_element_type=F32))
            o_ref[:, sl] = o
            upd = lax.dot_general(vh, k_out[:, sl], (((0,), (0,)), ((), ())),
                                  preferred_element_type=F32)
            st_scr[d, h] = st * dec[:, sl] + upd

    @pl.when(c == pl.num_programs(1) - 1)
    def _():
        for d in range(2):
            for h in range(HG_H):
                so_ref[d, h] = st_scr[d, h].T


def _hgrn(z, zg, lb, s0, row_off, n_b, seq_t):
    C = HG_C
    nc = seq_t // C
    base = row_off // C
    has_init = s0 is not None

    def fwd(col):
        return pl.BlockSpec((C, WIDTH), lambda b, c: (base + b * nc + c, col))

    def bwd(col):
        return pl.BlockSpec((C, WIDTH), lambda b, c: (base + b * nc + nc - 1 - c, col))

    full = lambda shape: pl.BlockSpec(shape, lambda b, c: (0,) * len(shape))
    in_specs = [fwd(COL_AQ), fwd(0), fwd(COL_AI), bwd(COL_AQ), bwd(1), bwd(COL_AI),
                full((2, WIDTH)), full((C, C)), full((C, C))]
    args = [z, zg, z, z, zg, z, lb, jnp.asarray(_hgrn_level_ids(False)), jnp.asarray(_hgrn_level_ids(True))]
    if has_init:
        in_specs.append(pl.BlockSpec((None, 2, HG_H, HG_D, HG_D), lambda b, c: (b, 0, 0, 0, 0)))
        args.append(s0)
    return pl.pallas_call(
        functools.partial(_hgrn_kernel, has_init=has_init),
        grid=(n_b, nc),
        in_specs=in_specs,
        out_specs=[pl.BlockSpec((C, WIDTH), lambda b, c: (b * nc + c, 0)),
                   pl.BlockSpec((C, WIDTH), lambda b, c: (b * nc + nc - 1 - c, 0)),
                   pl.BlockSpec((None, 2, HG_H, HG_D, HG_D), lambda b, c: (b, 0, 0, 0, 0))],
        out_shape=[jax.ShapeDtypeStruct((n_b * seq_t, WIDTH), F32),
                   jax.ShapeDtypeStruct((n_b * seq_t, WIDTH), F32),
                   jax.ShapeDtypeStruct((n_b, 2, HG_H, HG_D, HG_D), F32)],
        scratch_shapes=[pltpu.VMEM((2, HG_H, HG_D, HG_D), F32),
                        pltpu.VMEM((2, C, WIDTH), F32)],
        compiler_params=_cparams(("parallel", "arbitrary")),
        name="hgrn_lat" if has_init else "hgrn_ctx",
    )(*args)


POOL_T = 256
POOL_HALO = 16


def _pool_kernel(u_ref, up_ref, un_ref, w_ref, sc_ref, o_ref):
    i = pl.program_id(0)
    is_lat = i >= N_CTX // POOL_T
    seq_t = jnp.where(is_lat, LAT_T, CTX_T)
    t0 = jnp.where(is_lat, ((i - N_CTX // POOL_T) % (LAT_T // POOL_T)) * POOL_T, 0)
    TT, HL = POOL_T, POOL_HALO
    diff = lax.broadcasted_iota(jnp.int32, (TT, TT), 1) - lax.broadcasted_iota(jnp.int32, (TT, TT), 0)
    diff_h = lax.broadcasted_iota(jnp.int32, (TT, HL), 1) - lax.broadcasted_iota(jnp.int32, (TT, HL), 0)
    t_glob = t0 + lax.broadcasted_iota(jnp.int32, (TT, 128), 0)
    has_prev = t0 > 0
    has_next = t0 + TT < seq_t
    for g, w in enumerate(POOL_WINDOWS):
        half = w // 2
        sl = slice(g * 128, (g + 1) * 128)
        band = jnp.where((diff >= -half) & (diff < half), 1.0, 0.0).astype(BF16)
        dp = diff_h - HL
        band_p = jnp.where((dp >= -half) & (dp < half) & has_prev, 1.0, 0.0).astype(BF16)
        dn = diff_h + TT
        band_n = jnp.where((dn >= -half) & (dn < half) & has_next, 1.0, 0.0).astype(BF16)
        u = u_ref[:, sl]
        s = (jnp.dot(band, u, preferred_element_type=F32)
             + jnp.dot(band_p, up_ref[:, sl], preferred_element_type=F32)
             + jnp.dot(band_n, un_ref[:, sl], preferred_element_type=F32))
        cnt = (jnp.minimum(t_glob + half, seq_t) - jnp.maximum(t_glob - half, 0)).astype(F32)
        dd = s / cnt - u.astype(F32)
        y = jnp.dot(dd.astype(BF16), w_ref[g], preferred_element_type=F32)
        o_ref[:, sl] = y * sc_ref[:, sl]


def _pool(z, pool_w16, pool_scale):
    nt = N_TOK // POOL_T
    per = POOL_T // POOL_HALO
    return pl.pallas_call(
        _pool_kernel,
        grid=(nt,),
        in_specs=[pl.BlockSpec((POOL_T, WIDTH), lambda i: (i, COL_BU)),
                  pl.BlockSpec((POOL_HALO, WIDTH), lambda i: (jnp.maximum(i * per - 1, 0), COL_BU)),
                  pl.BlockSpec((POOL_HALO, WIDTH), lambda i: (jnp.minimum((i + 1) * per, nt * per - 1), COL_BU)),
                  pl.BlockSpec((4, 128, 128), lambda i: (0, 0, 0)),
                  pl.BlockSpec((1, WIDTH), lambda i: (0, 0))],
        out_specs=pl.BlockSpec((POOL_T, WIDTH), lambda i: (i, 0)),
        out_shape=jax.ShapeDtypeStruct((N_TOK, WIDTH), F32),
        compiler_params=_cparams(("parallel",)),
        name="pool",
    )(z, z, z, pool_w16, pool_scale.reshape(1, WIDTH))


def _head_mask(hh):
    lane = lax.broadcasted_iota(jnp.int32, (1, 128), 1)
    in_head = (lane >= hh * NA_D) & (lane < (hh + 1) * NA_D)
    return jnp.where(in_head, NA_D ** -0.5, 0.0).astype(BF16)


def _ctx_attn_kernel(q_ref, k_ref, v_ref, o_ref):
    lane = lax.broadcasted_iota(jnp.int32, (CTX_T, 128), 1)
    for j in range(NA_H // 2):
        sl = slice(j * 128, (j + 1) * 128)
        q = q_ref[:, sl]
        kt = k_ref[:, sl]
        vt = v_ref[:, sl]
        outs = []
        for hh in range(2):
            s = lax.dot_general(q * _head_mask(hh), kt, (((1,), (1,)), ((), ())), preferred_element_type=F32)
            m = jnp.max(s, axis=-1, keepdims=True)
            p = jnp.exp(s - m)
            l = jnp.sum(p, axis=-1, keepdims=True)
            outs.append(jnp.dot(p.astype(BF16), vt, preferred_element_type=F32) / l)
        o_ref[:, sl] = jnp.where(lane < NA_D, outs[0], outs[1])


def _ctx_attn(z):
    spec = lambda col: pl.BlockSpec((CTX_T, WIDTH), lambda b: (b, col))
    return pl.pallas_call(
        _ctx_attn_kernel,
        grid=(N_CTX_B,),
        in_specs=[spec(COL_CQ), spec(COL_CK), spec(COL_CV)],
        out_specs=pl.BlockSpec((CTX_T, WIDTH), lambda b: (b, 0)),
        out_shape=jax.ShapeDtypeStruct((N_CTX, WIDTH), F32),
        compiler_params=_cparams(("parallel",)),
        name="ctx_attn",
    )(z, z, z)


NA_QR = 4
NA_KR = 12
NA_NQ = NA_QR * GRID_W
NA_NK = NA_KR * GRID_W
NA_BLOCKS = GRID_ROWS // NA_QR


def _na_key_row0(blk):
    return jnp.clip(blk * NA_QR - WIN_ROWS // 2, 0, GRID_ROWS - NA_KR)


def _na_geometry():
    patterns, var_of_block = [], []
    for blk in range(NA_BLOCKS):
        r = blk * NA_QR + np.arange(NA_QR)[:, None]
        kr = int(np.clip(blk * NA_QR - WIN_ROWS // 2, 0, GRID_ROWS - NA_KR)) + np.arange(NA_KR)[None, :]
        rs = np.clip(r - WIN_ROWS // 2, 0, GRID_ROWS - WIN_ROWS)
        valid = (kr >= rs) & (kr < rs + WIN_ROWS)
        assert (valid.sum(axis=1) == WIN_ROWS).all(), "key rows must cover every query row's window"
        drow = np.where(valid, kr - r + WIN_ROWS - 1, 0)
        key = (drow.tobytes(), valid.tobytes())
        ids = [i for i, (k_, _, _) in enumerate(patterns) if k_ == key]
        if not ids:
            patterns.append((key, drow, valid))
            ids = [len(patterns) - 1]
        var_of_block.append(ids[0])
    drow = np.stack([p[1] for p in patterns])
    valid = np.stack([p[2] for p in patterns])
    return np.asarray(var_of_block, np.int32), drow, valid


NA_DROWS = 2 * WIN_ROWS - 1


def _na_bias_kernel(didx_ref, t2_ref, o_ref):
    v = pl.program_id(0)
    for qr in range(NA_QR):
        for kp in range(NA_KR // 2):
            base = (v * NA_QR + qr) * NA_KR + 2 * kp
            pair = jnp.concatenate([t2_ref[didx_ref[base]], t2_ref[didx_ref[base + 1]]], axis=1)
            o_ref[qr * GRID_W:(qr + 1) * GRID_W, kp * 128:(kp + 1) * 128] = pair


def _na_bias_table(rpb_l):
    _, drow, valid = _na_geometry()
    n_var = valid.shape[0]
    qc = np.arange(GRID_W)[:, None]
    kc = np.arange(GRID_W)[None, :]
    q0 = np.clip(qc - WIN_COLS // 2, 0, GRID_W - WIN_COLS)
    col_in = (kc >= q0) & (kc < q0 + WIN_COLS)
    dcol = np.clip(kc - qc, -(WIN_COLS - 1), WIN_COLS - 1) + WIN_COLS - 1
    oh_col = (dcol[None] == np.arange(2 * WIN_COLS - 1)[:, None, None]).astype(np.float32)
    t2 = jnp.einsum('hdc,cqk->hdqk', rpb_l, jnp.asarray(oh_col), precision=lax.Precision.HIGHEST)
    t2 = jnp.where(col_in[None, None], t2, NEG_BIG)
    t2 = jnp.concatenate([t2, jnp.full((NA_H, 1, GRID_W, GRID_W), NEG_BIG, F32)], axis=1)
    didx = np.where(valid, drow, NA_DROWS).astype(np.int32).reshape(-1)
    grid_spec = pltpu.PrefetchScalarGridSpec(
        num_scalar_prefetch=1,
        grid=(n_var, NA_H),
        in_specs=[pl.BlockSpec((None, NA_DROWS + 1, GRID_W, GRID_W), lambda v, h, didx: (h, 0, 0, 0))],
        out_specs=pl.BlockSpec((None, None, NA_NQ, NA_NK), lambda v, h, didx: (v, h, 0, 0)))
    return pl.pallas_call(
        _na_bias_kernel,
        grid_spec=grid_spec,
        out_shape=jax.ShapeDtypeStruct((n_var, NA_H, NA_NQ, NA_NK), F32),
        compiler_params=_cparams(("parallel", "parallel")),
        name="na_bias",
    )(jnp.asarray(didx), t2)


def _na_attn_kernel(var_ref, q_ref, k_ref, v_ref, kc_ref, vc_ref, tbl_ref, o_ref):
    del var_ref
    blk = pl.program_id(1)
    k0 = pl.multiple_of(_na_key_row0(blk) * GRID_W, GRID_W)
    lane = lax.broadcasted_iota(jnp.int32, (NA_NQ, 128), 1)
    for j in range(NA_H // 2):
        sl = slice(j * 128, (j + 1) * 128)
        q = q_ref[:, sl]
        kt = k_ref[pl.ds(k0, NA_NK), sl]
        vt = v_ref[pl.ds(k0, NA_NK), sl]
        kct = kc_ref[:, sl]
        vct = vc_ref[:, sl]
        outs = []
        for hh in range(2):
            qm = q * _head_mask(hh)
            s_loc = lax.dot_general(qm, kt, (((1,), (1,)), ((), ())), preferred_element_type=F32)
            s_ctx = lax.dot_general(qm, kct, (((1,), (1,)), ((), ())), preferred_element_type=F32)
            tb = tbl_ref[2 * j + hh]
            s_loc = jnp.where(tb > 0.5 * NEG_BIG, s_loc + tb, NEG_BIG)
            m = jnp.maximum(jnp.max(s_loc, axis=-1, keepdims=True), jnp.max(s_ctx, axis=-1, keepdims=True))
            p_loc = jnp.exp(s_loc - m)
            p_ctx = jnp.exp(s_ctx - m)
            l = jnp.sum(p_loc, axis=-1, keepdims=True) + jnp.sum(p_ctx, axis=-1, keepdims=True)
            o = (jnp.dot(p_loc.astype(BF16), vt, preferred_element_type=F32)
                 + jnp.dot(p_ctx.astype(BF16), vct, preferred_element_type=F32))
            outs.append(o / l)
        o_ref[:, sl] = jnp.where(lane < NA_D, outs[0], outs[1])


def _na_attn(z, cache_k16, cache_v16, tbl, layer):
    base_q = N_CTX // NA_NQ
    base_t = N_CTX // LAT_T
    ctx_spec = pl.BlockSpec((None, None, PAST, WIDTH), lambda b, r, var: (b, layer, 0, 0))
    grid_spec = pltpu.PrefetchScalarGridSpec(
        num_scalar_prefetch=1,
        grid=(N_LAT_B, NA_BLOCKS),
        in_specs=[pl.BlockSpec((NA_NQ, WIDTH), lambda b, r, var: (base_q + b * NA_BLOCKS + r, COL_CQ)),
                  pl.BlockSpec((LAT_T, WIDTH), lambda b, r, var: (base_t + b, COL_CK)),
                  pl.BlockSpec((LAT_T, WIDTH), lambda b, r, var: (base_t + b, COL_CV)),
                  ctx_spec, ctx_spec,
                  pl.BlockSpec((None, NA_H, NA_NQ, NA_NK), lambda b, r, var: (var[r], 0, 0, 0))],
        out_specs=pl.BlockSpec((NA_NQ, WIDTH), lambda b, r, var: (b * NA_BLOCKS + r, 0)))
    return pl.pallas_call(
        _na_attn_kernel,
        grid_spec=grid_spec,
        out_shape=jax.ShapeDtypeStruct((N_LAT, WIDTH), F32),
        compiler_params=_cparams(("parallel", "arbitrary")),
        name="na_attn",
    )(jnp.asarray(_na_geometry()[0]), z, z, z, cache_k16, cache_v16, tbl)


MERGE_BM = 512
MERGE_SLABS = 2


def _merge_kernel(*refs, with_router, n_x):
    x_refs, refs = refs[:n_x], refs[n_x:]
    (ofc_ref, obc_ref, occ_ref, ofl_ref, obl_ref, ocl_ref, ag_ref, op_ref, h16_ref, wg0_ref, wg1_ref, m_ref,
     hn_ref, nf_ref, wa_ref, wb_ref, wc_ref, wo_ref) = refs[:18]
    if with_router:
        rhi_ref, rlo_ref, xn_ref, h2_ref, route_ref = refs[18:]
    else:
        xn_ref, h2_ref = refs[18:]
    is_ctx = pl.program_id(0) < N_CTX // MERGE_BM
    for sl in range(MERGE_SLABS):
        rs = slice(sl * MERGE_BM // MERGE_SLABS, (sl + 1) * MERGE_BM // MERGE_SLABS)
        o = jnp.where(is_ctx, ofc_ref[rs, :] + obc_ref[rs, :], ofl_ref[rs, :] + obl_ref[rs, :])
        oc = jnp.where(is_ctx, occ_ref[rs, :], ocl_ref[rs, :])
        parts = []
        for h in range(HG_H):
            oh = o[:, h * HG_D:(h + 1) * HG_D]
            parts.append(oh * lax.rsqrt(jnp.mean(oh * oh, axis=-1, keepdims=True) + EPS))
        oa = jnp.concatenate(parts, axis=1) * hn_ref[...] * _silu(ag_ref[rs, :].astype(F32))
        h16 = h16_ref[rs, :]
        sg0 = _sigmoid(jnp.dot(h16, wg0_ref[...], preferred_element_type=F32).astype(BF16)).astype(F32)
        sg1 = _sigmoid(jnp.dot(h16, wg1_ref[...], preferred_element_type=F32).astype(BF16)).astype(F32)
        sga = sg0[:, :D]
        sgb = jnp.concatenate([sg0[:, D:], sg1[:, :WIDTH]], axis=1)
        sgc = sg1[:, WIDTH:]
        mix = (sga * jnp.dot(oa.astype(BF16), wa_ref[...], preferred_element_type=F32)
               + sgb * jnp.dot(op_ref[rs, :].astype(BF16), wb_ref[...], preferred_element_type=F32)
               + sgc * jnp.dot(oc.astype(BF16), wc_ref[...], preferred_element_type=F32))
        x = x_refs[0][rs, :] if len(x_refs) == 1 else jnp.where(is_ctx, x_refs[0][rs, :], x_refs[1][rs, :])
        xn = x + m_ref[2:3, :] * jnp.dot(mix.astype(BF16), wo_ref[...], preferred_element_type=F32)
        xn_ref[rs, :] = xn
        y = xn * lax.rsqrt(jnp.mean(xn * xn, axis=-1, keepdims=True) + EPS) * nf_ref[...]
        h2 = y * (1.0 + m_ref[4:5, :]) + m_ref[3:4, :]
        h2_ref[rs, :] = h2.astype(h2_ref.dtype)
        if not with_router:
            continue
        hhi = h2.astype(BF16)
        hlo = (h2 - hhi.astype(F32)).astype(BF16)
        logits = (jnp.dot(hhi, rhi_ref[...], preferred_element_type=F32)
                  + jnp.dot(hhi, rlo_ref[...], preferred_element_type=F32)
                  + jnp.dot(hlo, rhi_ref[...], preferred_element_type=F32))
        lane = lax.broadcasted_iota(jnp.int32, logits.shape, 1).astype(F32)
        lg = jnp.where(lane < N_EXP, logits, -jnp.inf)
        m1 = jnp.max(lg, axis=-1, keepdims=True)
        i1 = jnp.min(jnp.where(lg == m1, lane, 128.0), axis=-1, keepdims=True)
        lg2 = jnp.where(lane == i1, -jnp.inf, lg)
        m2 = jnp.max(lg2, axis=-1, keepdims=True)
        i2 = jnp.min(jnp.where(lg2 == m2, lane, 128.0), axis=-1, keepdims=True)
        e = jnp.exp(m2 - m1)
        w1 = 1.0 / (1.0 + e)
        route_ref[rs, :] = (jnp.where(lane == ROUTE_I1, i1, 0.0) + jnp.where(lane == ROUTE_I2, i2, 0.0)
                            + jnp.where(lane == ROUTE_W1, w1, 0.0) + jnp.where(lane == ROUTE_W2, e * w1, 0.0))


def _merge(ctx_parts, lat_parts, z, h16, w_in16, layer, o_pool, x, mods, hgrn_norm_l, norm_ffn_l, wa, wb, wc, wo,
           router_split):
    bm = MERGE_BM
    glw = 1536
    nct = N_CTX // bm
    with_router = router_split is not None
    row = lambda w, col=0: pl.BlockSpec((bm, w), lambda i: (i, col))
    ctx_row = pl.BlockSpec((bm, WIDTH), lambda i: (jnp.minimum(i, nct - 1), 0))
    lat_row = pl.BlockSpec((bm, WIDTH), lambda i: (jnp.maximum(i - nct, 0), 0))
    const = lambda shape: pl.BlockSpec(shape, lambda i: (0,) * len(shape))
    x_specs, x_args = _stream_rows(x, bm)
    gate_w = lambda t: pl.BlockSpec((None, D, glw), lambda i: (layer, 0, GL_COL0 // glw + t))
    in_specs = x_specs + [ctx_row] * 3 + [lat_row] * 3 + [
        row(WIDTH, COL_AG), row(WIDTH),
        row(D), gate_w(0), gate_w(1),
        pl.BlockSpec((None, 6, D), lambda i: (_group_of_rows(i * bm), 0, 0)),
        const((1, WIDTH)), const((1, D)),
        const((WIDTH, D)), const((WIDTH, D)), const((WIDTH, D)), const((D, D))]
    args = x_args + list(ctx_parts) + list(lat_parts) + [
        z, o_pool, h16, w_in16, w_in16, mods, jnp.tile(hgrn_norm_l, HG_H).reshape(1, WIDTH),
        norm_ffn_l.reshape(1, D), wa, wb, wc, wo]
    out_specs = [row(D), row(D)]
    out_shape = [jax.ShapeDtypeStruct((N_TOK, D), F32),
                 jax.ShapeDtypeStruct((N_TOK, D), F32 if with_router else BF16)]
    if with_router:
        in_specs += [const((D, 128)), const((D, 128))]
        args += list(router_split)
        out_specs.append(row(128))
        out_shape.append(jax.ShapeDtypeStruct((N_TOK, 128), F32))
    return pl.pallas_call(
        functools.partial(_merge_kernel, with_router=with_router, n_x=len(x_args)),
        grid=(N_TOK // bm,),
        in_specs=in_specs,
        out_specs=out_specs,
        out_shape=out_shape,
        compiler_params=_cparams(("parallel",)),
        name="merge_route" if with_router else "merge",
    )(*args)


def _ffn_kernel(h_ref, x_ref, m_ref, wa_ref, wb_ref, wo_ref, o_ref):
    f = pl.program_id(1)
    h = h_ref[...]
    a = jnp.dot(h, wa_ref[...], preferred_element_type=F32)
    b = jnp.dot(h, wb_ref[...], preferred_element_type=F32)
    g = (_silu(a) * b).astype(BF16)
    part = jnp.dot(g, wo_ref[...], preferred_element_type=F32)

    @pl.when(f == 0)
    def _():
        o_ref[...] = part

    @pl.when(f > 0)
    def _():
        o_ref[...] += part

    @pl.when(f == pl.num_programs(1) - 1)
    def _():
        o_ref[...] = x_ref[...] + m_ref[5:6, :] * o_ref[...]


FFN_BM = 512
FFN_TF = F_DENSE // 2


def _ffn(h2, x, mods, w_in, w_out):
    bm, tf = FFN_BM, FFN_TF
    nf = F_DENSE // tf
    return pl.pallas_call(
        _ffn_kernel,
        grid=(N_TOK // bm, nf),
        in_specs=[pl.BlockSpec((bm, D), lambda i, f: (i, 0)),
                  pl.BlockSpec((bm, D), lambda i, f: (i, 0)),
                  pl.BlockSpec((None, 6, D), lambda i, f: (_group_of_rows(i * bm), 0, 0)),
                  pl.BlockSpec((None, D, tf), lambda i, f: (0, 0, f)),
                  pl.BlockSpec((None, D, tf), lambda i, f: (0, 0, nf + f)),
                  pl.BlockSpec((None, tf, D), lambda i, f: (0, f, 0))],
        out_specs=pl.BlockSpec((bm, D), lambda i, f: (i, 0)),
        out_shape=jax.ShapeDtypeStruct((N_TOK, D), F32),
        compiler_params=_cparams(("parallel", "arbitrary")),
        name="ffn",
    )(h2, x, mods, w_in, w_in, w_out)


MOE_BM = 1024
MOE_TF = 512
MOE_NF = F_EXP // MOE_TF
MOE_PAIRS = 2 * N_TOK
MOE_ROWS = MOE_PAIRS + N_EXP * MOE_BM
MOE_TILES = MOE_ROWS // MOE_BM
MOE_CH = -(-MOE_BM // MOE_NF)
MOE_MOVED = MOE_CH * MOE_NF
MOE_BUF = -(-MOE_MOVED // 8) * 8
ROW_DMA_PRIORITY = 1


def _moe_routing(route):
    i1 = route[:, ROUTE_I1].astype(jnp.int32)
    i2 = route[:, ROUTE_I2].astype(jnp.int32)
    ep = jnp.stack([i1, i2], axis=1).reshape(-1)
    onehot = (ep[:, None] == jnp.arange(N_EXP, dtype=jnp.int32)[None, :]).astype(jnp.int32)
    counts = jnp.sum(onehot, axis=0)
    padded = ((counts + MOE_BM - 1) // MOE_BM) * MOE_BM
    ends = jnp.cumsum(padded)
    starts = ends - padded
    first = jnp.cumsum(counts) - counts
    order = jnp.sort(ep * MOE_PAIRS + jnp.arange(MOE_PAIRS, dtype=jnp.int32)) & (MOE_PAIRS - 1)
    tile_row0 = jnp.arange(MOE_TILES, dtype=jnp.int32) * MOE_BM
    tile_active = (tile_row0 < ends[-1]).astype(jnp.int32)
    last_row0 = jnp.maximum(ends[-1] - MOE_BM, 0)
    tile_expert = jnp.sum((jnp.minimum(tile_row0, last_row0)[:, None] >= ends[None, :]).astype(jnp.int32), axis=1)
    tile_expert = jnp.minimum(tile_expert, N_EXP - 1)
    row = jnp.arange(MOE_ROWS, dtype=jnp.int32)
    row_e = jnp.repeat(tile_expert, MOE_BM)
    rank = row - starts[row_e]
    valid = (rank < counts[row_e]) & (row < ends[-1])
    pair = jnp.where(valid, order[jnp.clip(first[row_e] + rank, 0, MOE_PAIRS - 1)], -1)
    pair = pair.reshape(MOE_TILES, MOE_BM)
    tile_count = jnp.sum((pair >= 0).astype(jnp.int32), axis=1)
    src_tok = jnp.pad(jnp.maximum(pair, 0) >> 1, ((0, 0), (0, MOE_BUF - MOE_BM)))
    dst_row = jnp.where(pair >= 0, (pair & 1) * N_TOK + (pair >> 1), -1)
    dst_row = jnp.pad(dst_row, ((1, 0), (0, MOE_BUF - MOE_BM)), constant_values=-1)
    return (src_tok.reshape(MOE_TILES, 1, MOE_BUF), dst_row.reshape(MOE_TILES + 1, 1, MOE_BUF),
            tile_expert, tile_active, tile_count)


def _row(ref, r):
    return ref.at[pl.ds(r, 1), :]


def _moe_group_kernel(te_ref, ta_ref, tc_ref, dprv_ref, dcur_ref, gcur_ref, gnxt_ref, h2_ref, wa_ref, wb_ref, wo_ref,
                      out_ref, xbuf, ybuf, h_scr, gsem, ssem):
    del te_ref
    i = pl.program_id(0)
    f = pl.program_id(1)
    nt = pl.num_programs(0)
    s = i % 2
    o = 1 - s
    active = ta_ref[i] == 1
    prv_active = (i >= 1) & (ta_ref[jnp.maximum(i - 1, 0)] == 1)

    def gather_row(idx_ref, k, slot):
        pltpu.make_async_copy(_row(h2_ref, idx_ref[0, k]), _row(xbuf.at[slot], k),
                              gsem.at[slot]).start(priority=ROW_DMA_PRIORITY)

    def scatter_row(idx_ref, k, slot):
        dst = idx_ref[0, k]

        @pl.when(dst >= 0)
        def _():
            pltpu.make_async_copy(_row(ybuf.at[slot], k), _row(out_ref, dst),
                                  ssem.at[slot]).start(priority=ROW_DMA_PRIORITY)

    def wait_rows(sem, n):
        n = jnp.asarray(n, jnp.int32)
        n8 = pl.multiple_of((n >> 3) << 3, 8)

        @pl.when(n8 > 0)
        def _():
            pltpu.make_async_copy(xbuf.at[0, pl.ds(0, n8), :], ybuf.at[0, pl.ds(0, n8), :], sem).wait()

        def one(k, carry):
            pltpu.make_async_copy(_row(xbuf.at[0], 0), _row(ybuf.at[0], 0), sem).wait()
            return carry
        lax.fori_loop(0, n - n8, one, 0)

    def rows_loop(fn, n):
        def body(k, carry):
            fn(k)
            return carry
        lax.fori_loop(0, n, body, 0, unroll=8)

    @pl.when((i == 0) & (f == 0))
    def _():
        rows_loop(lambda k: gather_row(gcur_ref, k, 0), MOE_MOVED)

    @pl.when((f == 0) & (i >= 2))
    def _():
        wait_rows(ssem.at[s], tc_ref[jnp.maximum(i - 2, 0)])

    @pl.when((f == 0) & ((i == 0) | prv_active))
    def _():
        wait_rows(gsem.at[s], MOE_MOVED)

    @pl.when((f == 0) & active)
    def _():
        h_scr[...] = xbuf[s, 0:MOE_BM, :].astype(BF16)
        ybuf[s, 0:MOE_BM, :] = jnp.zeros((MOE_BM, D), F32)

    @pl.when(active)
    def _():
        for u in range(MOE_CH):
            gather_row(gnxt_ref, f * MOE_CH + u, o)
        for u in range(MOE_CH):
            scatter_row(dprv_ref, f * MOE_CH + u, o)
        h = h_scr[...]
        a = jnp.dot(h, wa_ref[...].astype(BF16), preferred_element_type=F32)
        b = jnp.dot(h, wb_ref[...].astype(BF16), preferred_element_type=F32)
        g = (_silu(a) * b).astype(BF16)
        ybuf[s, 0:MOE_BM, :] += jnp.dot(g, wo_ref[...].astype(BF16), preferred_element_type=F32)

    @pl.when(jnp.logical_not(active) & prv_active)
    def _():
        rows_loop(lambda u: scatter_row(dprv_ref, f * MOE_CH + u, o), MOE_CH)

    @pl.when((i == nt - 1) & (f == pl.num_programs(1) - 1))
    def _():
        wait_rows(ssem.at[o], tc_ref[jnp.maximum(i - 1, 0)])
        rows_loop(lambda k: scatter_row(dcur_ref, k, s), MOE_BM)
        wait_rows(ssem.at[s], tc_ref[i])

        @pl.when(active)
        def _():
            wait_rows(gsem.at[o], MOE_MOVED)


def _moe_group(h2, src_tok, dst_row, tile_expert, tile_active, tile_count, w_in, w_out):
    nf = MOE_NF
    last = MOE_TILES - 1

    def fblk(i, f, ta):
        return jnp.where(ta[i] == 1, f, nf - 1)

    idx_spec = lambda off, hi: pl.BlockSpec((None, 1, MOE_BUF),
                                            lambda i, f, te, ta, tc: (jnp.minimum(i + off, hi), 0, 0),
                                            memory_space=pltpu.SMEM)
    grid_spec = pltpu.PrefetchScalarGridSpec(
        num_scalar_prefetch=3,
        grid=(MOE_TILES, nf),
        in_specs=[idx_spec(0, last + 1), idx_spec(1, last + 1),
                  idx_spec(0, last), idx_spec(1, last),
                  pl.BlockSpec(memory_space=pl.ANY),
                  pl.BlockSpec((None, None, D, MOE_TF), lambda i, f, te, ta, tc: (0, te[i], 0, fblk(i, f, ta))),
                  pl.BlockSpec((None, None, D, MOE_TF), lambda i, f, te, ta, tc: (0, te[i], 0, nf + fblk(i, f, ta))),
                  pl.BlockSpec((None, None, MOE_TF, D), lambda i, f, te, ta, tc: (0, te[i], fblk(i, f, ta), 0))],
        out_specs=pl.BlockSpec(memory_space=pl.ANY),
        scratch_shapes=[pltpu.VMEM((2, MOE_BUF, D), F32), pltpu.VMEM((2, MOE_BUF, D), F32),
                        pltpu.VMEM((MOE_BM, D), BF16),
                        pltpu.SemaphoreType.DMA((2,)), pltpu.SemaphoreType.DMA((2,))])
    return pl.pallas_call(
        _moe_group_kernel,
        grid_spec=grid_spec,
        out_shape=jax.ShapeDtypeStruct((MOE_PAIRS, D), F32),
        compiler_params=_cparams(("arbitrary", "arbitrary")),
        name="moe_group",
    )(tile_expert, tile_active, tile_count, dst_row, dst_row, src_tok, src_tok, h2, w_in, w_in, w_out)


MOE_COMBINE_BM = 512


def _moe_combine_kernel(route_ref, x_ref, m_ref, nf_ref, y1_ref, y2_ref, outc_ref, outl_ref):
    route = route_ref[...]
    f = (y1_ref[...] * route[:, ROUTE_W1:ROUTE_W1 + 1]
         + y2_ref[...] * route[:, ROUTE_W2:ROUTE_W2 + 1])
    xn = x_ref[...] + m_ref[5:6, :] * f
    y = xn * lax.rsqrt(jnp.mean(xn * xn, axis=-1, keepdims=True) + EPS) * nf_ref[...]
    is_ctx = pl.program_id(0) < N_CTX // MOE_COMBINE_BM

    @pl.when(is_ctx)
    def _():
        outc_ref[...] = y

    @pl.when(jnp.logical_not(is_ctx))
    def _():
        outl_ref[...] = y


def _moe_combine(ys, route, x, mods, norm_final):
    bm = MOE_COMBINE_BM
    nt = N_TOK // bm
    nct = N_CTX // bm
    return pl.pallas_call(
        _moe_combine_kernel,
        grid=(nt,),
        in_specs=[pl.BlockSpec((bm, 128), lambda i: (i, 0)),
                  pl.BlockSpec((bm, D), lambda i: (i, 0)),
                  pl.BlockSpec((None, 6, D), lambda i: (_group_of_rows(i * bm), 0, 0)),
                  pl.BlockSpec((1, D), lambda i: (0, 0)),
                  pl.BlockSpec((bm, D), lambda i: (i, 0)),
                  pl.BlockSpec((bm, D), lambda i: (nt + i, 0))],
        out_specs=[pl.BlockSpec((bm, D), lambda i: (jnp.minimum(i, nct - 1), 0)),
                   pl.BlockSpec((bm, D), lambda i: (jnp.maximum(i - nct, 0), 0))],
        out_shape=[jax.ShapeDtypeStruct((N_CTX, D), F32), jax.ShapeDtypeStruct((N_LAT, D), F32)],
        compiler_params=_cparams(("arbitrary",)),
        name="moe_combine",
    )(route, x, mods, norm_final.reshape(1, D), ys, ys)


def _moe(h2, route, x, mods, w_in, w_out, norm_final):
    src_tok, dst_row, tile_expert, tile_active, tile_count = _moe_routing(route)
    ys = _moe_group(h2, src_tok, dst_row, tile_expert, tile_active, tile_count, w_in, w_out)
    return _moe_combine(ys, route, x, mods, norm_final)


def _hgrn_lower_bounds(lb_param):
    sm = jax.nn.softmax(lb_param.astype(F32), axis=0)
    return jnp.cumsum(sm, axis=0) - sm[0:1]


def kernel(x_prompt, x_sample, cache_k, cache_v, state_hgrn, c, c_ctx, w_ada, b_ada, norm_mix, norm_ffn,
           w_in, hgrn_lb, hgrn_norm, pool_w, pool_scale, rpb, w_branch_a, w_branch_b, w_branch_c, w_out,
           ffn_w_in, ffn_w_out, router, moe_w_in, moe_w_out, norm_final):
    x = (x_prompt.reshape(N_CTX, D), x_sample.reshape(N_LAT, D))
    cond8 = jnp.concatenate([c_ctx[None], c, jnp.zeros((5, D), F32)], axis=0)
    mods = _adaln(cond8, w_ada, b_ada)[:, :3].reshape(DEPTH, 3, 6, D)
    lbs = _hgrn_lower_bounds(hgrn_lb)
    ck = cache_k.reshape(N_LAT_B, DEPTH, PAST, WIDTH).astype(BF16)
    cv = cache_v.reshape(N_LAT_B, DEPTH, PAST, WIDTH).astype(BF16)
    w_in16 = w_in.astype(BF16)
    router_pad = jnp.pad(router[0], ((0, 0), (0, 128 - N_EXP)))
    r_hi = router_pad.astype(BF16)
    r_lo = (router_pad - r_hi.astype(F32)).astype(BF16)

    kvs, ss = [], []
    for l in range(DEPTH):
        z, zg, kv, h16 = _inproj(x, norm_mix[l], mods[l], w_in16, l)
        of_c, ob_c, s_ctx = _hgrn(z, zg, lbs[l], None, 0, N_CTX_B, CTX_T)
        of_l, ob_l, _ = _hgrn(z, zg, lbs[l], state_hgrn[:, l], N_CTX, N_LAT_B, LAT_T)
        o_pool = _pool(z, pool_w[l].astype(BF16), pool_scale[l])
        oc_c = _ctx_attn(z)
        oc_l = _na_attn(z, ck, cv, _na_bias_table(rpb[l]), l)
        merged = _merge((of_c, ob_c, oc_c), (of_l, ob_l, oc_l), z, h16, w_in16, l, o_pool, x, mods[l],
                        hgrn_norm[l], norm_ffn[l],
                        w_branch_a[l].astype(BF16), w_branch_b[l].astype(BF16),
                        w_branch_c[l].astype(BF16), w_out[l].astype(BF16),
                        (r_hi, r_lo) if l % 2 == 1 else None)
        if l % 2 == 0:
            x, h2 = merged
            x = _ffn(h2, x, mods[l], ffn_w_in.astype(BF16), ffn_w_out.astype(BF16))
        else:
            x, h2, route = merged
            y_ctx, y_lat = _moe(h2, route, x, mods[l], moe_w_in, moe_w_out, norm_final)
        kvs.append(kv.reshape(N_CTX_B, CTX_T, 2, NA_H, NA_D))
        ss.append(s_ctx)
    kv = jnp.stack(kvs, axis=1)
    y_prompt = y_ctx.reshape(N_CTX_B, CTX_T, D)
    y_sample = y_lat.reshape(N_LAT_B, LAT_T, D)
    return (y_prompt, y_sample, kv[:, :, :, 0], kv[:, :, :, 1], jnp.stack(ss, axis=1))
```

```python
import functools

import numpy as np
import jax
import jax.numpy as jnp
from jax import lax
from jax.experimental import pallas as pl
from jax.experimental.pallas import tpu as pltpu

F32 = jnp.float32
BF16 = jnp.bfloat16

D = 1024
N_CTX_B, CTX_T = 32, 256
N_LAT_B, LAT_T = 2, 4096
N_CTX = N_CTX_B * CTX_T
N_LAT = N_LAT_B * LAT_T
N_TOK = N_CTX + N_LAT
DEPTH = 2
GRID_W = 64
GRID_ROWS = LAT_T // GRID_W
PAST = 512
HG_H, HG_D = 4, 128
WIDTH = 512
NA_H, NA_D = 8, 64
WIN_ROWS, WIN_COLS = 8, 16
POOL_WINDOWS = (2, 4, 8, 16)
IN_COLS = 7680
F_DENSE = 2816
N_EXP = 8
F_EXP = 3584
GATE_CLIP = 30.0
EPS = 1e-6
NEG_BIG = -1e30

COL_AQ, COL_AFF, COL_AFB, COL_AI, COL_AG, COL_BU, COL_CQ, COL_CK, COL_CV = range(9)
GL_COL0 = 9 * WIDTH

ROUTE_I1, ROUTE_I2, ROUTE_W1, ROUTE_W2 = 8, 9, 10, 11

VMEM_LIMIT = 56 * 1024 * 1024


def _cparams(sem):
    return pltpu.CompilerParams(dimension_semantics=sem, vmem_limit_bytes=VMEM_LIMIT)


def _sigmoid(x):
    return 1.0 / (1.0 + jnp.exp(-x))


def _silu(x):
    return x / (1.0 + jnp.exp(-x))


def _group_of_rows(row0):
    return jnp.maximum(row0 - N_CTX + LAT_T, 0) // LAT_T


def _adaln_kernel(c_ref, w_ref, b_ref, o_ref):
    s = _silu(c_ref[...]).astype(BF16)
    o_ref[...] = jnp.dot(s, w_ref[...].astype(BF16), preferred_element_type=F32) + b_ref[...]


def _adaln(cond8, w_ada, b_ada):
    tn = 1536
    return pl.pallas_call(
        _adaln_kernel,
        grid=(DEPTH, 6 * D // tn),
        in_specs=[pl.BlockSpec((8, D), lambda l, j: (0, 0)),
                  pl.BlockSpec((None, D, tn), lambda l, j: (l, 0, j)),
                  pl.BlockSpec((None, 1, tn), lambda l, j: (l, 0, j))],
        out_specs=pl.BlockSpec((None, 8, tn), lambda l, j: (l, 0, j)),
        out_shape=jax.ShapeDtypeStruct((DEPTH, 8, 6 * D), F32),
        compiler_params=_cparams(("parallel", "parallel")),
        name="adaln",
    )(cond8, w_ada, b_ada.reshape(DEPTH, 1, 6 * D))


INPROJ_BM, INPROJ_BN = 1024, 3 * WIDTH


def _stream_rows(x, bm):
    if not isinstance(x, tuple):
        specs = [pl.BlockSpec((bm, D), lambda i, *_: (i, 0))]
        return specs, [x]
    nct = N_CTX // bm
    specs = [pl.BlockSpec((bm, D), lambda i, *_: (jnp.minimum(i, nct - 1), 0)),
             pl.BlockSpec((bm, D), lambda i, *_: (jnp.maximum(i - nct, 0), 0))]
    return specs, list(x)


def _read_stream_rows(x_refs, bm):
    if len(x_refs) == 1:
        return x_refs[0][...]
    return jnp.where(pl.program_id(0) < N_CTX // bm, x_refs[0][...], x_refs[1][...])


def _inproj_kernel(*refs):
    nw_ref, m_ref, w_ref, z_ref, zg_ref, kv_ref, h_ref = refs[-7:]
    x_refs = refs[:-7]
    i = pl.program_id(0)
    j = pl.program_id(1)

    @pl.when(j == 0)
    def _():
        x = _read_stream_rows(x_refs, INPROJ_BM)
        y = x * lax.rsqrt(jnp.mean(x * x, axis=-1, keepdims=True) + EPS) * nw_ref[...]
        h_ref[...] = (y * (1.0 + m_ref[1:2, :]) + m_ref[0:1, :]).astype(BF16)

    acc = jnp.dot(h_ref[...], w_ref[...], preferred_element_type=F32)

    z_ref[...] = acc.astype(BF16)

    @pl.when(j == 0)
    def _():
        zg_ref[...] = acc[:, WIDTH:3 * WIDTH]

    @pl.when((j == 2) & (i < N_CTX // INPROJ_BM))
    def _():
        kv_ref[...] = acc[:, WIDTH:3 * WIDTH]


def _inproj(x, norm_w, mods, w_in16, layer):
    bm, bn = INPROJ_BM, INPROJ_BN
    nct = N_CTX // bm
    x_specs, x_args = _stream_rows(x, bm)
    return pl.pallas_call(
        _inproj_kernel,
        grid=(N_TOK // bm, GL_COL0 // bn),
        in_specs=x_specs + [
            pl.BlockSpec((1, D), lambda i, j: (0, 0)),
            pl.BlockSpec((None, 6, D), lambda i, j: (_group_of_rows(i * bm), 0, 0)),
            pl.BlockSpec((None, D, bn), lambda i, j: (layer, 0, j))],
        out_specs=[pl.BlockSpec((bm, bn), lambda i, j: (i, j)),
                   pl.BlockSpec((bm, 2 * WIDTH), lambda i, j: (i, 0)),
                   pl.BlockSpec((bm, 2 * WIDTH), lambda i, j: (jnp.minimum(i, nct - 1), 0)),
                   pl.BlockSpec((bm, D), lambda i, j: (i, 0))],
        out_shape=[jax.ShapeDtypeStruct((N_TOK, GL_COL0), BF16),
                   jax.ShapeDtypeStruct((N_TOK, 2 * WIDTH), F32),
                   jax.ShapeDtypeStruct((N_CTX, 2 * WIDTH), F32),
                   jax.ShapeDtypeStruct((N_TOK, D), BF16)],
        compiler_params=_cparams(("arbitrary", "arbitrary")),
        name="inproj",
    )(*x_args, norm_w.reshape(1, D), mods, w_in16)


HG_C = 128
HG_LEVELS = (4, 8, 16, 32, 64, 128)
LOG2_E = 1.4426950408889634
EXP2_CAP = 120.0


def _hgrn_level_ids(reverse):
    t = np.arange(HG_C)[:, None]
    s = np.arange(HG_C)[None, :]
    if reverse:
        t, s = s, t
    lev = np.full((HG_C, HG_C), -1, np.int32)
    lev[(t // 4 == s // 4) & (s <= t)] = 0
    for li, L in enumerate(HG_LEVELS[1:], start=1):
        m = (t // L == s // L) & (t % L >= L // 2) & (s % L < L // 2)
        lev[m] = li
    return lev


def _hgrn_ref_rows(b_scr, d, reverse):
    out = []
    r_lo, r_hi = (2, 6) if reverse else (1, 5)
    sub = lax.broadcasted_iota(jnp.int32, (8, WIDTH), 0)
    pieces = []
    for g in range(HG_C // 8):
        lo = jnp.broadcast_to(b_scr[d, 8 * g + r_lo:8 * g + r_lo + 1, :], (8, WIDTH))
        hi = jnp.broadcast_to(b_scr[d, 8 * g + r_hi:8 * g + r_hi + 1, :], (8, WIDTH))
        pieces.append(jnp.where(sub < 4, lo, hi))
    out.append(jnp.concatenate(pieces, axis=0))
    for L in HG_LEVELS[1:]:
        r = L // 2 - 1 if reverse else L // 2
        pieces = [jnp.broadcast_to(b_scr[d, L * g + r:L * g + r + 1, :], (L, WIDTH))
                  for g in range(HG_C // L)]
        out.append(pieces[0] if len(pieces) == 1 else jnp.concatenate(pieces, axis=0))
    return out


def _hgrn_kernel(*refs, has_init):
    if has_init:
        (qf_ref, ff_ref, vf_ref, qb_ref, fb_ref, vb_ref, lb_ref, levf_ref, levb_ref, s0_ref,
         of_ref, ob_ref, so_ref, st_scr, b_scr) = refs
    else:
        (qf_ref, ff_ref, vf_ref, qb_ref, fb_ref, vb_ref, lb_ref, levf_ref, levb_ref,
         of_ref, ob_ref, so_ref, st_scr, b_scr) = refs
    c = pl.program_id(1)
    C = HG_C

    @pl.when(c == 0)
    def _():
        for d in range(2):
            for h in range(HG_H):
                if has_init:
                    st_scr[d, h] = s0_ref[d, h].T
                else:
                    st_scr[d, h] = jnp.zeros((HG_D, HG_D), F32)

    row = lax.broadcasted_iota(jnp.int32, (C, C), 0)
    col = lax.broadcasted_iota(jnp.int32, (C, C), 1)
    dirs = ((qf_ref, ff_ref, vf_ref, levf_ref, of_ref), (qb_ref, fb_ref, vb_ref, levb_ref, ob_ref))
    for d, (q_ref, f_ref, v_ref, lev_ref, o_ref) in enumerate(dirs):
        reverse = d == 1
        tri = jnp.where((col >= row) if reverse else (col <= row), 1.0, 0.0).astype(BF16)
        q = _silu(q_ref[...].astype(F32))
        fx = jnp.clip(f_ref[...], -GATE_CLIP, GATE_CLIP)
        e = jnp.exp(-fx)
        sig_pos = 1.0 / (1.0 + e)
        sig_neg = e * sig_pos
        lb = lb_ref[d:d + 1, :]
        lf = jnp.log(lb + (1.0 - lb) * sig_pos) * LOG2_E
        k = (1.0 - lb) * sig_neg
        q16 = q.astype(BF16)
        k16 = k.astype(BF16)
        hi = lf.astype(BF16)
        lo = (lf - hi.astype(F32)).astype(BF16)
        b = (jnp.dot(tri, hi, preferred_element_type=F32)
             + jnp.dot(tri, lo, preferred_element_type=F32))
        b_scr[d] = b
        refs_m = _hgrn_ref_rows(b_scr, d, reverse)
        qs, ks = [], []
        for li, m in enumerate(refs_m):
            dlt = b - m
            if li == 0:
                qs.append(q16 * jnp.exp2(jnp.minimum(dlt, EXP2_CAP)).astype(BF16))
                ks.append(k16 * jnp.exp2(jnp.minimum(-dlt, EXP2_CAP)).astype(BF16))
            else:
                fac = jnp.exp2(-jnp.abs(dlt)).astype(BF16)
                qs.append(q16 * fac)
                ks.append(k16 * fac)
        lev = lev_ref[...]
        b_end = b[0:1, :] if reverse else b[C - 1:C, :]
        q_in = q16 * jnp.exp2(b).astype(BF16)
        k_out = k16 * jnp.exp2(b_end - b).astype(BF16)
        dec = jnp.exp2(b_end)
        vb16 = v_ref[...]
        for h in range(HG_H):
            sl = slice(h * HG_D, (h + 1) * HG_D)
            a = jnp.zeros((C, C), F32)
            for li in range(len(HG_LEVELS)):
                p = lax.dot_general(qs[li][:, sl], ks[li][:, sl], (((1,), (1,)), ((), ())),
                                    preferred_element_type=F32)
                a = jnp.where(lev == li, p, a)
            vh = vb16[:, sl]
            st = st_scr[d, h]
            o = (jnp.dot(a.astype(BF16), vh, preferred_element_type=F32)
                 + lax.dot_general(q_in[:, sl], st.astype(BF16), (((1,), (1,)), ((), ())),
                                   preferred_element_type=F32))
            o_ref[:, sl] = o
            upd = lax.dot_general(vh, k_out[:, sl], (((0,), (0,)), ((), ())),
                                  preferred_element_type=F32)
            st_scr[d, h] = st * dec[:, sl] + upd

    @pl.when(c == pl.num_programs(1) - 1)
    def _():
        for d in range(2):
            for h in range(HG_H):
                so_ref[d, h] = st_scr[d, h].T


def _hgrn(z, zg, lb, s0, row_off, n_b, seq_t):
    C = HG_C
    nc = seq_t // C
    base = row_off // C
    has_init = s0 is not None

    def fwd(col):
        return pl.BlockSpec((C, WIDTH), lambda b, c: (base + b * nc + c, col))

    def bwd(col):
        return pl.BlockSpec((C, WIDTH), lambda b, c: (base + b * nc + nc - 1 - c, col))

    full = lambda shape: pl.BlockSpec(shape, lambda b, c: (0,) * len(shape))
    in_specs = [fwd(COL_AQ), fwd(0), fwd(COL_AI), bwd(COL_AQ), bwd(1), bwd(COL_AI),
                full((2, WIDTH)), full((C, C)), full((C, C))]
    args = [z, zg, z, z, zg, z, lb, jnp.asarray(_hgrn_level_ids(False)), jnp.asarray(_hgrn_level_ids(True))]
    if has_init:
        in_specs.append(pl.BlockSpec((None, 2, HG_H, HG_D, HG_D), lambda b, c: (b, 0, 0, 0, 0)))
        args.append(s0)
    return pl.pallas_call(
        functools.partial(_hgrn_kernel, has_init=has_init),
        grid=(n_b, nc),
        in_specs=in_specs,
        out_specs=[pl.BlockSpec((C, WIDTH), lambda b, c: (b * nc + c, 0)),
                   pl.BlockSpec((C, WIDTH), lambda b, c: (b * nc + nc - 1 - c, 0)),
                   pl.BlockSpec((None, 2, HG_H, HG_D, HG_D), lambda b, c: (b, 0, 0, 0, 0))],
        out_shape=[jax.ShapeDtypeStruct((n_b * seq_t, WIDTH), F32),
                   jax.ShapeDtypeStruct((n_b * seq_t, WIDTH), F32),
                   jax.ShapeDtypeStruct((n_b, 2, HG_H, HG_D, HG_D), F32)],
        scratch_shapes=[pltpu.VMEM((2, HG_H, HG_D, HG_D), F32),
                        pltpu.VMEM((2, C, WIDTH), F32)],
        compiler_params=_cparams(("parallel", "arbitrary")),
        name="hgrn_lat" if has_init else "hgrn_ctx",
    )(*args)


POOL_T = 256
POOL_HALO = 16


def _pool_kernel(u_ref, up_ref, un_ref, w_ref, sc_ref, o_ref):
    i = pl.program_id(0)
    is_lat = i >= N_CTX // POOL_T
    seq_t = jnp.where(is_lat, LAT_T, CTX_T)
    t0 = jnp.where(is_lat, ((i - N_CTX // POOL_T) % (LAT_T // POOL_T)) * POOL_T, 0)
    TT, HL = POOL_T, POOL_HALO
    diff = lax.broadcasted_iota(jnp.int32, (TT, TT), 1) - lax.broadcasted_iota(jnp.int32, (TT, TT), 0)
    diff_h = lax.broadcasted_iota(jnp.int32, (TT, HL), 1) - lax.broadcasted_iota(jnp.int32, (TT, HL), 0)
    t_glob = t0 + lax.broadcasted_iota(jnp.int32, (TT, 128), 0)
    has_prev = t0 > 0
    has_next = t0 + TT < seq_t
    for g, w in enumerate(POOL_WINDOWS):
        half = w // 2
        sl = slice(g * 128, (g + 1) * 128)
        band = jnp.where((diff >= -half) & (diff < half), 1.0, 0.0).astype(BF16)
        dp = diff_h - HL
        band_p = jnp.where((dp >= -half) & (dp < half) & has_prev, 1.0, 0.0).astype(BF16)
        dn = diff_h + TT
        band_n = jnp.where((dn >= -half) & (dn < half) & has_next, 1.0, 0.0).astype(BF16)
        u = u_ref[:, sl]
        s = (jnp.dot(band, u, preferred_element_type=F32)
             + jnp.dot(band_p, up_ref[:, sl], preferred_element_type=F32)
             + jnp.dot(band_n, un_ref[:, sl], preferred_element_type=F32))
        cnt = (jnp.minimum(t_glob + half, seq_t) - jnp.maximum(t_glob - half, 0)).astype(F32)
        dd = s / cnt - u.astype(F32)
        y = jnp.dot(dd.astype(BF16), w_ref[g], preferred_element_type=F32)
        o_ref[:, sl] = y * sc_ref[:, sl]


def _pool(z, pool_w16, pool_scale):
    nt = N_TOK // POOL_T
    per = POOL_T // POOL_HALO
    return pl.pallas_call(
        _pool_kernel,
        grid=(nt,),
        in_specs=[pl.BlockSpec((POOL_T, WIDTH), lambda i: (i, COL_BU)),
                  pl.BlockSpec((POOL_HALO, WIDTH), lambda i: (jnp.maximum(i * per - 1, 0), COL_BU)),
                  pl.BlockSpec((POOL_HALO, WIDTH), lambda i: (jnp.minimum((i + 1) * per, nt * per - 1), COL_BU)),
                  pl.BlockSpec((4, 128, 128), lambda i: (0, 0, 0)),
                  pl.BlockSpec((1, WIDTH), lambda i: (0, 0))],
        out_specs=pl.BlockSpec((POOL_T, WIDTH), lambda i: (i, 0)),
        out_shape=jax.ShapeDtypeStruct((N_TOK, WIDTH), F32),
        compiler_params=_cparams(("parallel",)),
        name="pool",
    )(z, z, z, pool_w16, pool_scale.reshape(1, WIDTH))


def _head_mask(hh):
    lane = lax.broadcasted_iota(jnp.int32, (1, 128), 1)
    in_head = (lane >= hh * NA_D) & (lane < (hh + 1) * NA_D)
    return jnp.where(in_head, NA_D ** -0.5, 0.0).astype(BF16)


def _ctx_attn_kernel(q_ref, k_ref, v_ref, o_ref):
    lane = lax.broadcasted_iota(jnp.int32, (CTX_T, 128), 1)
    for j in range(NA_H // 2):
        sl = slice(j * 128, (j + 1) * 128)
        q = q_ref[:, sl]
        kt = k_ref[:, sl]
        vt = v_ref[:, sl]
        outs = []
        for hh in range(2):
            s = lax.dot_general(q * _head_mask(hh), kt, (((1,), (1,)), ((), ())), preferred_element_type=F32)
            m = jnp.max(s, axis=-1, keepdims=True)
            p = jnp.exp(s - m)
            l = jnp.sum(p, axis=-1, keepdims=True)
            outs.append(jnp.dot(p.astype(BF16), vt, preferred_element_type=F32) / l)
        o_ref[:, sl] = jnp.where(lane < NA_D, outs[0], outs[1])


def _ctx_attn(z):
    spec = lambda col: pl.BlockSpec((CTX_T, WIDTH), lambda b: (b, col))
    return pl.pallas_call(
        _ctx_attn_kernel,
        grid=(N_CTX_B,),
        in_specs=[spec(COL_CQ), spec(COL_CK), spec(COL_CV)],
        out_specs=pl.BlockSpec((CTX_T, WIDTH), lambda b: (b, 0)),
        out_shape=jax.ShapeDtypeStruct((N_CTX, WIDTH), F32),
        compiler_params=_cparams(("parallel",)),
        name="ctx_attn",
    )(z, z, z)


NA_QR = 4
NA_KR = 12
NA_NQ = NA_QR * GRID_W
NA_NK = NA_KR * GRID_W
NA_BLOCKS = GRID_ROWS // NA_QR


def _na_key_row0(blk):
    return jnp.clip(blk * NA_QR - WIN_ROWS // 2, 0, GRID_ROWS - NA_KR)


def _na_geometry():
    patterns, var_of_block = [], []
    for blk in range(NA_BLOCKS):
        r = blk * NA_QR + np.arange(NA_QR)[:, None]
        kr = int(np.clip(blk * NA_QR - WIN_ROWS // 2, 0, GRID_ROWS - NA_KR)) + np.arange(NA_KR)[None, :]
        rs = np.clip(r - WIN_ROWS // 2, 0, GRID_ROWS - WIN_ROWS)
        valid = (kr >= rs) & (kr < rs + WIN_ROWS)
        assert (valid.sum(axis=1) == WIN_ROWS).all(), "key rows must cover every query row's window"
        drow = np.where(valid, kr - r + WIN_ROWS - 1, 0)
        key = (drow.tobytes(), valid.tobytes())
        ids = [i for i, (k_, _, _) in enumerate(patterns) if k_ == key]
        if not ids:
            patterns.append((key, drow, valid))
            ids = [len(patterns) - 1]
        var_of_block.append(ids[0])
    drow = np.stack([p[1] for p in patterns])
    valid = np.stack([p[2] for p in patterns])
    return np.asarray(var_of_block, np.int32), drow, valid


NA_DROWS = 2 * WIN_ROWS - 1


def _na_bias_kernel(didx_ref, t2_ref, o_ref):
    v = pl.program_id(0)
    for qr in range(NA_QR):
        for kp in range(NA_KR // 2):
            base = (v * NA_QR + qr) * NA_KR + 2 * kp
            pair = jnp.concatenate([t2_ref[didx_ref[base]], t2_ref[didx_ref[base + 1]]], axis=1)
            o_ref[qr * GRID_W:(qr + 1) * GRID_W, kp * 128:(kp + 1) * 128] = pair


def _na_bias_table(rpb_l):
    _, drow, valid = _na_geometry()
    n_var = valid.shape[0]
    qc = np.arange(GRID_W)[:, None]
    kc = np.arange(GRID_W)[None, :]
    q0 = np.clip(qc - WIN_COLS // 2, 0, GRID_W - WIN_COLS)
    col_in = (kc >= q0) & (kc < q0 + WIN_COLS)
    dcol = np.clip(kc - qc, -(WIN_COLS - 1), WIN_COLS - 1) + WIN_COLS - 1
    oh_col = (dcol[None] == np.arange(2 * WIN_COLS - 1)[:, None, None]).astype(np.float32)
    t2 = jnp.einsum('hdc,cqk->hdqk', rpb_l, jnp.asarray(oh_col), precision=lax.Precision.HIGHEST)
    t2 = jnp.where(col_in[None, None], t2, NEG_BIG)
    t2 = jnp.concatenate([t2, jnp.full((NA_H, 1, GRID_W, GRID_W), NEG_BIG, F32)], axis=1)
    didx = np.where(valid, drow, NA_DROWS).astype(np.int32).reshape(-1)
    grid_spec = pltpu.PrefetchScalarGridSpec(
        num_scalar_prefetch=1,
        grid=(n_var, NA_H),
        in_specs=[pl.BlockSpec((None, NA_DROWS + 1, GRID_W, GRID_W), lambda v, h, didx: (h, 0, 0, 0))],
        out_specs=pl.BlockSpec((None, None, NA_NQ, NA_NK), lambda v, h, didx: (v, h, 0, 0)))
    return pl.pallas_call(
        _na_bias_kernel,
        grid_spec=grid_spec,
        out_shape=jax.ShapeDtypeStruct((n_var, NA_H, NA_NQ, NA_NK), F32),
        compiler_params=_cparams(("parallel", "parallel")),
        name="na_bias",
    )(jnp.asarray(didx), t2)


def _na_attn_kernel(var_ref, q_ref, k_ref, v_ref, kc_ref, vc_ref, tbl_ref, o_ref):
    del var_ref
    blk = pl.program_id(1)
    k0 = pl.multiple_of(_na_key_row0(blk) * GRID_W, GRID_W)
    lane = lax.broadcasted_iota(jnp.int32, (NA_NQ, 128), 1)
    for j in range(NA_H // 2):
        sl = slice(j * 128, (j + 1) * 128)
        q = q_ref[:, sl]
        kt = k_ref[pl.ds(k0, NA_NK), sl]
        vt = v_ref[pl.ds(k0, NA_NK), sl]
        kct = kc_ref[:, sl]
        vct = vc_ref[:, sl]
        outs = []
        for hh in range(2):
            qm = q * _head_mask(hh)
            s_loc = lax.dot_general(qm, kt, (((1,), (1,)), ((), ())), preferred_element_type=F32)
            s_ctx = lax.dot_general(qm, kct, (((1,), (1,)), ((), ())), preferred_element_type=F32)
            tb = tbl_ref[2 * j + hh]
            s_loc = jnp.where(tb > 0.5 * NEG_BIG, s_loc + tb, NEG_BIG)
            m = jnp.maximum(jnp.max(s_loc, axis=-1, keepdims=True), jnp.max(s_ctx, axis=-1, keepdims=True))
            p_loc = jnp.exp(s_loc - m)
            p_ctx = jnp.exp(s_ctx - m)
            l = jnp.sum(p_loc, axis=-1, keepdims=True) + jnp.sum(p_ctx, axis=-1, keepdims=True)
            o = (jnp.dot(p_loc.astype(BF16), vt, preferred_element_type=F32)
                 + jnp.dot(p_ctx.astype(BF16), vct, preferred_element_type=F32))
            outs.append(o / l)
        o_ref[:, sl] = jnp.where(lane < NA_D, outs[0], outs[1])


def _na_attn(z, cache_k16, cache_v16, tbl, layer):
    base_q = N_CTX // NA_NQ
    base_t = N_CTX // LAT_T
    ctx_spec = pl.BlockSpec((None, None, PAST, WIDTH), lambda b, r, var: (b, layer, 0, 0))
    grid_spec = pltpu.PrefetchScalarGridSpec(
        num_scalar_prefetch=1,
        grid=(N_LAT_B, NA_BLOCKS),
        in_specs=[pl.BlockSpec((NA_NQ, WIDTH), lambda b, r, var: (base_q + b * NA_BLOCKS + r, COL_CQ)),
                  pl.BlockSpec((LAT_T, WIDTH), lambda b, r, var: (base_t + b, COL_CK)),
                  pl.BlockSpec((LAT_T, WIDTH), lambda b, r, var: (base_t + b, COL_CV)),
                  ctx_spec, ctx_spec,
                  pl.BlockSpec((None, NA_H, NA_NQ, NA_NK), lambda b, r, var: (var[r], 0, 0, 0))],
        out_specs=pl.BlockSpec((NA_NQ, WIDTH), lambda b, r, var: (b * NA_BLOCKS + r, 0)))
    return pl.pallas_call(
        _na_attn_kernel,
        grid_spec=grid_spec,
        out_shape=jax.ShapeDtypeStruct((N_LAT, WIDTH), F32),
        compiler_params=_cparams(("parallel", "arbitrary")),
        name="na_attn",
    )(jnp.asarray(_na_geometry()[0]), z, z, z, cache_k16, cache_v16, tbl)


MERGE_BM = 512
MERGE_SLABS = 2


def _merge_kernel(*refs, with_router, n_x):
    x_refs, refs = refs[:n_x], refs[n_x:]
    (ofc_ref, obc_ref, occ_ref, ofl_ref, obl_ref, ocl_ref, ag_ref, op_ref, h16_ref, wg0_ref, wg1_ref, m_ref,
     hn_ref, nf_ref, wa_ref, wb_ref, wc_ref, wo_ref) = refs[:18]
    if with_router:
        rhi_ref, rlo_ref, xn_ref, h2_ref, route_ref = refs[18:]
    else:
        xn_ref, h2_ref = refs[18:]
    is_ctx = pl.program_id(0) < N_CTX // MERGE_BM
    for sl in range(MERGE_SLABS):
        rs = slice(sl * MERGE_BM // MERGE_SLABS, (sl + 1) * MERGE_BM // MERGE_SLABS)
        o = jnp.where(is_ctx, ofc_ref[rs, :] + obc_ref[rs, :], ofl_ref[rs, :] + obl_ref[rs, :])
        oc = jnp.where(is_ctx, occ_ref[rs, :], ocl_ref[rs, :])
        parts = []
        for h in range(HG_H):
            oh = o[:, h * HG_D:(h + 1) * HG_D]
            parts.append(oh * lax.rsqrt(jnp.mean(oh * oh, axis=-1, keepdims=True) + EPS))
        oa = jnp.concatenate(parts, axis=1) * hn_ref[...] * _silu(ag_ref[rs, :].astype(F32))
        h16 = h16_ref[rs, :]
        sg0 = _sigmoid(jnp.dot(h16, wg0_ref[...], preferred_element_type=F32).astype(BF16)).astype(F32)
        sg1 = _sigmoid(jnp.dot(h16, wg1_ref[...], preferred_element_type=F32).astype(BF16)).astype(F32)
        sga = sg0[:, :D]
        sgb = jnp.concatenate([sg0[:, D:], sg1[:, :WIDTH]], axis=1)
        sgc = sg1[:, WIDTH:]
        mix = (sga * jnp.dot(oa.astype(BF16), wa_ref[...], preferred_element_type=F32)
               + sgb * jnp.dot(op_ref[rs, :].astype(BF16), wb_ref[...], preferred_element_type=F32)
               + sgc * jnp.dot(oc.astype(BF16), wc_ref[...], preferred_element_type=F32))
        x = x_refs[0][rs, :] if len(x_refs) == 1 else jnp.where(is_ctx, x_refs[0][rs, :], x_refs[1][rs, :])
        xn = x + m_ref[2:3, :] * jnp.dot(mix.astype(BF16), wo_ref[...], preferred_element_type=F32)
        xn_ref[rs, :] = xn
        y = xn * lax.rsqrt(jnp.mean(xn * xn, axis=-1, keepdims=True) + EPS) * nf_ref[...]
        h2 = y * (1.0 + m_ref[4:5, :]) + m_ref[3:4, :]
        h2_ref[rs, :] = h2.astype(h2_ref.dtype)
        if not with_router:
            continue
        hhi = h2.astype(BF16)
        hlo = (h2 - hhi.astype(F32)).astype(BF16)
        logits = (jnp.dot(hhi, rhi_ref[...], preferred_element_type=F32)
                  + jnp.dot(hhi, rlo_ref[...], preferred_element_type=F32)
                  + jnp.dot(hlo, rhi_ref[...], preferred_element_type=F32))
        lane = lax.broadcasted_iota(jnp.int32, logits.shape, 1).astype(F32)
        lg = jnp.where(lane < N_EXP, logits, -jnp.inf)
        m1 = jnp.max(lg, axis=-1, keepdims=True)
        i1 = jnp.min(jnp.where(lg == m1, lane, 128.0), axis=-1, keepdims=True)
        lg2 = jnp.where(lane == i1, -jnp.inf, lg)
        m2 = jnp.max(lg2, axis=-1, keepdims=True)
        i2 = jnp.min(jnp.where(lg2 == m2, lane, 128.0), axis=-1, keepdims=True)
        e = jnp.exp(m2 - m1)
        w1 = 1.0 / (1.0 + e)
        route_ref[rs, :] = (jnp.where(lane == ROUTE_I1, i1, 0.0) + jnp.where(lane == ROUTE_I2, i2, 0.0)
                            + jnp.where(lane == ROUTE_W1, w1, 0.0) + jnp.where(lane == ROUTE_W2, e * w1, 0.0))


def _merge(ctx_parts, lat_parts, z, h16, w_in16, layer, o_pool, x, mods, hgrn_norm_l, norm_ffn_l, wa, wb, wc, wo,
           router_split):
    bm = MERGE_BM
    glw = 1536
    nct = N_CTX // bm
    with_router = router_split is not None
    row = lambda w, col=0: pl.BlockSpec((bm, w), lambda i: (i, col))
    ctx_row = pl.BlockSpec((bm, WIDTH), lambda i: (jnp.minimum(i, nct - 1), 0))
    lat_row = pl.BlockSpec((bm, WIDTH), lambda i: (jnp.maximum(i - nct, 0), 0))
    const = lambda shape: pl.BlockSpec(shape, lambda i: (0,) * len(shape))
    x_specs, x_args = _stream_rows(x, bm)
    gate_w = lambda t: pl.BlockSpec((None, D, glw), lambda i: (layer, 0, GL_COL0 // glw + t))
    in_specs = x_specs + [ctx_row] * 3 + [lat_row] * 3 + [
        row(WIDTH, COL_AG), row(WIDTH),
        row(D), gate_w(0), gate_w(1),
        pl.BlockSpec((None, 6, D), lambda i: (_group_of_rows(i * bm), 0, 0)),
        const((1, WIDTH)), const((1, D)),
        const((WIDTH, D)), const((WIDTH, D)), const((WIDTH, D)), const((D, D))]
    args = x_args + list(ctx_parts) + list(lat_parts) + [
        z, o_pool, h16, w_in16, w_in16, mods, jnp.tile(hgrn_norm_l, HG_H).reshape(1, WIDTH),
        norm_ffn_l.reshape(1, D), wa, wb, wc, wo]
    out_specs = [row(D), row(D)]
    out_shape = [jax.ShapeDtypeStruct((N_TOK, D), F32),
                 jax.ShapeDtypeStruct((N_TOK, D), F32 if with_router else BF16)]
    if with_router:
        in_specs += [const((D, 128)), const((D, 128))]
        args += list(router_split)
        out_specs.append(row(128))
        out_shape.append(jax.ShapeDtypeStruct((N_TOK, 128), F32))
    return pl.pallas_call(
        functools.partial(_merge_kernel, with_router=with_router, n_x=len(x_args)),
        grid=(N_TOK // bm,),
        in_specs=in_specs,
        out_specs=out_specs,
        out_shape=out_shape,
        compiler_params=_cparams(("parallel",)),
        name="merge_route" if with_router else "merge",
    )(*args)


def _ffn_kernel(h_ref, x_ref, m_ref, wa_ref, wb_ref, wo_ref, o_ref):
    f = pl.program_id(1)
    h = h_ref[...]
    a = jnp.dot(h, wa_ref[...], preferred_element_type=F32)
    b = jnp.dot(h, wb_ref[...], preferred_element_type=F32)
    g = (_silu(a) * b).astype(BF16)
    part = jnp.dot(g, wo_ref[...], preferred_element_type=F32)

    @pl.when(f == 0)
    def _():
        o_ref[...] = part

    last = pl.num_programs(1) - 1

    @pl.when((f > 0) & (f < last))
    def _():
        o_ref[...] += part

    @pl.when(f == last)
    def _():
        o_ref[...] = x_ref[...] + m_ref[5:6, :] * (o_ref[...] + part)


FFN_BM = 512
FFN_TF = F_DENSE // 2


def _ffn(h2, x, mods, w_in, w_out):
    bm, tf = FFN_BM, FFN_TF
    nf = F_DENSE // tf
    assert nf >= 2, "the first and the last hidden chunk are handled by different branches"
    return pl.pallas_call(
        _ffn_kernel,
        grid=(N_TOK // bm, nf),
        in_specs=[pl.BlockSpec((bm, D), lambda i, f: (i, 0)),
                  pl.BlockSpec((bm, D), lambda i, f: (i, 0)),
                  pl.BlockSpec((None, 6, D), lambda i, f: (_group_of_rows(i * bm), 0, 0)),
                  pl.BlockSpec((None, D, tf), lambda i, f: (0, 0, f)),
                  pl.BlockSpec((None, D, tf), lambda i, f: (0, 0, nf + f)),
                  pl.BlockSpec((None, tf, D), lambda i, f: (0, f, 0))],
        out_specs=pl.BlockSpec((bm, D), lambda i, f: (i, 0)),
        out_shape=jax.ShapeDtypeStruct((N_TOK, D), F32),
        compiler_params=_cparams(("parallel", "arbitrary")),
        name="ffn",
    )(h2, x, mods, w_in, w_in, w_out)


MOE_BM = 1024
MOE_TF = 512
MOE_NF = F_EXP // MOE_TF
MOE_PAIRS = 2 * N_TOK
MOE_ROWS = MOE_PAIRS + N_EXP * MOE_BM
MOE_TILES = MOE_ROWS // MOE_BM
MOE_CH = -(-MOE_BM // MOE_NF)
MOE_MOVED = MOE_CH * MOE_NF
MOE_BUF = -(-MOE_MOVED // 8) * 8
ROW_DMA_PRIORITY = 1


def _moe_routing(route):
    i1 = route[:, ROUTE_I1].astype(jnp.int32)
    i2 = route[:, ROUTE_I2].astype(jnp.int32)
    ep = jnp.stack([i1, i2], axis=1).reshape(-1)
    onehot = (ep[:, None] == jnp.arange(N_EXP, dtype=jnp.int32)[None, :]).astype(jnp.int32)
    counts = jnp.sum(onehot, axis=0)
    padded = ((counts + MOE_BM - 1) // MOE_BM) * MOE_BM
    ends = jnp.cumsum(padded)
    starts = ends - padded
    first = jnp.cumsum(counts) - counts
    order = jnp.sort(ep * MOE_PAIRS + jnp.arange(MOE_PAIRS, dtype=jnp.int32)) & (MOE_PAIRS - 1)
    tile_row0 = jnp.arange(MOE_TILES, dtype=jnp.int32) * MOE_BM
    tile_active = (tile_row0 < ends[-1]).astype(jnp.int32)
    last_row0 = jnp.maximum(ends[-1] - MOE_BM, 0)
    tile_expert = jnp.sum((jnp.minimum(tile_row0, last_row0)[:, None] >= ends[None, :]).astype(jnp.int32), axis=1)
    tile_expert = jnp.minimum(tile_expert, N_EXP - 1)
    row = jnp.arange(MOE_ROWS, dtype=jnp.int32)
    row_e = jnp.repeat(tile_expert, MOE_BM)
    rank = row - starts[row_e]
    valid = (rank < counts[row_e]) & (row < ends[-1])
    pair = jnp.where(valid, order[jnp.clip(first[row_e] + rank, 0, MOE_PAIRS - 1)], -1)
    pair = pair.reshape(MOE_TILES, MOE_BM)
    tile_count = jnp.sum((pair >= 0).astype(jnp.int32), axis=1)
    src_tok = jnp.pad(jnp.maximum(pair, 0) >> 1, ((0, 0), (0, MOE_BUF - MOE_BM)))
    dst_row = jnp.where(pair >= 0, (pair & 1) * N_TOK + (pair >> 1), -1)
    dst_row = jnp.pad(dst_row, ((1, 0), (0, MOE_BUF - MOE_BM)), constant_values=-1)
    return (src_tok.reshape(MOE_TILES, 1, MOE_BUF), dst_row.reshape(MOE_TILES + 1, 1, MOE_BUF),
            tile_expert, tile_active, tile_count)


def _row(ref, r):
    return ref.at[pl.ds(r, 1), :]


def _moe_group_kernel(te_ref, ta_ref, tc_ref, dprv_ref, dcur_ref, gcur_ref, gnxt_ref, h2_ref, wa_ref, wb_ref, wo_ref,
                      out_ref, xbuf, ybuf, h_scr, gsem, ssem):
    del te_ref
    i = pl.program_id(0)
    f = pl.program_id(1)
    nt = pl.num_programs(0)
    s = i % 2
    o = 1 - s
    active = ta_ref[i] == 1
    prv_active = (i >= 1) & (ta_ref[jnp.maximum(i - 1, 0)] == 1)

    def gather_row(idx_ref, k, slot):
        pltpu.make_async_copy(_row(h2_ref, idx_ref[0, k]), _row(xbuf.at[slot], k),
                              gsem.at[slot]).start(priority=ROW_DMA_PRIORITY)

    def scatter_row(idx_ref, k, slot):
        dst = idx_ref[0, k]

        @pl.when(dst >= 0)
        def _():
            pltpu.make_async_copy(_row(ybuf.at[slot], k), _row(out_ref, dst),
                                  ssem.at[slot]).start(priority=ROW_DMA_PRIORITY)

    def wait_rows(sem, n):
        n = jnp.asarray(n, jnp.int32)
        n8 = pl.multiple_of((n >> 3) << 3, 8)

        @pl.when(n8 > 0)
        def _():
            pltpu.make_async_copy(xbuf.at[0, pl.ds(0, n8), :], ybuf.at[0, pl.ds(0, n8), :], sem).wait()

        def one(k, carry):
            pltpu.make_async_copy(_row(xbuf.at[0], 0), _row(ybuf.at[0], 0), sem).wait()
            return carry
        lax.fori_loop(0, n - n8, one, 0)

    def rows_loop(fn, n):
        def body(k, carry):
            fn(k)
            return carry
        lax.fori_loop(0, n, body, 0, unroll=8)

    @pl.when((i == 0) & (f == 0))
    def _():
        rows_loop(lambda k: gather_row(gcur_ref, k, 0), MOE_MOVED)

    @pl.when((f == 0) & (i >= 2))
    def _():
        wait_rows(ssem.at[s], tc_ref[jnp.maximum(i - 2, 0)])

    @pl.when((f == 0) & ((i == 0) | prv_active))
    def _():
        wait_rows(gsem.at[s], MOE_MOVED)

    @pl.when((f == 0) & active)
    def _():
        h_scr[...] = xbuf[s, 0:MOE_BM, :].astype(BF16)
        ybuf[s, 0:MOE_BM, :] = jnp.zeros((MOE_BM, D), F32)

    def step(rows):
        for u in range(MOE_CH):
            gather_row(gnxt_ref, f * MOE_CH + u, o)
        for u in range(MOE_CH):
            scatter_row(dprv_ref, f * MOE_CH + u, o)
        h = h_scr[0:rows, :]
        a = jnp.dot(h, wa_ref[...].astype(BF16), preferred_element_type=F32)
        b = jnp.dot(h, wb_ref[...].astype(BF16), preferred_element_type=F32)
        g = (_silu(a) * b).astype(BF16)
        ybuf[s, 0:rows, :] += jnp.dot(g, wo_ref[...].astype(BF16), preferred_element_type=F32)

    few = tc_ref[i] <= MOE_BM // 2

    @pl.when(active & jnp.logical_not(few))
    def _():
        step(MOE_BM)

    @pl.when(active & few)
    def _():
        step(MOE_BM // 2)

    @pl.when(jnp.logical_not(active) & prv_active)
    def _():
        rows_loop(lambda u: scatter_row(dprv_ref, f * MOE_CH + u, o), MOE_CH)

    @pl.when((i == nt - 1) & (f == pl.num_programs(1) - 1))
    def _():
        wait_rows(ssem.at[o], tc_ref[jnp.maximum(i - 1, 0)])
        rows_loop(lambda k: scatter_row(dcur_ref, k, s), MOE_BM)
        wait_rows(ssem.at[s], tc_ref[i])

        @pl.when(active)
        def _():
            wait_rows(gsem.at[o], MOE_MOVED)


def _moe_group(h2, src_tok, dst_row, tile_expert, tile_active, tile_count, w_in, w_out):
    nf = MOE_NF
    last = MOE_TILES - 1

    def fblk(i, f, ta):
        return jnp.where(ta[i] == 1, f, nf - 1)

    idx_spec = lambda off, hi: pl.BlockSpec((None, 1, MOE_BUF),
                                            lambda i, f, te, ta, tc: (jnp.minimum(i + off, hi), 0, 0),
                                            memory_space=pltpu.SMEM)
    grid_spec = pltpu.PrefetchScalarGridSpec(
        num_scalar_prefetch=3,
        grid=(MOE_TILES, nf),
        in_specs=[idx_spec(0, last + 1), idx_spec(1, last + 1),
                  idx_spec(0, last), idx_spec(1, last),
                  pl.BlockSpec(memory_space=pl.ANY),
                  pl.BlockSpec((None, None, D, MOE_TF), lambda i, f, te, ta, tc: (0, te[i], 0, fblk(i, f, ta))),
                  pl.BlockSpec((None, None, D, MOE_TF), lambda i, f, te, ta, tc: (0, te[i], 0, nf + fblk(i, f, ta))),
                  pl.BlockSpec((None, None, MOE_TF, D), lambda i, f, te, ta, tc: (0, te[i], fblk(i, f, ta), 0))],
        out_specs=pl.BlockSpec(memory_space=pl.ANY),
        scratch_shapes=[pltpu.VMEM((2, MOE_BUF, D), F32), pltpu.VMEM((2, MOE_BUF, D), F32),
                        pltpu.VMEM((MOE_BM, D), BF16),
                        pltpu.SemaphoreType.DMA((2,)), pltpu.SemaphoreType.DMA((2,))])
    return pl.pallas_call(
        _moe_group_kernel,
        grid_spec=grid_spec,
        out_shape=jax.ShapeDtypeStruct((MOE_PAIRS, D), F32),
        compiler_params=_cparams(("arbitrary", "arbitrary")),
        name="moe_group",
    )(tile_expert, tile_active, tile_count, dst_row, dst_row, src_tok, src_tok, h2, w_in, w_in, w_out)


MOE_COMBINE_BM = 512


def _moe_combine_kernel(route_ref, x_ref, m_ref, nf_ref, y1_ref, y2_ref, outc_ref, outl_ref):
    route = route_ref[...]
    f = (y1_ref[...] * route[:, ROUTE_W1:ROUTE_W1 + 1]
         + y2_ref[...] * route[:, ROUTE_W2:ROUTE_W2 + 1])
    xn = x_ref[...] + m_ref[5:6, :] * f
    y = xn * lax.rsqrt(jnp.mean(xn * xn, axis=-1, keepdims=True) + EPS) * nf_ref[...]
    is_ctx = pl.program_id(0) < N_CTX // MOE_COMBINE_BM

    @pl.when(is_ctx)
    def _():
        outc_ref[...] = y

    @pl.when(jnp.logical_not(is_ctx))
    def _():
        outl_ref[...] = y


def _moe_combine(ys, route, x, mods, norm_final):
    bm = MOE_COMBINE_BM
    nt = N_TOK // bm
    nct = N_CTX // bm
    return pl.pallas_call(
        _moe_combine_kernel,
        grid=(nt,),
        in_specs=[pl.BlockSpec((bm, 128), lambda i: (i, 0)),
                  pl.BlockSpec((bm, D), lambda i: (i, 0)),
                  pl.BlockSpec((None, 6, D), lambda i: (_group_of_rows(i * bm), 0, 0)),
                  pl.BlockSpec((1, D), lambda i: (0, 0)),
                  pl.BlockSpec((bm, D), lambda i: (i, 0)),
                  pl.BlockSpec((bm, D), lambda i: (nt + i, 0))],
        out_specs=[pl.BlockSpec((bm, D), lambda i: (jnp.minimum(i, nct - 1), 0)),
                   pl.BlockSpec((bm, D), lambda i: (jnp.maximum(i - nct, 0), 0))],
        out_shape=[jax.ShapeDtypeStruct((N_CTX, D), F32), jax.ShapeDtypeStruct((N_LAT, D), F32)],
        compiler_params=_cparams(("arbitrary",)),
        name="moe_combine",
    )(route, x, mods, norm_final.reshape(1, D), ys, ys)


def _moe(h2, route, x, mods, w_in, w_out, norm_final):
    src_tok, dst_row, tile_expert, tile_active, tile_count = _moe_routing(route)
    ys = _moe_group(h2, src_tok, dst_row, tile_expert, tile_active, tile_count, w_in, w_out)
    return _moe_combine(ys, route, x, mods, norm_final)


def _hgrn_lower_bounds(lb_param):
    sm = jax.nn.softmax(lb_param.astype(F32), axis=0)
    return jnp.cumsum(sm, axis=0) - sm[0:1]


def kernel(x_prompt, x_sample, cache_k, cache_v, state_hgrn, c, c_ctx, w_ada, b_ada, norm_mix, norm_ffn,
           w_in, hgrn_lb, hgrn_norm, pool_w, pool_scale, rpb, w_branch_a, w_branch_b, w_branch_c, w_out,
           ffn_w_in, ffn_w_out, router, moe_w_in, moe_w_out, norm_final):
    x = (x_prompt.reshape(N_CTX, D), x_sample.reshape(N_LAT, D))
    cond8 = jnp.concatenate([c_ctx[None], c, jnp.zeros((5, D), F32)], axis=0)
    mods = _adaln(cond8, w_ada, b_ada)[:, :3].reshape(DEPTH, 3, 6, D)
    lbs = _hgrn_lower_bounds(hgrn_lb)
    ck = cache_k.reshape(N_LAT_B, DEPTH, PAST, WIDTH).astype(BF16)
    cv = cache_v.reshape(N_LAT_B, DEPTH, PAST, WIDTH).astype(BF16)
    w_in16 = w_in.astype(BF16)
    router_pad = jnp.pad(router[0], ((0, 0), (0, 128 - N_EXP)))
    r_hi = router_pad.astype(BF16)
    r_lo = (router_pad - r_hi.astype(F32)).astype(BF16)

    kvs, ss = [], []
    for l in range(DEPTH):
        z, zg, kv, h16 = _inproj(x, norm_mix[l], mods[l], w_in16, l)
        of_c, ob_c, s_ctx = _hgrn(z, zg, lbs[l], None, 0, N_CTX_B, CTX_T)
        of_l, ob_l, _ = _hgrn(z, zg, lbs[l], state_hgrn[:, l], N_CTX, N_LAT_B, LAT_T)
        o_pool = _pool(z, pool_w[l].astype(BF16), pool_scale[l])
        oc_c = _ctx_attn(z)
        oc_l = _na_attn(z, ck, cv, _na_bias_table(rpb[l]), l)
        merged = _merge((of_c, ob_c, oc_c), (of_l, ob_l, oc_l), z, h16, w_in16, l, o_pool, x, mods[l],
                        hgrn_norm[l], norm_ffn[l],
                        w_branch_a[l].astype(BF16), w_branch_b[l].astype(BF16),
                        w_branch_c[l].astype(BF16), w_out[l].astype(BF16),
                        (r_hi, r_lo) if l % 2 == 1 else None)
        if l % 2 == 0:
            x, h2 = merged
            x = _ffn(h2, x, mods[l], ffn_w_in.astype(BF16), ffn_w_out.astype(BF16))
        else:
            x, h2, route = merged
            y_ctx, y_lat = _moe(h2, route, x, mods[l], moe_w_in, moe_w_out, norm_final)
        kvs.append(kv)
        ss.append(s_ctx)

    def cache(lo):
        per_layer = [kv[:, lo:lo + WIDTH].reshape(N_CTX_B, 1, CTX_T, NA_H, NA_D) for kv in kvs]
        return jnp.concatenate(per_layer, axis=1)

    y_prompt = y_ctx.reshape(N_CTX_B, CTX_T, D)
    y_sample = y_lat.reshape(N_LAT_B, LAT_T, D)
    return (y_prompt, y_sample, cache(0), cache(WIDTH), jnp.stack(ss, axis=1))
```

```python
import functools

import numpy as np
import jax
import jax.numpy as jnp
from jax import lax
from jax.experimental import pallas as pl
from jax.experimental.pallas import tpu as pltpu

F32 = jnp.float32
BF16 = jnp.bfloat16

D = 1024
N_CTX_B, CTX_T = 32, 256
N_LAT_B, LAT_T = 2, 4096
N_CTX = N_CTX_B * CTX_T
N_LAT = N_LAT_B * LAT_T
N_TOK = N_CTX + N_LAT
DEPTH = 2
GRID_W = 64
GRID_ROWS = LAT_T // GRID_W
PAST = 512
HG_H, HG_D = 4, 128
WIDTH = 512
NA_H, NA_D = 8, 64
WIN_ROWS, WIN_COLS = 8, 16
POOL_WINDOWS = (2, 4, 8, 16)
IN_COLS = 7680
F_DENSE = 2816
N_EXP = 8
F_EXP = 3584
GATE_CLIP = 30.0
EPS = 1e-6
NEG_BIG = -1e30

COL_AQ, COL_AFF, COL_AFB, COL_AI, COL_AG, COL_BU, COL_CQ, COL_CK, COL_CV = range(9)
GL_COL0 = 9 * WIDTH

ROUTE_I1, ROUTE_I2, ROUTE_W1, ROUTE_W2 = 8, 9, 10, 11

VMEM_LIMIT = 56 * 1024 * 1024


def _cparams(sem):
    return pltpu.CompilerParams(dimension_semantics=sem, vmem_limit_bytes=VMEM_LIMIT)


def _sigmoid(x):
    return 1.0 / (1.0 + jnp.exp(-x))


def _silu(x):
    return x / (1.0 + jnp.exp(-x))


def _group_of_rows(row0):
    return jnp.maximum(row0 - N_CTX + LAT_T, 0) // LAT_T


def _adaln_kernel(c_ref, w_ref, b_ref, o_ref):
    s = _silu(c_ref[...]).astype(BF16)
    o_ref[...] = jnp.dot(s, w_ref[...].astype(BF16), preferred_element_type=F32) + b_ref[...]


def _adaln(cond8, w_ada, b_ada):
    tn = 1536
    return pl.pallas_call(
        _adaln_kernel,
        grid=(DEPTH, 6 * D // tn),
        in_specs=[pl.BlockSpec((8, D), lambda l, j: (0, 0)),
                  pl.BlockSpec((None, D, tn), lambda l, j: (l, 0, j)),
                  pl.BlockSpec((None, 1, tn), lambda l, j: (l, 0, j))],
        out_specs=pl.BlockSpec((None, 8, tn), lambda l, j: (l, 0, j)),
        out_shape=jax.ShapeDtypeStruct((DEPTH, 8, 6 * D), F32),
        compiler_params=_cparams(("parallel", "parallel")),
        name="adaln",
    )(cond8, w_ada, b_ada.reshape(DEPTH, 1, 6 * D))


INPROJ_BM, INPROJ_BN = 1024, 3 * WIDTH


def _stream_rows(x, bm):
    if not isinstance(x, tuple):
        specs = [pl.BlockSpec((bm, D), lambda i, *_: (i, 0))]
        return specs, [x]
    nct = N_CTX // bm
    specs = [pl.BlockSpec((bm, D), lambda i, *_: (jnp.minimum(i, nct - 1), 0)),
             pl.BlockSpec((bm, D), lambda i, *_: (jnp.maximum(i - nct, 0), 0))]
    return specs, list(x)


def _read_stream_rows(x_refs, bm):
    if len(x_refs) == 1:
        return x_refs[0][...]
    return jnp.where(pl.program_id(0) < N_CTX // bm, x_refs[0][...], x_refs[1][...])


def _inproj_kernel(*refs):
    nw_ref, m_ref, w_ref, z_ref, zg_ref, kv_ref, h_ref = refs[-7:]
    x_refs = refs[:-7]
    i = pl.program_id(0)
    j = pl.program_id(1)

    @pl.when(j == 0)
    def _():
        x = _read_stream_rows(x_refs, INPROJ_BM)
        y = x * lax.rsqrt(jnp.mean(x * x, axis=-1, keepdims=True) + EPS) * nw_ref[...]
        h_ref[...] = (y * (1.0 + m_ref[1:2, :]) + m_ref[0:1, :]).astype(BF16)

    acc = jnp.dot(h_ref[...], w_ref[...], preferred_element_type=F32)

    z_ref[...] = acc.astype(BF16)

    @pl.when(j == 0)
    def _():
        zg_ref[...] = acc[:, WIDTH:3 * WIDTH]

    @pl.when((j == 2) & (i < N_CTX // INPROJ_BM))
    def _():
        kv_ref[...] = acc[:, WIDTH:3 * WIDTH]


def _inproj(x, norm_w, mods, w_in16, layer):
    bm, bn = INPROJ_BM, INPROJ_BN
    nct = N_CTX // bm
    x_specs, x_args = _stream_rows(x, bm)
    return pl.pallas_call(
        _inproj_kernel,
        grid=(N_TOK // bm, GL_COL0 // bn),
        in_specs=x_specs + [
            pl.BlockSpec((1, D), lambda i, j: (0, 0)),
            pl.BlockSpec((None, 6, D), lambda i, j: (_group_of_rows(i * bm), 0, 0)),
            pl.BlockSpec((None, D, bn), lambda i, j: (layer, 0, j))],
        out_specs=[pl.BlockSpec((bm, bn), lambda i, j: (i, j)),
                   pl.BlockSpec((bm, 2 * WIDTH), lambda i, j: (i, 0)),
                   pl.BlockSpec((bm, 2 * WIDTH), lambda i, j: (jnp.minimum(i, nct - 1), 0)),
                   pl.BlockSpec((bm, D), lambda i, j: (i, 0))],
        out_shape=[jax.ShapeDtypeStruct((N_TOK, GL_COL0), BF16),
                   jax.ShapeDtypeStruct((N_TOK, 2 * WIDTH), F32),
                   jax.ShapeDtypeStruct((N_CTX, 2 * WIDTH), F32),
                   jax.ShapeDtypeStruct((N_TOK, D), BF16)],
        compiler_params=_cparams(("arbitrary", "arbitrary")),
        name="inproj",
    )(*x_args, norm_w.reshape(1, D), mods, w_in16)


HG_C = 128
HG_LEVELS = (4, 8, 16, 32, 64, 128)
LOG2_E = 1.4426950408889634


def _hgrn_level_ids(reverse):
    t = np.arange(HG_C)[:, None]
    s = np.arange(HG_C)[None, :]
    if reverse:
        t, s = s, t
    lev = np.full((HG_C, HG_C), -1, np.int32)
    lev[(t // 4 == s // 4) & (s <= t)] = 0
    for li, L in enumerate(HG_LEVELS[1:], start=1):
        m = (t // L == s // L) & (t % L >= L // 2) & (s % L < L // 2)
        lev[m] = li
    return lev


def _hgrn_ref_rows(b_scr, d, reverse):
    out = []
    r_lo, r_hi = (2, 6) if reverse else (1, 5)
    sub = lax.broadcasted_iota(jnp.int32, (8, WIDTH), 0)
    pieces = []
    for g in range(HG_C // 8):
        lo = jnp.broadcast_to(b_scr[d, 8 * g + r_lo:8 * g + r_lo + 1, :], (8, WIDTH))
        hi = jnp.broadcast_to(b_scr[d, 8 * g + r_hi:8 * g + r_hi + 1, :], (8, WIDTH))
        pieces.append(jnp.where(sub < 4, lo, hi))
    out.append(jnp.concatenate(pieces, axis=0))
    for L in HG_LEVELS[1:]:
        r = L // 2 - 1 if reverse else L // 2
        pieces = [jnp.broadcast_to(b_scr[d, L * g + r:L * g + r + 1, :], (L, WIDTH))
                  for g in range(HG_C // L)]
        out.append(pieces[0] if len(pieces) == 1 else jnp.concatenate(pieces, axis=0))
    return out


def _hgrn_kernel(*refs, has_init):
    if has_init:
        (qf_ref, ff_ref, vf_ref, qb_ref, fb_ref, vb_ref, lb_ref, levf_ref, levb_ref, s0_ref,
         of_ref, ob_ref, so_ref, st_scr, b_scr) = refs
    else:
        (qf_ref, ff_ref, vf_ref, qb_ref, fb_ref, vb_ref, lb_ref, levf_ref, levb_ref,
         of_ref, ob_ref, so_ref, st_scr, b_scr) = refs
    c = pl.program_id(1)
    C = HG_C

    @pl.when(c == 0)
    def _():
        for d in range(2):
            for h in range(HG_H):
                if has_init:
                    st_scr[d, h] = s0_ref[d, h].T
                else:
                    st_scr[d, h] = jnp.zeros((HG_D, HG_D), F32)

    row = lax.broadcasted_iota(jnp.int32, (C, C), 0)
    col = lax.broadcasted_iota(jnp.int32, (C, C), 1)
    dirs = ((qf_ref, ff_ref, vf_ref, levf_ref, of_ref), (qb_ref, fb_ref, vb_ref, levb_ref, ob_ref))
    for d, (q_ref, f_ref, v_ref, lev_ref, o_ref) in enumerate(dirs):
        reverse = d == 1
        tri = jnp.where((col >= row) if reverse else (col <= row), 1.0, 0.0).astype(BF16)
        q = _silu(q_ref[...].astype(F32))
        fx = jnp.clip(f_ref[...], -GATE_CLIP, GATE_CLIP)
        e = jnp.exp(-fx)
        sig_pos = 1.0 / (1.0 + e)
        sig_neg = e * sig_pos
        lb = lb_ref[d:d + 1, :]
        lf = jnp.log(lb + (1.0 - lb) * sig_pos) * LOG2_E
        k = (1.0 - lb) * sig_neg
        q16 = q.astype(BF16)
        k16 = k.astype(BF16)
        hi = lf.astype(BF16)
        lo = (lf - hi.astype(F32)).astype(BF16)
        b = (jnp.dot(tri, hi, preferred_element_type=F32)
             + jnp.dot(tri, lo, preferred_element_type=F32))
        b_scr[d] = b
        refs_m = _hgrn_ref_rows(b_scr, d, reverse)
        qs, ks = [], []
        for li, m in enumerate(refs_m):
            dlt = b - m
            if li == 0:
                qs.append(q16 * jnp.exp2(dlt).astype(BF16))
                ks.append(k16 * jnp.exp2(-dlt).astype(BF16))
            else:
                fac = jnp.exp2(-jnp.abs(dlt)).astype(BF16)
                qs.append(q16 * fac)
                ks.append(k16 * fac)
        lev = lev_ref[...]
        b_end = b[0:1, :] if reverse else b[C - 1:C, :]
        q_in = q16 * jnp.exp2(b).astype(BF16)
        k_out = k16 * jnp.exp2(b_end - b).astype(BF16)
        dec = jnp.exp2(b_end)
        vb16 = v_ref[...]
        for h in range(HG_H):
            sl = slice(h * HG_D, (h + 1) * HG_D)
            a = jnp.zeros((C, C), F32)
            for li in range(len(HG_LEVELS)):
                p = lax.dot_general(qs[li][:, sl], ks[li][:, sl], (((1,), (1,)), ((), ())),
                                    preferred_element_type=F32)
                a = jnp.where(lev == li, p, a)
            vh = vb16[:, sl]
            st = st_scr[d, h]
            o = (jnp.dot(a.astype(BF16), vh, preferred_element_type=F32)
                 + lax.dot_general(q_in[:, sl], st.astype(BF16), (((1,), (1,)), ((), ())),
                                   preferred_element_type=F32))
            o_ref[:, sl] = o
            upd = lax.dot_general(vh, k_out[:, sl], (((0,), (0,)), ((), ())),
                                  preferred_element_type=F32)
            st_scr[d, h] = st * dec[:, sl] + upd

    @pl.when(c == pl.num_programs(1) - 1)
    def _():
        for d in range(2):
            for h in range(HG_H):
                so_ref[d, h] = st_scr[d, h].T


def _hgrn(z, zg, lb, s0, row_off, n_b, seq_t):
    C = HG_C
    nc = seq_t // C
    base = row_off // C
    has_init = s0 is not None

    def fwd(col):
        return pl.BlockSpec((C, WIDTH), lambda b, c: (base + b * nc + c, col))

    def bwd(col):
        return pl.BlockSpec((C, WIDTH), lambda b, c: (base + b * nc + nc - 1 - c, col))

    full = lambda shape: pl.BlockSpec(shape, lambda b, c: (0,) * len(shape))
    in_specs = [fwd(COL_AQ), fwd(0), fwd(COL_AI), bwd(COL_AQ), bwd(1), bwd(COL_AI),
                full((2, WIDTH)), full((C, C)), full((C, C))]
    args = [z, zg, z, z, zg, z, lb, jnp.asarray(_hgrn_level_ids(False)), jnp.asarray(_hgrn_level_ids(True))]
    if has_init:
        in_specs.append(pl.BlockSpec((None, 2, HG_H, HG_D, HG_D), lambda b, c: (b, 0, 0, 0, 0)))
        args.append(s0)
    return pl.pallas_call(
        functools.partial(_hgrn_kernel, has_init=has_init),
        grid=(n_b, nc),
        in_specs=in_specs,
        out_specs=[pl.BlockSpec((C, WIDTH), lambda b, c: (b * nc + c, 0)),
                   pl.BlockSpec((C, WIDTH), lambda b, c: (b * nc + nc - 1 - c, 0)),
                   pl.BlockSpec((None, 2, HG_H, HG_D, HG_D), lambda b, c: (b, 0, 0, 0, 0))],
        out_shape=[jax.ShapeDtypeStruct((n_b * seq_t, WIDTH), F32),
                   jax.ShapeDtypeStruct((n_b * seq_t, WIDTH), F32),
                   jax.ShapeDtypeStruct((n_b, 2, HG_H, HG_D, HG_D), F32)],
        scratch_shapes=[pltpu.VMEM((2, HG_H, HG_D, HG_D), F32),
                        pltpu.VMEM((2, C, WIDTH), F32)],
        compiler_params=_cparams(("parallel", "arbitrary")),
        name="hgrn_lat" if has_init else "hgrn_ctx",
    )(*args)


POOL_T = 256
POOL_HALO = 16


def _pool_kernel(u_ref, up_ref, un_ref, w_ref, sc_ref, o_ref):
    i = pl.program_id(0)
    is_lat = i >= N_CTX // POOL_T
    seq_t = jnp.where(is_lat, LAT_T, CTX_T)
    t0 = jnp.where(is_lat, ((i - N_CTX // POOL_T) % (LAT_T // POOL_T)) * POOL_T, 0)
    TT, HL = POOL_T, POOL_HALO
    diff = lax.broadcasted_iota(jnp.int32, (TT, TT), 1) - lax.broadcasted_iota(jnp.int32, (TT, TT), 0)
    diff_h = lax.broadcasted_iota(jnp.int32, (TT, HL), 1) - lax.broadcasted_iota(jnp.int32, (TT, HL), 0)
    t_glob = t0 + lax.broadcasted_iota(jnp.int32, (TT, 128), 0)
    has_prev = t0 > 0
    has_next = t0 + TT < seq_t
    for g, w in enumerate(POOL_WINDOWS):
        half = w // 2
        sl = slice(g * 128, (g + 1) * 128)
        band = jnp.where((diff >= -half) & (diff < half), 1.0, 0.0).astype(BF16)
        dp = diff_h - HL
        band_p = jnp.where((dp >= -half) & (dp < half) & has_prev, 1.0, 0.0).astype(BF16)
        dn = diff_h + TT
        band_n = jnp.where((dn >= -half) & (dn < half) & has_next, 1.0, 0.0).astype(BF16)
        u = u_ref[:, sl]
        s = (jnp.dot(band, u, preferred_element_type=F32)
             + jnp.dot(band_p, up_ref[:, sl], preferred_element_type=F32)
             + jnp.dot(band_n, un_ref[:, sl], preferred_element_type=F32))
        cnt = (jnp.minimum(t_glob + half, seq_t) - jnp.maximum(t_glob - half, 0)).astype(F32)
        dd = s / cnt - u.astype(F32)
        y = jnp.dot(dd.astype(BF16), w_ref[g], preferred_element_type=F32)
        o_ref[:, sl] = y * sc_ref[:, sl]


def _pool(z, pool_w16, pool_scale):
    nt = N_TOK // POOL_T
    per = POOL_T // POOL_HALO
    return pl.pallas_call(
        _pool_kernel,
        grid=(nt,),
        in_specs=[pl.BlockSpec((POOL_T, WIDTH), lambda i: (i, COL_BU)),
                  pl.BlockSpec((POOL_HALO, WIDTH), lambda i: (jnp.maximum(i * per - 1, 0), COL_BU)),
                  pl.BlockSpec((POOL_HALO, WIDTH), lambda i: (jnp.minimum((i + 1) * per, nt * per - 1), COL_BU)),
                  pl.BlockSpec((4, 128, 128), lambda i: (0, 0, 0)),
                  pl.BlockSpec((1, WIDTH), lambda i: (0, 0))],
        out_specs=pl.BlockSpec((POOL_T, WIDTH), lambda i: (i, 0)),
        out_shape=jax.ShapeDtypeStruct((N_TOK, WIDTH), F32),
        compiler_params=_cparams(("parallel",)),
        name="pool",
    )(z, z, z, pool_w16, pool_scale.reshape(1, WIDTH))


def _head_mask(hh):
    lane = lax.broadcasted_iota(jnp.int32, (1, 128), 1)
    in_head = (lane >= hh * NA_D) & (lane < (hh + 1) * NA_D)
    return jnp.where(in_head, NA_D ** -0.5, 0.0).astype(BF16)


def _ctx_attn_kernel(q_ref, k_ref, v_ref, o_ref):
    lane = lax.broadcasted_iota(jnp.int32, (CTX_T, 128), 1)
    for j in range(NA_H // 2):
        sl = slice(j * 128, (j + 1) * 128)
        q = q_ref[:, sl]
        kt = k_ref[:, sl]
        vt = v_ref[:, sl]
        outs = []
        for hh in range(2):
            s = lax.dot_general(q * _head_mask(hh), kt, (((1,), (1,)), ((), ())), preferred_element_type=F32)
            m = jnp.max(s, axis=-1, keepdims=True)
            p = jnp.exp(s - m)
            l = jnp.sum(p, axis=-1, keepdims=True)
            outs.append(jnp.dot(p.astype(BF16), vt, preferred_element_type=F32) / l)
        o_ref[:, sl] = jnp.where(lane < NA_D, outs[0], outs[1])


def _ctx_attn(z):
    spec = lambda col: pl.BlockSpec((CTX_T, WIDTH), lambda b: (b, col))
    return pl.pallas_call(
        _ctx_attn_kernel,
        grid=(N_CTX_B,),
        in_specs=[spec(COL_CQ), spec(COL_CK), spec(COL_CV)],
        out_specs=pl.BlockSpec((CTX_T, WIDTH), lambda b: (b, 0)),
        out_shape=jax.ShapeDtypeStruct((N_CTX, WIDTH), F32),
        compiler_params=_cparams(("parallel",)),
        name="ctx_attn",
    )(z, z, z)


NA_QR = 4
NA_KR = 12
NA_NQ = NA_QR * GRID_W
NA_NK = NA_KR * GRID_W
NA_BLOCKS = GRID_ROWS // NA_QR


def _na_key_row0(blk):
    return jnp.clip(blk * NA_QR - WIN_ROWS // 2, 0, GRID_ROWS - NA_KR)


def _na_geometry():
    patterns, var_of_block = [], []
    for blk in range(NA_BLOCKS):
        r = blk * NA_QR + np.arange(NA_QR)[:, None]
        kr = int(np.clip(blk * NA_QR - WIN_ROWS // 2, 0, GRID_ROWS - NA_KR)) + np.arange(NA_KR)[None, :]
        rs = np.clip(r - WIN_ROWS // 2, 0, GRID_ROWS - WIN_ROWS)
        valid = (kr >= rs) & (kr < rs + WIN_ROWS)
        assert (valid.sum(axis=1) == WIN_ROWS).all(), "key rows must cover every query row's window"
        drow = np.where(valid, kr - r + WIN_ROWS - 1, 0)
        key = (drow.tobytes(), valid.tobytes())
        ids = [i for i, (k_, _, _) in enumerate(patterns) if k_ == key]
        if not ids:
            patterns.append((key, drow, valid))
            ids = [len(patterns) - 1]
        var_of_block.append(ids[0])
    drow = np.stack([p[1] for p in patterns])
    valid = np.stack([p[2] for p in patterns])
    return np.asarray(var_of_block, np.int32), drow, valid


NA_DROWS = 2 * WIN_ROWS - 1


def _na_bias_kernel(didx_ref, t2_ref, o_ref):
    v = pl.program_id(0)
    for qr in range(NA_QR):
        for kp in range(NA_KR // 2):
            base = (v * NA_QR + qr) * NA_KR + 2 * kp
            pair = jnp.concatenate([t2_ref[didx_ref[base]], t2_ref[didx_ref[base + 1]]], axis=1)
            o_ref[qr * GRID_W:(qr + 1) * GRID_W, kp * 128:(kp + 1) * 128] = pair


def _na_bias_table(rpb_l):
    _, drow, valid = _na_geometry()
    n_var = valid.shape[0]
    qc = np.arange(GRID_W)[:, None]
    kc = np.arange(GRID_W)[None, :]
    q0 = np.clip(qc - WIN_COLS // 2, 0, GRID_W - WIN_COLS)
    col_in = (kc >= q0) & (kc < q0 + WIN_COLS)
    dcol = np.clip(kc - qc, -(WIN_COLS - 1), WIN_COLS - 1) + WIN_COLS - 1
    oh_col = (dcol[None] == np.arange(2 * WIN_COLS - 1)[:, None, None]).astype(np.float32)
    t2 = jnp.einsum('hdc,cqk->hdqk', rpb_l, jnp.asarray(oh_col), precision=lax.Precision.HIGHEST)
    t2 = jnp.where(col_in[None, None], t2, NEG_BIG)
    t2 = jnp.concatenate([t2, jnp.full((NA_H, 1, GRID_W, GRID_W), NEG_BIG, F32)], axis=1)
    didx = np.where(valid, drow, NA_DROWS).astype(np.int32).reshape(-1)
    grid_spec = pltpu.PrefetchScalarGridSpec(
        num_scalar_prefetch=1,
        grid=(n_var, NA_H),
        in_specs=[pl.BlockSpec((None, NA_DROWS + 1, GRID_W, GRID_W), lambda v, h, didx: (h, 0, 0, 0))],
        out_specs=pl.BlockSpec((None, None, NA_NQ, NA_NK), lambda v, h, didx: (v, h, 0, 0)))
    return pl.pallas_call(
        _na_bias_kernel,
        grid_spec=grid_spec,
        out_shape=jax.ShapeDtypeStruct((n_var, NA_H, NA_NQ, NA_NK), F32),
        compiler_params=_cparams(("parallel", "parallel")),
        name="na_bias",
    )(jnp.asarray(didx), t2)


def _na_attn_kernel(var_ref, q_ref, k_ref, v_ref, kc_ref, vc_ref, tbl_ref, o_ref):
    del var_ref
    blk = pl.program_id(1)
    k0 = pl.multiple_of(_na_key_row0(blk) * GRID_W, GRID_W)
    lane = lax.broadcasted_iota(jnp.int32, (NA_NQ, 128), 1)
    for j in range(NA_H // 2):
        sl = slice(j * 128, (j + 1) * 128)
        q = q_ref[:, sl]
        kt = k_ref[pl.ds(k0, NA_NK), sl]
        vt = v_ref[pl.ds(k0, NA_NK), sl]
        kct = kc_ref[:, sl]
        vct = vc_ref[:, sl]
        outs = []
        for hh in range(2):
            qm = q * _head_mask(hh)
            s_loc = lax.dot_general(qm, kt, (((1,), (1,)), ((), ())), preferred_element_type=F32)
            s_ctx = lax.dot_general(qm, kct, (((1,), (1,)), ((), ())), preferred_element_type=F32)
            tb = tbl_ref[2 * j + hh]
            s_loc = jnp.where(tb > 0.5 * NEG_BIG, s_loc + tb, NEG_BIG)
            m = jnp.maximum(jnp.max(s_loc, axis=-1, keepdims=True), jnp.max(s_ctx, axis=-1, keepdims=True))
            p_loc = jnp.exp(s_loc - m)
            p_ctx = jnp.exp(s_ctx - m)
            l = jnp.sum(p_loc, axis=-1, keepdims=True) + jnp.sum(p_ctx, axis=-1, keepdims=True)
            o = (jnp.dot(p_loc.astype(BF16), vt, preferred_element_type=F32)
                 + jnp.dot(p_ctx.astype(BF16), vct, preferred_element_type=F32))
            outs.append(o / l)
        o_ref[:, sl] = jnp.where(lane < NA_D, outs[0], outs[1])


def _na_attn(z, cache_k16, cache_v16, tbl, layer):
    base_q = N_CTX // NA_NQ
    base_t = N_CTX // LAT_T
    ctx_spec = pl.BlockSpec((None, None, PAST, WIDTH), lambda b, r, var: (b, layer, 0, 0))
    grid_spec = pltpu.PrefetchScalarGridSpec(
        num_scalar_prefetch=1,
        grid=(N_LAT_B, NA_BLOCKS),
        in_specs=[pl.BlockSpec((NA_NQ, WIDTH), lambda b, r, var: (base_q + b * NA_BLOCKS + r, COL_CQ)),
                  pl.BlockSpec((LAT_T, WIDTH), lambda b, r, var: (base_t + b, COL_CK)),
                  pl.BlockSpec((LAT_T, WIDTH), lambda b, r, var: (base_t + b, COL_CV)),
                  ctx_spec, ctx_spec,
                  pl.BlockSpec((None, NA_H, NA_NQ, NA_NK), lambda b, r, var: (var[r], 0, 0, 0))],
        out_specs=pl.BlockSpec((NA_NQ, WIDTH), lambda b, r, var: (b * NA_BLOCKS + r, 0)))
    return pl.pallas_call(
        _na_attn_kernel,
        grid_spec=grid_spec,
        out_shape=jax.ShapeDtypeStruct((N_LAT, WIDTH), F32),
        compiler_params=_cparams(("parallel", "arbitrary")),
        name="na_attn",
    )(jnp.asarray(_na_geometry()[0]), z, z, z, cache_k16, cache_v16, tbl)


MERGE_BM = 512
MERGE_SLABS = 2


def _merge_kernel(*refs, with_router, n_x):
    x_refs, refs = refs[:n_x], refs[n_x:]
    (ofc_ref, obc_ref, occ_ref, ofl_ref, obl_ref, ocl_ref, ag_ref, op_ref, h16_ref, wg0_ref, wg1_ref, m_ref,
     hn_ref, nf_ref, wa_ref, wb_ref, wc_ref, wo_ref) = refs[:18]
    if with_router:
        rhi_ref, rlo_ref, xn_ref, h2_ref, route_ref = refs[18:]
    else:
        xn_ref, h2_ref = refs[18:]
    is_ctx = pl.program_id(0) < N_CTX // MERGE_BM
    for sl in range(MERGE_SLABS):
        rs = slice(sl * MERGE_BM // MERGE_SLABS, (sl + 1) * MERGE_BM // MERGE_SLABS)
        o = jnp.where(is_ctx, ofc_ref[rs, :] + obc_ref[rs, :], ofl_ref[rs, :] + obl_ref[rs, :])
        oc = jnp.where(is_ctx, occ_ref[rs, :], ocl_ref[rs, :])
        parts = []
        for h in range(HG_H):
            oh = o[:, h * HG_D:(h + 1) * HG_D]
            parts.append(oh * lax.rsqrt(jnp.mean(oh * oh, axis=-1, keepdims=True) + EPS))
        oa = jnp.concatenate(parts, axis=1) * hn_ref[...] * _silu(ag_ref[rs, :].astype(F32))
        h16 = h16_ref[rs, :]
        sg0 = _sigmoid(jnp.dot(h16, wg0_ref[...], preferred_element_type=F32).astype(BF16)).astype(F32)
        sg1 = _sigmoid(jnp.dot(h16, wg1_ref[...], preferred_element_type=F32).astype(BF16)).astype(F32)
        sga = sg0[:, :D]
        sgb = jnp.concatenate([sg0[:, D:], sg1[:, :WIDTH]], axis=1)
        sgc = sg1[:, WIDTH:]
        mix = (sga * jnp.dot(oa.astype(BF16), wa_ref[...], preferred_element_type=F32)
               + sgb * jnp.dot(op_ref[rs, :].astype(BF16), wb_ref[...], preferred_element_type=F32)
               + sgc * jnp.dot(oc.astype(BF16), wc_ref[...], preferred_element_type=F32))
        x = x_refs[0][rs, :] if len(x_refs) == 1 else jnp.where(is_ctx, x_refs[0][rs, :], x_refs[1][rs, :])
        xn = x + m_ref[2:3, :] * jnp.dot(mix.astype(BF16), wo_ref[...], preferred_element_type=F32)
        xn_ref[rs, :] = xn
        y = xn * lax.rsqrt(jnp.mean(xn * xn, axis=-1, keepdims=True) + EPS) * nf_ref[...]
        h2 = y * (1.0 + m_ref[4:5, :]) + m_ref[3:4, :]
        h2_ref[rs, :] = h2.astype(h2_ref.dtype)
        if not with_router:
            continue
        hhi = h2.astype(BF16)
        hlo = (h2 - hhi.astype(F32)).astype(BF16)
        logits = (jnp.dot(hhi, rhi_ref[...], preferred_element_type=F32)
                  + jnp.dot(hhi, rlo_ref[...], preferred_element_type=F32)
                  + jnp.dot(hlo, rhi_ref[...], preferred_element_type=F32))
        lane = lax.broadcasted_iota(jnp.int32, logits.shape, 1).astype(F32)
        lg = jnp.where(lane < N_EXP, logits, -jnp.inf)
        m1 = jnp.max(lg, axis=-1, keepdims=True)
        i1 = jnp.min(jnp.where(lg == m1, lane, 128.0), axis=-1, keepdims=True)
        lg2 = jnp.where(lane == i1, -jnp.inf, lg)
        m2 = jnp.max(lg2, axis=-1, keepdims=True)
        i2 = jnp.min(jnp.where(lg2 == m2, lane, 128.0), axis=-1, keepdims=True)
        e = jnp.exp(m2 - m1)
        w1 = 1.0 / (1.0 + e)
        route_ref[rs, :] = (jnp.where(lane == ROUTE_I1, i1, 0.0) + jnp.where(lane == ROUTE_I2, i2, 0.0)
                            + jnp.where(lane == ROUTE_W1, w1, 0.0) + jnp.where(lane == ROUTE_W2, e * w1, 0.0))


def _merge(ctx_parts, lat_parts, z, h16, w_in16, layer, o_pool, x, mods, hgrn_norm_l, norm_ffn_l, wa, wb, wc, wo,
           router_split):
    bm = MERGE_BM
    glw = 1536
    nct = N_CTX // bm
    with_router = router_split is not None
    row = lambda w, col=0: pl.BlockSpec((bm, w), lambda i: (i, col))
    ctx_row = pl.BlockSpec((bm, WIDTH), lambda i: (jnp.minimum(i, nct - 1), 0))
    lat_row = pl.BlockSpec((bm, WIDTH), lambda i: (jnp.maximum(i - nct, 0), 0))
    const = lambda shape: pl.BlockSpec(shape, lambda i: (0,) * len(shape))
    x_specs, x_args = _stream_rows(x, bm)
    gate_w = lambda t: pl.BlockSpec((None, D, glw), lambda i: (layer, 0, GL_COL0 // glw + t))
    in_specs = x_specs + [ctx_row] * 3 + [lat_row] * 3 + [
        row(WIDTH, COL_AG), row(WIDTH),
        row(D), gate_w(0), gate_w(1),
        pl.BlockSpec((None, 6, D), lambda i: (_group_of_rows(i * bm), 0, 0)),
        const((1, WIDTH)), const((1, D)),
        const((WIDTH, D)), const((WIDTH, D)), const((WIDTH, D)), const((D, D))]
    args = x_args + list(ctx_parts) + list(lat_parts) + [
        z, o_pool, h16, w_in16, w_in16, mods, jnp.tile(hgrn_norm_l, HG_H).reshape(1, WIDTH),
        norm_ffn_l.reshape(1, D), wa, wb, wc, wo]
    out_specs = [row(D), row(D)]
    out_shape = [jax.ShapeDtypeStruct((N_TOK, D), F32),
                 jax.ShapeDtypeStruct((N_TOK, D), F32 if with_router else BF16)]
    if with_router:
        in_specs += [const((D, 128)), const((D, 128))]
        args += list(router_split)
        out_specs.append(row(128))
        out_shape.append(jax.ShapeDtypeStruct((N_TOK, 128), F32))
    return pl.pallas_call(
        functools.partial(_merge_kernel, with_router=with_router, n_x=len(x_args)),
        grid=(N_TOK // bm,),
        in_specs=in_specs,
        out_specs=out_specs,
        out_shape=out_shape,
        compiler_params=_cparams(("parallel",)),
        name="merge_route" if with_router else "merge",
    )(*args)


def _ffn_kernel(h_ref, x_ref, m_ref, wa_ref, wb_ref, wo_ref, o_ref):
    h = h_ref[...]
    a = jnp.dot(h, wa_ref[...], preferred_element_type=F32)
    b = jnp.dot(h, wb_ref[...], preferred_element_type=F32)
    g = (_silu(a) * b).astype(BF16)
    o_ref[...] = x_ref[...] + m_ref[5:6, :] * jnp.dot(g, wo_ref[...], preferred_element_type=F32)


FFN_BM = 512


def _ffn(h2, x, mods, w_in, w_out):
    bm = FFN_BM
    resident = pl.Buffered(1)
    return pl.pallas_call(
        _ffn_kernel,
        grid=(N_TOK // bm,),
        in_specs=[pl.BlockSpec((bm, D), lambda i: (i, 0)),
                  pl.BlockSpec((bm, D), lambda i: (i, 0)),
                  pl.BlockSpec((None, 6, D), lambda i: (_group_of_rows(i * bm), 0, 0)),
                  pl.BlockSpec((None, D, F_DENSE), lambda i: (0, 0, 0), pipeline_mode=resident),
                  pl.BlockSpec((None, D, F_DENSE), lambda i: (0, 0, 1), pipeline_mode=resident),
                  pl.BlockSpec((None, F_DENSE, D), lambda i: (0, 0, 0), pipeline_mode=resident)],
        out_specs=pl.BlockSpec((bm, D), lambda i: (i, 0)),
        out_shape=jax.ShapeDtypeStruct((N_TOK, D), F32),
        compiler_params=_cparams(("parallel",)),
        name="ffn",
    )(h2, x, mods, w_in, w_in, w_out)


MOE_BM = 1024
MOE_TF = 512
MOE_NF = F_EXP // MOE_TF
MOE_PAIRS = 2 * N_TOK
MOE_ROWS = MOE_PAIRS + N_EXP * MOE_BM
MOE_TILES = MOE_ROWS // MOE_BM
MOE_CH = -(-MOE_BM // MOE_NF)
MOE_MOVED = MOE_CH * MOE_NF
MOE_BUF = -(-MOE_MOVED // 8) * 8
ROW_DMA_PRIORITY = 1


def _moe_routing(route):
    i1 = route[:, ROUTE_I1].astype(jnp.int32)
    i2 = route[:, ROUTE_I2].astype(jnp.int32)
    ep = jnp.stack([i1, i2], axis=1).reshape(-1)
    onehot = (ep[:, None] == jnp.arange(N_EXP, dtype=jnp.int32)[None, :]).astype(jnp.int32)
    counts = jnp.sum(onehot, axis=0)
    padded = ((counts + MOE_BM - 1) // MOE_BM) * MOE_BM
    ends = jnp.cumsum(padded)
    starts = ends - padded
    first = jnp.cumsum(counts) - counts
    order = jnp.sort(ep * MOE_PAIRS + jnp.arange(MOE_PAIRS, dtype=jnp.int32)) & (MOE_PAIRS - 1)
    tile_row0 = jnp.arange(MOE_TILES, dtype=jnp.int32) * MOE_BM
    tile_active = (tile_row0 < ends[-1]).astype(jnp.int32)
    last_row0 = jnp.maximum(ends[-1] - MOE_BM, 0)
    tile_expert = jnp.sum((jnp.minimum(tile_row0, last_row0)[:, None] >= ends[None, :]).astype(jnp.int32), axis=1)
    tile_expert = jnp.minimum(tile_expert, N_EXP - 1)
    row = jnp.arange(MOE_ROWS, dtype=jnp.int32)
    row_e = jnp.repeat(tile_expert, MOE_BM)
    rank = row - starts[row_e]
    valid = (rank < counts[row_e]) & (row < ends[-1])
    pair = jnp.where(valid, order[jnp.clip(first[row_e] + rank, 0, MOE_PAIRS - 1)], -1)
    pair = pair.reshape(MOE_TILES, MOE_BM)
    tile_count = jnp.sum((pair >= 0).astype(jnp.int32), axis=1)
    src_tok = jnp.pad(jnp.maximum(pair, 0) >> 1, ((0, 0), (0, MOE_BUF - MOE_BM)))
    dst_row = jnp.where(pair >= 0, (pair & 1) * N_TOK + (pair >> 1), -1)
    dst_row = jnp.pad(dst_row, ((1, 0), (0, MOE_BUF - MOE_BM)), constant_values=-1)
    return (src_tok.reshape(MOE_TILES, 1, MOE_BUF), dst_row.reshape(MOE_TILES + 1, 1, MOE_BUF),
            tile_expert, tile_active, tile_count)


def _row(ref, r):
    return ref.at[pl.ds(r, 1), :]


def _moe_group_kernel(te_ref, ta_ref, tc_ref, dprv_ref, dcur_ref, gcur_ref, gnxt_ref, h2_ref, wa_ref, wb_ref, wo_ref,
                      out_ref, xbuf, ybuf, h_scr, gsem, ssem):
    del te_ref
    i = pl.program_id(0)
    f = pl.program_id(1)
    nt = pl.num_programs(0)
    s = i % 2
    o = 1 - s
    active = ta_ref[i] == 1
    prv_active = (i >= 1) & (ta_ref[jnp.maximum(i - 1, 0)] == 1)

    def gather_row(idx_ref, k, slot):
        pltpu.make_async_copy(_row(h2_ref, idx_ref[0, k]), _row(xbuf.at[slot], k),
                              gsem.at[slot]).start(priority=ROW_DMA_PRIORITY)

    def scatter_row(idx_ref, k, slot):
        dst = idx_ref[0, k]

        @pl.when(dst >= 0)
        def _():
            pltpu.make_async_copy(_row(ybuf.at[slot], k), _row(out_ref, dst),
                                  ssem.at[slot]).start(priority=ROW_DMA_PRIORITY)

    def wait_rows(sem, n):
        n = jnp.asarray(n, jnp.int32)
        n8 = pl.multiple_of((n >> 3) << 3, 8)

        @pl.when(n8 > 0)
        def _():
            pltpu.make_async_copy(xbuf.at[0, pl.ds(0, n8), :], ybuf.at[0, pl.ds(0, n8), :], sem).wait()

        def one(k, carry):
            pltpu.make_async_copy(_row(xbuf.at[0], 0), _row(ybuf.at[0], 0), sem).wait()
            return carry
        lax.fori_loop(0, n - n8, one, 0)

    def rows_loop(fn, n):
        def body(k, carry):
            fn(k)
            return carry
        lax.fori_loop(0, n, body, 0, unroll=8)

    @pl.when((i == 0) & (f == 0))
    def _():
        rows_loop(lambda k: gather_row(gcur_ref, k, 0), MOE_MOVED)

    @pl.when((f == 0) & (i >= 2))
    def _():
        wait_rows(ssem.at[s], tc_ref[jnp.maximum(i - 2, 0)])

    @pl.when((f == 0) & ((i == 0) | prv_active))
    def _():
        wait_rows(gsem.at[s], MOE_MOVED)

    @pl.when((f == 0) & active)
    def _():
        h_scr[...] = xbuf[s, 0:MOE_BM, :].astype(BF16)
        ybuf[s, 0:MOE_BM, :] = jnp.zeros((MOE_BM, D), F32)

    def step(rows):
        for u in range(MOE_CH):
            gather_row(gnxt_ref, f * MOE_CH + u, o)
        for u in range(MOE_CH):
            scatter_row(dprv_ref, f * MOE_CH + u, o)
        h = h_scr[0:rows, :]
        a = jnp.dot(h, wa_ref[...].astype(BF16), preferred_element_type=F32)
        b = jnp.dot(h, wb_ref[...].astype(BF16), preferred_element_type=F32)
        g = (_silu(a) * b).astype(BF16)
        ybuf[s, 0:rows, :] += jnp.dot(g, wo_ref[...].astype(BF16), preferred_element_type=F32)

    few = tc_ref[i] <= MOE_BM // 2

    @pl.when(active & jnp.logical_not(few))
    def _():
        step(MOE_BM)

    @pl.when(active & few)
    def _():
        step(MOE_BM // 2)

    @pl.when(jnp.logical_not(active) & prv_active)
    def _():
        rows_loop(lambda u: scatter_row(dprv_ref, f * MOE_CH + u, o), MOE_CH)

    @pl.when((i == nt - 1) & (f == pl.num_programs(1) - 1))
    def _():
        wait_rows(ssem.at[o], tc_ref[jnp.maximum(i - 1, 0)])
        rows_loop(lambda k: scatter_row(dcur_ref, k, s), MOE_BM)
        wait_rows(ssem.at[s], tc_ref[i])

        @pl.when(active)
        def _():
            wait_rows(gsem.at[o], MOE_MOVED)


def _moe_group(h2, src_tok, dst_row, tile_expert, tile_active, tile_count, w_in, w_out):
    nf = MOE_NF
    last = MOE_TILES - 1

    def fblk(i, f, ta):
        return jnp.where(ta[i] == 1, f, nf - 1)

    idx_spec = lambda off, hi: pl.BlockSpec((None, 1, MOE_BUF),
                                            lambda i, f, te, ta, tc: (jnp.minimum(i + off, hi), 0, 0),
                                            memory_space=pltpu.SMEM)
    grid_spec = pltpu.PrefetchScalarGridSpec(
        num_scalar_prefetch=3,
        grid=(MOE_TILES, nf),
        in_specs=[idx_spec(0, last + 1), idx_spec(1, last + 1),
                  idx_spec(0, last), idx_spec(1, last),
                  pl.BlockSpec(memory_space=pl.ANY),
                  pl.BlockSpec((None, None, D, MOE_TF), lambda i, f, te, ta, tc: (0, te[i], 0, fblk(i, f, ta))),
                  pl.BlockSpec((None, None, D, MOE_TF), lambda i, f, te, ta, tc: (0, te[i], 0, nf + fblk(i, f, ta))),
                  pl.BlockSpec((None, None, MOE_TF, D), lambda i, f, te, ta, tc: (0, te[i], fblk(i, f, ta), 0))],
        out_specs=pl.BlockSpec(memory_space=pl.ANY),
        scratch_shapes=[pltpu.VMEM((2, MOE_BUF, D), F32), pltpu.VMEM((2, MOE_BUF, D), F32),
                        pltpu.VMEM((MOE_BM, D), BF16),
                        pltpu.SemaphoreType.DMA((2,)), pltpu.SemaphoreType.DMA((2,))])
    return pl.pallas_call(
        _moe_group_kernel,
        grid_spec=grid_spec,
        out_shape=jax.ShapeDtypeStruct((MOE_PAIRS, D), F32),
        compiler_params=_cparams(("arbitrary", "arbitrary")),
        name="moe_group",
    )(tile_expert, tile_active, tile_count, dst_row, dst_row, src_tok, src_tok, h2, w_in, w_in, w_out)


MOE_COMBINE_BM = 512


def _moe_combine_kernel(route_ref, x_ref, m_ref, nf_ref, y1_ref, y2_ref, outc_ref, outl_ref):
    route = route_ref[...]
    f = (y1_ref[...] * route[:, ROUTE_W1:ROUTE_W1 + 1]
         + y2_ref[...] * route[:, ROUTE_W2:ROUTE_W2 + 1])
    xn = x_ref[...] + m_ref[5:6, :] * f
    y = xn * lax.rsqrt(jnp.mean(xn * xn, axis=-1, keepdims=True) + EPS) * nf_ref[...]
    is_ctx = pl.program_id(0) < N_CTX // MOE_COMBINE_BM

    @pl.when(is_ctx)
    def _():
        outc_ref[...] = y

    @pl.when(jnp.logical_not(is_ctx))
    def _():
        outl_ref[...] = y


def _moe_combine(ys, route, x, mods, norm_final):
    bm = MOE_COMBINE_BM
    nt = N_TOK // bm
    nct = N_CTX // bm
    return pl.pallas_call(
        _moe_combine_kernel,
        grid=(nt,),
        in_specs=[pl.BlockSpec((bm, 128), lambda i: (i, 0)),
                  pl.BlockSpec((bm, D), lambda i: (i, 0)),
                  pl.BlockSpec((None, 6, D), lambda i: (_group_of_rows(i * bm), 0, 0)),
                  pl.BlockSpec((1, D), lambda i: (0, 0)),
                  pl.BlockSpec((bm, D), lambda i: (i, 0)),
                  pl.BlockSpec((bm, D), lambda i: (nt + i, 0))],
        out_specs=[pl.BlockSpec((bm, D), lambda i: (jnp.minimum(i, nct - 1), 0)),
                   pl.BlockSpec((bm, D), lambda i: (jnp.maximum(i - nct, 0), 0))],
        out_shape=[jax.ShapeDtypeStruct((N_CTX, D), F32), jax.ShapeDtypeStruct((N_LAT, D), F32)],
        compiler_params=_cparams(("arbitrary",)),
        name="moe_combine",
    )(route, x, mods, norm_final.reshape(1, D), ys, ys)


def _moe(h2, route, x, mods, w_in, w_out, norm_final):
    src_tok, dst_row, tile_expert, tile_active, tile_count = _moe_routing(route)
    ys = _moe_group(h2, src_tok, dst_row, tile_expert, tile_active, tile_count, w_in, w_out)
    return _moe_combine(ys, route, x, mods, norm_final)


def _hgrn_lower_bounds(lb_param):
    sm = jax.nn.softmax(lb_param.astype(F32), axis=0)
    return jnp.cumsum(sm, axis=0) - sm[0:1]


def kernel(x_prompt, x_sample, cache_k, cache_v, state_hgrn, c, c_ctx, w_ada, b_ada, norm_mix, norm_ffn,
           w_in, hgrn_lb, hgrn_norm, pool_w, pool_scale, rpb, w_branch_a, w_branch_b, w_branch_c, w_out,
           ffn_w_in, ffn_w_out, router, moe_w_in, moe_w_out, norm_final):
    x = (x_prompt.reshape(N_CTX, D), x_sample.reshape(N_LAT, D))
    cond8 = jnp.concatenate([c_ctx[None], c, jnp.zeros((5, D), F32)], axis=0)
    mods = _adaln(cond8, w_ada, b_ada)[:, :3].reshape(DEPTH, 3, 6, D)
    lbs = _hgrn_lower_bounds(hgrn_lb)
    ck = cache_k.reshape(N_LAT_B, DEPTH, PAST, WIDTH).astype(BF16)
    cv = cache_v.reshape(N_LAT_B, DEPTH, PAST, WIDTH).astype(BF16)
    w_in16 = w_in.astype(BF16)
    router_pad = jnp.pad(router[0], ((0, 0), (0, 128 - N_EXP)))
    r_hi = router_pad.astype(BF16)
    r_lo = (router_pad - r_hi.astype(F32)).astype(BF16)

    kvs, ss = [], []
    for l in range(DEPTH):
        z, zg, kv, h16 = _inproj(x, norm_mix[l], mods[l], w_in16, l)
        of_c, ob_c, s_ctx = _hgrn(z, zg, lbs[l], None, 0, N_CTX_B, CTX_T)
        of_l, ob_l, _ = _hgrn(z, zg, lbs[l], state_hgrn[:, l], N_CTX, N_LAT_B, LAT_T)
        o_pool = _pool(z, pool_w[l].astype(BF16), pool_scale[l])
        oc_c = _ctx_attn(z)
        oc_l = _na_attn(z, ck, cv, _na_bias_table(rpb[l]), l)
        merged = _merge((of_c, ob_c, oc_c), (of_l, ob_l, oc_l), z, h16, w_in16, l, o_pool, x, mods[l],
                        hgrn_norm[l], norm_ffn[l],
                        w_branch_a[l].astype(BF16), w_branch_b[l].astype(BF16),
                        w_branch_c[l].astype(BF16), w_out[l].astype(BF16),
                        (r_hi, r_lo) if l % 2 == 1 else None)
        if l % 2 == 0:
            x, h2 = merged
            x = _ffn(h2, x, mods[l], ffn_w_in.astype(BF16), ffn_w_out.astype(BF16))
        else:
            x, h2, route = merged
            y_ctx, y_lat = _moe(h2, route, x, mods[l], moe_w_in, moe_w_out, norm_final)
        kvs.append(kv)
        ss.append(s_ctx)

    def cache(lo):
        per_layer = [kv[:, lo:lo + WIDTH].reshape(N_CTX_B, 1, CTX_T, NA_H, NA_D) for kv in kvs]
        return jnp.concatenate(per_layer, axis=1)

    y_prompt = y_ctx.reshape(N_CTX_B, CTX_T, D)
    y_sample = y_lat.reshape(N_LAT_B, LAT_T, D)
    return (y_prompt, y_sample, cache(0), cache(WIDTH), jnp.stack(ss, axis=1))
```

```python
import functools

import numpy as np
import jax
import jax.numpy as jnp
from jax import lax
from jax.experimental import pallas as pl
from jax.experimental.pallas import tpu as pltpu

F32 = jnp.float32
BF16 = jnp.bfloat16

D = 1024
N_CTX_B, CTX_T = 32, 256
N_LAT_B, LAT_T = 2, 4096
N_CTX = N_CTX_B * CTX_T
N_LAT = N_LAT_B * LAT_T
N_TOK = N_CTX + N_LAT
DEPTH = 2
GRID_W = 64
GRID_ROWS = LAT_T // GRID_W
PAST = 512
HG_H, HG_D = 4, 128
WIDTH = 512
NA_H, NA_D = 8, 64
WIN_ROWS, WIN_COLS = 8, 16
POOL_WINDOWS = (2, 4, 8, 16)
IN_COLS = 7680
F_DENSE = 2816
N_EXP = 8
F_EXP = 3584
GATE_CLIP = 30.0
EPS = 1e-6
NEG_BIG = -1e30

COL_AQ, COL_AFF, COL_AFB, COL_AI, COL_AG, COL_BU, COL_CQ, COL_CK, COL_CV = range(9)
GL_COL0 = 9 * WIDTH

ROUTE_I1, ROUTE_I2, ROUTE_W1, ROUTE_W2 = 8, 9, 10, 11

VMEM_LIMIT = 56 * 1024 * 1024


def _cparams(sem):
    return pltpu.CompilerParams(dimension_semantics=sem, vmem_limit_bytes=VMEM_LIMIT)


def _sigmoid(x):
    return 1.0 / (1.0 + jnp.exp(-x))


def _silu(x):
    return x / (1.0 + jnp.exp(-x))


def _group_of_rows(row0):
    return jnp.maximum(row0 - N_CTX + LAT_T, 0) // LAT_T


def _adaln_kernel(c_ref, w_ref, b_ref, o_ref):
    s = _silu(c_ref[...]).astype(BF16)
    o_ref[...] = jnp.dot(s, w_ref[...].astype(BF16), preferred_element_type=F32) + b_ref[...]


def _adaln(cond8, w_ada, b_ada):
    tn = 1536
    return pl.pallas_call(
        _adaln_kernel,
        grid=(DEPTH, 6 * D // tn),
        in_specs=[pl.BlockSpec((8, D), lambda l, j: (0, 0)),
                  pl.BlockSpec((None, D, tn), lambda l, j: (l, 0, j)),
                  pl.BlockSpec((None, 1, tn), lambda l, j: (l, 0, j))],
        out_specs=pl.BlockSpec((None, 8, tn), lambda l, j: (l, 0, j)),
        out_shape=jax.ShapeDtypeStruct((DEPTH, 8, 6 * D), F32),
        compiler_params=_cparams(("parallel", "parallel")),
        name="adaln",
    )(cond8, w_ada, b_ada.reshape(DEPTH, 1, 6 * D))


INPROJ_BM = 512


def _stream_rows(x, bm):
    if not isinstance(x, tuple):
        specs = [pl.BlockSpec((bm, D), lambda i, *_: (i, 0))]
        return specs, [x]
    nct = N_CTX // bm
    specs = [pl.BlockSpec((bm, D), lambda i, *_: (jnp.minimum(i, nct - 1), 0)),
             pl.BlockSpec((bm, D), lambda i, *_: (jnp.maximum(i - nct, 0), 0))]
    return specs, list(x)


def _read_stream_rows(x_refs, bm):
    if len(x_refs) == 1:
        return x_refs[0][...]
    return jnp.where(pl.program_id(0) < N_CTX // bm, x_refs[0][...], x_refs[1][...])


def _inproj_kernel(*refs):
    nw_ref, m_ref, w_ref, z_ref, zg_ref, kv_ref, h_ref = refs[-7:]
    x_refs = refs[:-7]
    x = _read_stream_rows(x_refs, INPROJ_BM)
    y = x * lax.rsqrt(jnp.mean(x * x, axis=-1, keepdims=True) + EPS) * nw_ref[...]
    h = (y * (1.0 + m_ref[1:2, :]) + m_ref[0:1, :]).astype(BF16)
    h_ref[...] = h
    acc = jnp.dot(h, w_ref[...], preferred_element_type=F32)
    z_ref[...] = acc.astype(BF16)
    zg_ref[...] = acc[:, COL_AFF * WIDTH:(COL_AFB + 1) * WIDTH]

    @pl.when(pl.program_id(0) < N_CTX // INPROJ_BM)
    def _():
        kv_ref[...] = acc[:, COL_CK * WIDTH:(COL_CV + 1) * WIDTH]


def _inproj(x, norm_w, mods, w_in16, layer):
    bm = INPROJ_BM
    nct = N_CTX // bm
    x_specs, x_args = _stream_rows(x, bm)
    return pl.pallas_call(
        _inproj_kernel,
        grid=(N_TOK // bm,),
        in_specs=x_specs + [
            pl.BlockSpec((1, D), lambda i: (0, 0)),
            pl.BlockSpec((None, 6, D), lambda i: (_group_of_rows(i * bm), 0, 0)),
            pl.BlockSpec((None, D, GL_COL0), lambda i: (layer, 0, 0), pipeline_mode=pl.Buffered(1))],
        out_specs=[pl.BlockSpec((bm, GL_COL0), lambda i: (i, 0)),
                   pl.BlockSpec((bm, 2 * WIDTH), lambda i: (i, 0)),
                   pl.BlockSpec((bm, 2 * WIDTH), lambda i: (jnp.minimum(i, nct - 1), 0)),
                   pl.BlockSpec((bm, D), lambda i: (i, 0))],
        out_shape=[jax.ShapeDtypeStruct((N_TOK, GL_COL0), BF16),
                   jax.ShapeDtypeStruct((N_TOK, 2 * WIDTH), F32),
                   jax.ShapeDtypeStruct((N_CTX, 2 * WIDTH), F32),
                   jax.ShapeDtypeStruct((N_TOK, D), BF16)],
        compiler_params=_cparams(("arbitrary",)),
        name="inproj",
    )(*x_args, norm_w.reshape(1, D), mods, w_in16)


HG_C = 128
HG_LEVELS = (4, 8, 16, 32, 64, 128)
LOG2_E = 1.4426950408889634


def _hgrn_level_ids(reverse):
    t = np.arange(HG_C)[:, None]
    s = np.arange(HG_C)[None, :]
    if reverse:
        t, s = s, t
    lev = np.full((HG_C, HG_C), -1, np.int32)
    lev[(t // 4 == s // 4) & (s <= t)] = 0
    for li, L in enumerate(HG_LEVELS[1:], start=1):
        m = (t // L == s // L) & (t % L >= L // 2) & (s % L < L // 2)
        lev[m] = li
    return lev


def _hgrn_ref_rows(b_scr, d, reverse):
    out = []
    r_lo, r_hi = (2, 6) if reverse else (1, 5)
    sub = lax.broadcasted_iota(jnp.int32, (8, WIDTH), 0)
    pieces = []
    for g in range(HG_C // 8):
        lo = jnp.broadcast_to(b_scr[d, 8 * g + r_lo:8 * g + r_lo + 1, :], (8, WIDTH))
        hi = jnp.broadcast_to(b_scr[d, 8 * g + r_hi:8 * g + r_hi + 1, :], (8, WIDTH))
        pieces.append(jnp.where(sub < 4, lo, hi))
    out.append(jnp.concatenate(pieces, axis=0))
    for L in HG_LEVELS[1:]:
        r = L // 2 - 1 if reverse else L // 2
        pieces = [jnp.broadcast_to(b_scr[d, L * g + r:L * g + r + 1, :], (L, WIDTH))
                  for g in range(HG_C // L)]
        out.append(pieces[0] if len(pieces) == 1 else jnp.concatenate(pieces, axis=0))
    return out


def _hgrn_kernel(*refs, has_init):
    if has_init:
        (qf_ref, ff_ref, vf_ref, qb_ref, fb_ref, vb_ref, lb_ref, levf_ref, levb_ref, s0_ref,
         of_ref, ob_ref, so_ref, st_scr, b_scr) = refs
    else:
        (qf_ref, ff_ref, vf_ref, qb_ref, fb_ref, vb_ref, lb_ref, levf_ref, levb_ref,
         of_ref, ob_ref, so_ref, st_scr, b_scr) = refs
    c = pl.program_id(1)
    C = HG_C

    @pl.when(c == 0)
    def _():
        for d in range(2):
            for h in range(HG_H):
                if has_init:
                    st_scr[d, h] = s0_ref[d, h].T
                else:
                    st_scr[d, h] = jnp.zeros((HG_D, HG_D), F32)

    row = lax.broadcasted_iota(jnp.int32, (C, C), 0)
    col = lax.broadcasted_iota(jnp.int32, (C, C), 1)
    dirs = ((qf_ref, ff_ref, vf_ref, levf_ref, of_ref), (qb_ref, fb_ref, vb_ref, levb_ref, ob_ref))
    for d, (q_ref, f_ref, v_ref, lev_ref, o_ref) in enumerate(dirs):
        reverse = d == 1
        tri = jnp.where((col >= row) if reverse else (col <= row), 1.0, 0.0).astype(BF16)
        q = _silu(q_ref[...].astype(F32))
        fx = jnp.clip(f_ref[...], -GATE_CLIP, GATE_CLIP)
        e = jnp.exp(-fx)
        sig_pos = 1.0 / (1.0 + e)
        sig_neg = e * sig_pos
        lb = lb_ref[d:d + 1, :]
        lf = jnp.log(lb + (1.0 - lb) * sig_pos) * LOG2_E
        k = (1.0 - lb) * sig_neg
        q16 = q.astype(BF16)
        k16 = k.astype(BF16)
        hi = lf.astype(BF16)
        lo = (lf - hi.astype(F32)).astype(BF16)
        b = (jnp.dot(tri, hi, preferred_element_type=F32)
             + jnp.dot(tri, lo, preferred_element_type=F32))
        b_scr[d] = b
        refs_m = _hgrn_ref_rows(b_scr, d, reverse)
        qs, ks = [], []
        for li, m in enumerate(refs_m):
            dlt = b - m
            if li == 0:
                qs.append(q16 * jnp.exp2(dlt).astype(BF16))
                ks.append(k16 * jnp.exp2(-dlt).astype(BF16))
            else:
                fac = jnp.exp2(-jnp.abs(dlt)).astype(BF16)
                qs.append(q16 * fac)
                ks.append(k16 * fac)
        lev = lev_ref[...]
        b_end = b[0:1, :] if reverse else b[C - 1:C, :]
        q_in = q16 * jnp.exp2(b).astype(BF16)
        k_out = k16 * jnp.exp2(b_end - b).astype(BF16)
        dec = jnp.exp2(b_end)
        vb16 = v_ref[...]
        for h in range(HG_H):
            sl = slice(h * HG_D, (h + 1) * HG_D)
            a = jnp.zeros((C, C), F32)
            for li in range(len(HG_LEVELS)):
                p = lax.dot_general(qs[li][:, sl], ks[li][:, sl], (((1,), (1,)), ((), ())),
                                    preferred_element_type=F32)
                a = jnp.where(lev == li, p, a)
            vh = vb16[:, sl]
            st = st_scr[d, h]
            o = (jnp.dot(a.astype(BF16), vh, preferred_element_type=F32)
                 + lax.dot_general(q_in[:, sl], st.astype(BF16), (((1,), (1,)), ((), ())),
                                   preferred_element_type=F32))
            o_ref[:, sl] = o
            upd = lax.dot_general(vh, k_out[:, sl], (((0,), (0,)), ((), ())),
                                  preferred_element_type=F32)
            st_scr[d, h] = st * dec[:, sl] + upd

    @pl.when(c == pl.num_programs(1) - 1)
    def _():
        for d in range(2):
            for h in range(HG_H):
                so_ref[d, h] = st_scr[d, h].T


def _hgrn(z, zg, lb, s0, row_off, n_b, seq_t):
    C = HG_C
    nc = seq_t // C
    base = row_off // C
    has_init = s0 is not None

    def fwd(col):
        return pl.BlockSpec((C, WIDTH), lambda b, c: (base + b * nc + c, col))

    def bwd(col):
        return pl.BlockSpec((C, WIDTH), lambda b, c: (base + b * nc + nc - 1 - c, col))

    full = lambda shape: pl.BlockSpec(shape, lambda b, c: (0,) * len(shape))
    in_specs = [fwd(COL_AQ), fwd(0), fwd(COL_AI), bwd(COL_AQ), bwd(1), bwd(COL_AI),
                full((2, WIDTH)), full((C, C)), full((C, C))]
    args = [z, zg, z, z, zg, z, lb, jnp.asarray(_hgrn_level_ids(False)), jnp.asarray(_hgrn_level_ids(True))]
    if has_init:
        in_specs.append(pl.BlockSpec((None, 2, HG_H, HG_D, HG_D), lambda b, c: (b, 0, 0, 0, 0)))
        args.append(s0)
    return pl.pallas_call(
        functools.partial(_hgrn_kernel, has_init=has_init),
        grid=(n_b, nc),
        in_specs=in_specs,
        out_specs=[pl.BlockSpec((C, WIDTH), lambda b, c: (b * nc + c, 0)),
                   pl.BlockSpec((C, WIDTH), lambda b, c: (b * nc + nc - 1 - c, 0)),
                   pl.BlockSpec((None, 2, HG_H, HG_D, HG_D), lambda b, c: (b, 0, 0, 0, 0))],
        out_shape=[jax.ShapeDtypeStruct((n_b * seq_t, WIDTH), F32),
                   jax.ShapeDtypeStruct((n_b * seq_t, WIDTH), F32),
                   jax.ShapeDtypeStruct((n_b, 2, HG_H, HG_D, HG_D), F32)],
        scratch_shapes=[pltpu.VMEM((2, HG_H, HG_D, HG_D), F32),
                        pltpu.VMEM((2, C, WIDTH), F32)],
        compiler_params=_cparams(("parallel", "arbitrary")),
        name="hgrn_lat" if has_init else "hgrn_ctx",
    )(*args)


POOL_T = 256
POOL_HALO = 16


def _pool_kernel(u_ref, up_ref, un_ref, w_ref, sc_ref, o_ref):
    i = pl.program_id(0)
    is_lat = i >= N_CTX // POOL_T
    seq_t = jnp.where(is_lat, LAT_T, CTX_T)
    t0 = jnp.where(is_lat, ((i - N_CTX // POOL_T) % (LAT_T // POOL_T)) * POOL_T, 0)
    TT, HL = POOL_T, POOL_HALO
    diff = lax.broadcasted_iota(jnp.int32, (TT, TT), 1) - lax.broadcasted_iota(jnp.int32, (TT, TT), 0)
    diff_h = lax.broadcasted_iota(jnp.int32, (TT, HL), 1) - lax.broadcasted_iota(jnp.int32, (TT, HL), 0)
    t_glob = t0 + lax.broadcasted_iota(jnp.int32, (TT, 128), 0)
    has_prev = t0 > 0
    has_next = t0 + TT < seq_t
    for g, w in enumerate(POOL_WINDOWS):
        half = w // 2
        sl = slice(g * 128, (g + 1) * 128)
        band = jnp.where((diff >= -half) & (diff < half), 1.0, 0.0).astype(BF16)
        dp = diff_h - HL
        band_p = jnp.where((dp >= -half) & (dp < half) & has_prev, 1.0, 0.0).astype(BF16)
        dn = diff_h + TT
        band_n = jnp.where((dn >= -half) & (dn < half) & has_next, 1.0, 0.0).astype(BF16)
        u = u_ref[:, sl]
        s = (jnp.dot(band, u, preferred_element_type=F32)
             + jnp.dot(band_p, up_ref[:, sl], preferred_element_type=F32)
             + jnp.dot(band_n, un_ref[:, sl], preferred_element_type=F32))
        cnt = (jnp.minimum(t_glob + half, seq_t) - jnp.maximum(t_glob - half, 0)).astype(F32)
        dd = s / cnt - u.astype(F32)
        y = jnp.dot(dd.astype(BF16), w_ref[g], preferred_element_type=F32)
        o_ref[:, sl] = y * sc_ref[:, sl]


def _pool(z, pool_w16, pool_scale):
    nt = N_TOK // POOL_T
    per = POOL_T // POOL_HALO
    return pl.pallas_call(
        _pool_kernel,
        grid=(nt,),
        in_specs=[pl.BlockSpec((POOL_T, WIDTH), lambda i: (i, COL_BU)),
                  pl.BlockSpec((POOL_HALO, WIDTH), lambda i: (jnp.maximum(i * per - 1, 0), COL_BU)),
                  pl.BlockSpec((POOL_HALO, WIDTH), lambda i: (jnp.minimum((i + 1) * per, nt * per - 1), COL_BU)),
                  pl.BlockSpec((4, 128, 128), lambda i: (0, 0, 0)),
                  pl.BlockSpec((1, WIDTH), lambda i: (0, 0))],
        out_specs=pl.BlockSpec((POOL_T, WIDTH), lambda i: (i, 0)),
        out_shape=jax.ShapeDtypeStruct((N_TOK, WIDTH), F32),
        compiler_params=_cparams(("parallel",)),
        name="pool",
    )(z, z, z, pool_w16, pool_scale.reshape(1, WIDTH))


def _head_mask(hh):
    lane = lax.broadcasted_iota(jnp.int32, (1, 128), 1)
    in_head = (lane >= hh * NA_D) & (lane < (hh + 1) * NA_D)
    return jnp.where(in_head, NA_D ** -0.5, 0.0).astype(BF16)


def _ctx_attn_kernel(q_ref, k_ref, v_ref, o_ref):
    lane = lax.broadcasted_iota(jnp.int32, (CTX_T, 128), 1)
    for j in range(NA_H // 2):
        sl = slice(j * 128, (j + 1) * 128)
        q = q_ref[:, sl]
        kt = k_ref[:, sl]
        vt = v_ref[:, sl]
        outs = []
        for hh in range(2):
            s = lax.dot_general(q * _head_mask(hh), kt, (((1,), (1,)), ((), ())), preferred_element_type=F32)
            m = jnp.max(s, axis=-1, keepdims=True)
            p = jnp.exp(s - m)
            l = jnp.sum(p, axis=-1, keepdims=True)
            outs.append(jnp.dot(p.astype(BF16), vt, preferred_element_type=F32) / l)
        o_ref[:, sl] = jnp.where(lane < NA_D, outs[0], outs[1])


def _ctx_attn(z):
    spec = lambda col: pl.BlockSpec((CTX_T, WIDTH), lambda b: (b, col))
    return pl.pallas_call(
        _ctx_attn_kernel,
        grid=(N_CTX_B,),
        in_specs=[spec(COL_CQ), spec(COL_CK), spec(COL_CV)],
        out_specs=pl.BlockSpec((CTX_T, WIDTH), lambda b: (b, 0)),
        out_shape=jax.ShapeDtypeStruct((N_CTX, WIDTH), F32),
        compiler_params=_cparams(("parallel",)),
        name="ctx_attn",
    )(z, z, z)


NA_QR = 4
NA_KR = 12
NA_NQ = NA_QR * GRID_W
NA_NK = NA_KR * GRID_W
NA_BLOCKS = GRID_ROWS // NA_QR


def _na_key_row0(blk):
    return jnp.clip(blk * NA_QR - WIN_ROWS // 2, 0, GRID_ROWS - NA_KR)


def _na_geometry():
    patterns, var_of_block = [], []
    for blk in range(NA_BLOCKS):
        r = blk * NA_QR + np.arange(NA_QR)[:, None]
        kr = int(np.clip(blk * NA_QR - WIN_ROWS // 2, 0, GRID_ROWS - NA_KR)) + np.arange(NA_KR)[None, :]
        rs = np.clip(r - WIN_ROWS // 2, 0, GRID_ROWS - WIN_ROWS)
        valid = (kr >= rs) & (kr < rs + WIN_ROWS)
        assert (valid.sum(axis=1) == WIN_ROWS).all(), "key rows must cover every query row's window"
        drow = np.where(valid, kr - r + WIN_ROWS - 1, 0)
        key = (drow.tobytes(), valid.tobytes())
        ids = [i for i, (k_, _, _) in enumerate(patterns) if k_ == key]
        if not ids:
            patterns.append((key, drow, valid))
            ids = [len(patterns) - 1]
        var_of_block.append(ids[0])
    drow = np.stack([p[1] for p in patterns])
    valid = np.stack([p[2] for p in patterns])
    return np.asarray(var_of_block, np.int32), drow, valid


NA_DROWS = 2 * WIN_ROWS - 1


def _na_bias_kernel(didx_ref, t2_ref, o_ref):
    v = pl.program_id(0)
    for qr in range(NA_QR):
        for kp in range(NA_KR // 2):
            base = (v * NA_QR + qr) * NA_KR + 2 * kp
            pair = jnp.concatenate([t2_ref[didx_ref[base]], t2_ref[didx_ref[base + 1]]], axis=1)
            o_ref[qr * GRID_W:(qr + 1) * GRID_W, kp * 128:(kp + 1) * 128] = pair


def _na_bias_table(rpb_l):
    _, drow, valid = _na_geometry()
    n_var = valid.shape[0]
    qc = np.arange(GRID_W)[:, None]
    kc = np.arange(GRID_W)[None, :]
    q0 = np.clip(qc - WIN_COLS // 2, 0, GRID_W - WIN_COLS)
    col_in = (kc >= q0) & (kc < q0 + WIN_COLS)
    dcol = np.clip(kc - qc, -(WIN_COLS - 1), WIN_COLS - 1) + WIN_COLS - 1
    oh_col = (dcol[None] == np.arange(2 * WIN_COLS - 1)[:, None, None]).astype(np.float32)
    t2 = jnp.einsum('hdc,cqk->hdqk', rpb_l, jnp.asarray(oh_col), precision=lax.Precision.HIGHEST)
    t2 = jnp.where(col_in[None, None], t2, NEG_BIG)
    t2 = jnp.concatenate([t2, jnp.full((NA_H, 1, GRID_W, GRID_W), NEG_BIG, F32)], axis=1)
    didx = np.where(valid, drow, NA_DROWS).astype(np.int32).reshape(-1)
    grid_spec = pltpu.PrefetchScalarGridSpec(
        num_scalar_prefetch=1,
        grid=(n_var, NA_H),
        in_specs=[pl.BlockSpec((None, NA_DROWS + 1, GRID_W, GRID_W), lambda v, h, didx: (h, 0, 0, 0))],
        out_specs=pl.BlockSpec((None, None, NA_NQ, NA_NK), lambda v, h, didx: (v, h, 0, 0)))
    return pl.pallas_call(
        _na_bias_kernel,
        grid_spec=grid_spec,
        out_shape=jax.ShapeDtypeStruct((n_var, NA_H, NA_NQ, NA_NK), F32),
        compiler_params=_cparams(("parallel", "parallel")),
        name="na_bias",
    )(jnp.asarray(didx), t2)


def _na_attn_kernel(var_ref, q_ref, k_ref, v_ref, kc_ref, vc_ref, tbl_ref, o_ref):
    del var_ref
    blk = pl.program_id(1)
    k0 = pl.multiple_of(_na_key_row0(blk) * GRID_W, GRID_W)
    lane = lax.broadcasted_iota(jnp.int32, (NA_NQ, 128), 1)
    for j in range(NA_H // 2):
        sl = slice(j * 128, (j + 1) * 128)
        q = q_ref[:, sl]
        kt = k_ref[pl.ds(k0, NA_NK), sl]
        vt = v_ref[pl.ds(k0, NA_NK), sl]
        kct = kc_ref[:, sl]
        vct = vc_ref[:, sl]
        outs = []
        for hh in range(2):
            qm = q * _head_mask(hh)
            s_loc = lax.dot_general(qm, kt, (((1,), (1,)), ((), ())), preferred_element_type=F32)
            s_ctx = lax.dot_general(qm, kct, (((1,), (1,)), ((), ())), preferred_element_type=F32)
            tb = tbl_ref[2 * j + hh]
            s_loc = jnp.where(tb > 0.5 * NEG_BIG, s_loc + tb, NEG_BIG)
            m = jnp.maximum(jnp.max(s_loc, axis=-1, keepdims=True), jnp.max(s_ctx, axis=-1, keepdims=True))
            p_loc = jnp.exp(s_loc - m)
            p_ctx = jnp.exp(s_ctx - m)
            l = jnp.sum(p_loc, axis=-1, keepdims=True) + jnp.sum(p_ctx, axis=-1, keepdims=True)
            o = (jnp.dot(p_loc.astype(BF16), vt, preferred_element_type=F32)
                 + jnp.dot(p_ctx.astype(BF16), vct, preferred_element_type=F32))
            outs.append(o / l)
        o_ref[:, sl] = jnp.where(lane < NA_D, outs[0], outs[1])


def _na_attn(z, cache_k16, cache_v16, tbl, layer):
    base_q = N_CTX // NA_NQ
    base_t = N_CTX // LAT_T
    ctx_spec = pl.BlockSpec((None, None, PAST, WIDTH), lambda b, r, var: (b, layer, 0, 0))
    grid_spec = pltpu.PrefetchScalarGridSpec(
        num_scalar_prefetch=1,
        grid=(N_LAT_B, NA_BLOCKS),
        in_specs=[pl.BlockSpec((NA_NQ, WIDTH), lambda b, r, var: (base_q + b * NA_BLOCKS + r, COL_CQ)),
                  pl.BlockSpec((LAT_T, WIDTH), lambda b, r, var: (base_t + b, COL_CK)),
                  pl.BlockSpec((LAT_T, WIDTH), lambda b, r, var: (base_t + b, COL_CV)),
                  ctx_spec, ctx_spec,
                  pl.BlockSpec((None, NA_H, NA_NQ, NA_NK), lambda b, r, var: (var[r], 0, 0, 0))],
        out_specs=pl.BlockSpec((NA_NQ, WIDTH), lambda b, r, var: (b * NA_BLOCKS + r, 0)))
    return pl.pallas_call(
        _na_attn_kernel,
        grid_spec=grid_spec,
        out_shape=jax.ShapeDtypeStruct((N_LAT, WIDTH), F32),
        compiler_params=_cparams(("parallel", "arbitrary")),
        name="na_attn",
    )(jnp.asarray(_na_geometry()[0]), z, z, z, cache_k16, cache_v16, tbl)


MERGE_BM = 512
MERGE_SLABS = 2


def _merge_kernel(*refs, with_router, n_x):
    x_refs, refs = refs[:n_x], refs[n_x:]
    (ofc_ref, obc_ref, occ_ref, ofl_ref, obl_ref, ocl_ref, ag_ref, op_ref, h16_ref, wg0_ref, wg1_ref, m_ref,
     hn_ref, nf_ref, wa_ref, wb_ref, wc_ref, wo_ref) = refs[:18]
    if with_router:
        rhi_ref, rlo_ref, xn_ref, h2_ref, route_ref = refs[18:]
    else:
        xn_ref, h2_ref = refs[18:]
    is_ctx = pl.program_id(0) < N_CTX // MERGE_BM
    for sl in range(MERGE_SLABS):
        rs = slice(sl * MERGE_BM // MERGE_SLABS, (sl + 1) * MERGE_BM // MERGE_SLABS)
        o = jnp.where(is_ctx, ofc_ref[rs, :] + obc_ref[rs, :], ofl_ref[rs, :] + obl_ref[rs, :])
        oc = jnp.where(is_ctx, occ_ref[rs, :], ocl_ref[rs, :])
        parts = []
        for h in range(HG_H):
            oh = o[:, h * HG_D:(h + 1) * HG_D]
            parts.append(oh * lax.rsqrt(jnp.mean(oh * oh, axis=-1, keepdims=True) + EPS))
        oa = jnp.concatenate(parts, axis=1) * hn_ref[...] * _silu(ag_ref[rs, :].astype(F32))
        h16 = h16_ref[rs, :]
        sg0 = _sigmoid(jnp.dot(h16, wg0_ref[...], preferred_element_type=F32).astype(BF16)).astype(F32)
        sg1 = _sigmoid(jnp.dot(h16, wg1_ref[...], preferred_element_type=F32).astype(BF16)).astype(F32)
        sga = sg0[:, :D]
        sgb = jnp.concatenate([sg0[:, D:], sg1[:, :WIDTH]], axis=1)
        sgc = sg1[:, WIDTH:]
        mix = (sga * jnp.dot(oa.astype(BF16), wa_ref[...], preferred_element_type=F32)
               + sgb * jnp.dot(op_ref[rs, :].astype(BF16), wb_ref[...], preferred_element_type=F32)
               + sgc * jnp.dot(oc.astype(BF16), wc_ref[...], preferred_element_type=F32))
        x = x_refs[0][rs, :] if len(x_refs) == 1 else jnp.where(is_ctx, x_refs[0][rs, :], x_refs[1][rs, :])
        xn = x + m_ref[2:3, :] * jnp.dot(mix.astype(BF16), wo_ref[...], preferred_element_type=F32)
        xn_ref[rs, :] = xn
        y = xn * lax.rsqrt(jnp.mean(xn * xn, axis=-1, keepdims=True) + EPS) * nf_ref[...]
        h2 = y * (1.0 + m_ref[4:5, :]) + m_ref[3:4, :]
        h2_ref[rs, :] = h2.astype(h2_ref.dtype)
        if not with_router:
            continue
        hhi = h2.astype(BF16)
        hlo = (h2 - hhi.astype(F32)).astype(BF16)
        logits = (jnp.dot(hhi, rhi_ref[...], preferred_element_type=F32)
                  + jnp.dot(hhi, rlo_ref[...], preferred_element_type=F32)
                  + jnp.dot(hlo, rhi_ref[...], preferred_element_type=F32))
        lane = lax.broadcasted_iota(jnp.int32, logits.shape, 1).astype(F32)
        lg = jnp.where(lane < N_EXP, logits, -jnp.inf)
        m1 = jnp.max(lg, axis=-1, keepdims=True)
        i1 = jnp.min(jnp.where(lg == m1, lane, 128.0), axis=-1, keepdims=True)
        lg2 = jnp.where(lane == i1, -jnp.inf, lg)
        m2 = jnp.max(lg2, axis=-1, keepdims=True)
        i2 = jnp.min(jnp.where(lg2 == m2, lane, 128.0), axis=-1, keepdims=True)
        e = jnp.exp(m2 - m1)
        w1 = 1.0 / (1.0 + e)
        route_ref[rs, :] = (jnp.where(lane == ROUTE_I1, i1, 0.0) + jnp.where(lane == ROUTE_I2, i2, 0.0)
                            + jnp.where(lane == ROUTE_W1, w1, 0.0) + jnp.where(lane == ROUTE_W2, e * w1, 0.0))


def _merge(ctx_parts, lat_parts, z, h16, w_in16, layer, o_pool, x, mods, hgrn_norm_l, norm_ffn_l, wa, wb, wc, wo,
           router_split):
    bm = MERGE_BM
    glw = 1536
    nct = N_CTX // bm
    with_router = router_split is not None
    row = lambda w, col=0: pl.BlockSpec((bm, w), lambda i: (i, col))
    ctx_row = pl.BlockSpec((bm, WIDTH), lambda i: (jnp.minimum(i, nct - 1), 0))
    lat_row = pl.BlockSpec((bm, WIDTH), lambda i: (jnp.maximum(i - nct, 0), 0))
    const = lambda shape: pl.BlockSpec(shape, lambda i: (0,) * len(shape))
    x_specs, x_args = _stream_rows(x, bm)
    gate_w = lambda t: pl.BlockSpec((None, D, glw), lambda i: (layer, 0, GL_COL0 // glw + t))
    in_specs = x_specs + [ctx_row] * 3 + [lat_row] * 3 + [
        row(WIDTH, COL_AG), row(WIDTH),
        row(D), gate_w(0), gate_w(1),
        pl.BlockSpec((None, 6, D), lambda i: (_group_of_rows(i * bm), 0, 0)),
        const((1, WIDTH)), const((1, D)),
        const((WIDTH, D)), const((WIDTH, D)), const((WIDTH, D)), const((D, D))]
    args = x_args + list(ctx_parts) + list(lat_parts) + [
        z, o_pool, h16, w_in16, w_in16, mods, jnp.tile(hgrn_norm_l, HG_H).reshape(1, WIDTH),
        norm_ffn_l.reshape(1, D), wa, wb, wc, wo]
    out_specs = [row(D), row(D)]
    out_shape = [jax.ShapeDtypeStruct((N_TOK, D), F32),
                 jax.ShapeDtypeStruct((N_TOK, D), F32 if with_router else BF16)]
    if with_router:
        in_specs += [const((D, 128)), const((D, 128))]
        args += list(router_split)
        out_specs.append(row(128))
        out_shape.append(jax.ShapeDtypeStruct((N_TOK, 128), F32))
    return pl.pallas_call(
        functools.partial(_merge_kernel, with_router=with_router, n_x=len(x_args)),
        grid=(N_TOK // bm,),
        in_specs=in_specs,
        out_specs=out_specs,
        out_shape=out_shape,
        compiler_params=_cparams(("parallel",)),
        name="merge_route" if with_router else "merge",
    )(*args)


def _ffn_kernel(h_ref, x_ref, m_ref, wa_ref, wb_ref, wo_ref, o_ref):
    h = h_ref[...]
    a = jnp.dot(h, wa_ref[...], preferred_element_type=F32)
    b = jnp.dot(h, wb_ref[...], preferred_element_type=F32)
    g = (_silu(a) * b).astype(BF16)
    o_ref[...] = x_ref[...] + m_ref[5:6, :] * jnp.dot(g, wo_ref[...], preferred_element_type=F32)


FFN_BM = 512


def _ffn(h2, x, mods, w_in, w_out):
    bm = FFN_BM
    resident = pl.Buffered(1)
    return pl.pallas_call(
        _ffn_kernel,
        grid=(N_TOK // bm,),
        in_specs=[pl.BlockSpec((bm, D), lambda i: (i, 0)),
                  pl.BlockSpec((bm, D), lambda i: (i, 0)),
                  pl.BlockSpec((None, 6, D), lambda i: (_group_of_rows(i * bm), 0, 0)),
                  pl.BlockSpec((None, D, F_DENSE), lambda i: (0, 0, 0), pipeline_mode=resident),
                  pl.BlockSpec((None, D, F_DENSE), lambda i: (0, 0, 1), pipeline_mode=resident),
                  pl.BlockSpec((None, F_DENSE, D), lambda i: (0, 0, 0), pipeline_mode=resident)],
        out_specs=pl.BlockSpec((bm, D), lambda i: (i, 0)),
        out_shape=jax.ShapeDtypeStruct((N_TOK, D), F32),
        compiler_params=_cparams(("parallel",)),
        name="ffn",
    )(h2, x, mods, w_in, w_in, w_out)


MOE_BM = 1024
MOE_TF = 512
MOE_NF = F_EXP // MOE_TF
MOE_PAIRS = 2 * N_TOK
MOE_ROWS = MOE_PAIRS + N_EXP * MOE_BM
MOE_TILES = MOE_ROWS // MOE_BM
MOE_CH = -(-MOE_BM // MOE_NF)
MOE_MOVED = MOE_CH * MOE_NF
MOE_BUF = -(-MOE_MOVED // 8) * 8
ROW_DMA_PRIORITY = 1


def _moe_routing(route):
    i1 = route[:, ROUTE_I1].astype(jnp.int32)
    i2 = route[:, ROUTE_I2].astype(jnp.int32)
    ep = jnp.stack([i1, i2], axis=1).reshape(-1)
    onehot = (ep[:, None] == jnp.arange(N_EXP, dtype=jnp.int32)[None, :]).astype(jnp.int32)
    counts = jnp.sum(onehot, axis=0)
    padded = ((counts + MOE_BM - 1) // MOE_BM) * MOE_BM
    ends = jnp.cumsum(padded)
    starts = ends - padded
    first = jnp.cumsum(counts) - counts
    order = jnp.sort(ep * MOE_PAIRS + jnp.arange(MOE_PAIRS, dtype=jnp.int32)) & (MOE_PAIRS - 1)
    tile_row0 = jnp.arange(MOE_TILES, dtype=jnp.int32) * MOE_BM
    tile_active = (tile_row0 < ends[-1]).astype(jnp.int32)
    last_row0 = jnp.maximum(ends[-1] - MOE_BM, 0)
    tile_expert = jnp.sum((jnp.minimum(tile_row0, last_row0)[:, None] >= ends[None, :]).astype(jnp.int32), axis=1)
    tile_expert = jnp.minimum(tile_expert, N_EXP - 1)
    row = jnp.arange(MOE_ROWS, dtype=jnp.int32)
    row_e = jnp.repeat(tile_expert, MOE_BM)
    rank = row - starts[row_e]
    valid = (rank < counts[row_e]) & (row < ends[-1])
    pair = jnp.where(valid, order[jnp.clip(first[row_e] + rank, 0, MOE_PAIRS - 1)], -1)
    pair = pair.reshape(MOE_TILES, MOE_BM)
    tile_count = jnp.sum((pair >= 0).astype(jnp.int32), axis=1)
    src_tok = jnp.pad(jnp.maximum(pair, 0) >> 1, ((0, 0), (0, MOE_BUF - MOE_BM)))
    dst_row = jnp.where(pair >= 0, (pair & 1) * N_TOK + (pair >> 1), -1)
    dst_row = jnp.pad(dst_row, ((1, 0), (0, MOE_BUF - MOE_BM)), constant_values=-1)
    return (src_tok.reshape(MOE_TILES, 1, MOE_BUF), dst_row.reshape(MOE_TILES + 1, 1, MOE_BUF),
            tile_expert, tile_active, tile_count)


def _row(ref, r):
    return ref.at[pl.ds(r, 1), :]


def _moe_group_kernel(te_ref, ta_ref, tc_ref, dprv_ref, dcur_ref, gcur_ref, gnxt_ref, h2_ref, wa_ref, wb_ref, wo_ref,
                      out_ref, xbuf, ybuf, h_scr, gsem, ssem):
    del te_ref
    i = pl.program_id(0)
    f = pl.program_id(1)
    nt = pl.num_programs(0)
    s = i % 2
    o = 1 - s
    active = ta_ref[i] == 1
    prv_active = (i >= 1) & (ta_ref[jnp.maximum(i - 1, 0)] == 1)

    def gather_row(idx_ref, k, slot):
        pltpu.make_async_copy(_row(h2_ref, idx_ref[0, k]), _row(xbuf.at[slot], k),
                              gsem.at[slot]).start(priority=ROW_DMA_PRIORITY)

    def scatter_row(idx_ref, k, slot):
        dst = idx_ref[0, k]

        @pl.when(dst >= 0)
        def _():
            pltpu.make_async_copy(_row(ybuf.at[slot], k), _row(out_ref, dst),
                                  ssem.at[slot]).start(priority=ROW_DMA_PRIORITY)

    def wait_rows(sem, n):
        n = jnp.asarray(n, jnp.int32)
        n8 = pl.multiple_of((n >> 3) << 3, 8)

        @pl.when(n8 > 0)
        def _():
            pltpu.make_async_copy(xbuf.at[0, pl.ds(0, n8), :], ybuf.at[0, pl.ds(0, n8), :], sem).wait()

        def one(k, carry):
            pltpu.make_async_copy(_row(xbuf.at[0], 0), _row(ybuf.at[0], 0), sem).wait()
            return carry
        lax.fori_loop(0, n - n8, one, 0)

    def rows_loop(fn, n):
        def body(k, carry):
            fn(k)
            return carry
        lax.fori_loop(0, n, body, 0, unroll=8)

    @pl.when((i == 0) & (f == 0))
    def _():
        rows_loop(lambda k: gather_row(gcur_ref, k, 0), MOE_MOVED)

    @pl.when((f == 0) & (i >= 2))
    def _():
        wait_rows(ssem.at[s], tc_ref[jnp.maximum(i - 2, 0)])

    @pl.when((f == 0) & ((i == 0) | prv_active))
    def _():
        wait_rows(gsem.at[s], MOE_MOVED)

    @pl.when((f == 0) & active)
    def _():
        h_scr[...] = xbuf[s, 0:MOE_BM, :].astype(BF16)
        ybuf[s, 0:MOE_BM, :] = jnp.zeros((MOE_BM, D), F32)

    def step(rows):
        for u in range(MOE_CH):
            gather_row(gnxt_ref, f * MOE_CH + u, o)
        for u in range(MOE_CH):
            scatter_row(dprv_ref, f * MOE_CH + u, o)
        h = h_scr[0:rows, :]
        a = jnp.dot(h, wa_ref[...].astype(BF16), preferred_element_type=F32)
        b = jnp.dot(h, wb_ref[...].astype(BF16), preferred_element_type=F32)
        g = (_silu(a) * b).astype(BF16)
        ybuf[s, 0:rows, :] += jnp.dot(g, wo_ref[...].astype(BF16), preferred_element_type=F32)

    few = tc_ref[i] <= MOE_BM // 2

    @pl.when(active & jnp.logical_not(few))
    def _():
        step(MOE_BM)

    @pl.when(active & few)
    def _():
        step(MOE_BM // 2)

    @pl.when(jnp.logical_not(active) & prv_active)
    def _():
        rows_loop(lambda u: scatter_row(dprv_ref, f * MOE_CH + u, o), MOE_CH)

    @pl.when((i == nt - 1) & (f == pl.num_programs(1) - 1))
    def _():
        wait_rows(ssem.at[o], tc_ref[jnp.maximum(i - 1, 0)])
        rows_loop(lambda k: scatter_row(dcur_ref, k, s), MOE_BM)
        wait_rows(ssem.at[s], tc_ref[i])

        @pl.when(active)
        def _():
            wait_rows(gsem.at[o], MOE_MOVED)


def _moe_group(h2, src_tok, dst_row, tile_expert, tile_active, tile_count, w_in, w_out):
    nf = MOE_NF
    last = MOE_TILES - 1

    def fblk(i, f, ta):
        return jnp.where(ta[i] == 1, f, nf - 1)

    idx_spec = lambda off, hi: pl.BlockSpec((None, 1, MOE_BUF),
                                            lambda i, f, te, ta, tc: (jnp.minimum(i + off, hi), 0, 0),
                                            memory_space=pltpu.SMEM)
    grid_spec = pltpu.PrefetchScalarGridSpec(
        num_scalar_prefetch=3,
        grid=(MOE_TILES, nf),
        in_specs=[idx_spec(0, last + 1), idx_spec(1, last + 1),
                  idx_spec(0, last), idx_spec(1, last),
                  pl.BlockSpec(memory_space=pl.ANY),
                  pl.BlockSpec((None, None, D, MOE_TF), lambda i, f, te, ta, tc: (0, te[i], 0, fblk(i, f, ta))),
                  pl.BlockSpec((None, None, D, MOE_TF), lambda i, f, te, ta, tc: (0, te[i], 0, nf + fblk(i, f, ta))),
                  pl.BlockSpec((None, None, MOE_TF, D), lambda i, f, te, ta, tc: (0, te[i], fblk(i, f, ta), 0))],
        out_specs=pl.BlockSpec(memory_space=pl.ANY),
        scratch_shapes=[pltpu.VMEM((2, MOE_BUF, D), F32), pltpu.VMEM((2, MOE_BUF, D), F32),
                        pltpu.VMEM((MOE_BM, D), BF16),
                        pltpu.SemaphoreType.DMA((2,)), pltpu.SemaphoreType.DMA((2,))])
    return pl.pallas_call(
        _moe_group_kernel,
        grid_spec=grid_spec,
        out_shape=jax.ShapeDtypeStruct((MOE_PAIRS, D), F32),
        compiler_params=_cparams(("arbitrary", "arbitrary")),
        name="moe_group",
    )(tile_expert, tile_active, tile_count, dst_row, dst_row, src_tok, src_tok, h2, w_in, w_in, w_out)


MOE_COMBINE_BM = 512


def _moe_combine_kernel(route_ref, x_ref, m_ref, nf_ref, y1_ref, y2_ref, outc_ref, outl_ref):
    route = route_ref[...]
    f = (y1_ref[...] * route[:, ROUTE_W1:ROUTE_W1 + 1]
         + y2_ref[...] * route[:, ROUTE_W2:ROUTE_W2 + 1])
    xn = x_ref[...] + m_ref[5:6, :] * f
    y = xn * lax.rsqrt(jnp.mean(xn * xn, axis=-1, keepdims=True) + EPS) * nf_ref[...]
    is_ctx = pl.program_id(0) < N_CTX // MOE_COMBINE_BM

    @pl.when(is_ctx)
    def _():
        outc_ref[...] = y

    @pl.when(jnp.logical_not(is_ctx))
    def _():
        outl_ref[...] = y


def _moe_combine(ys, route, x, mods, norm_final):
    bm = MOE_COMBINE_BM
    nt = N_TOK // bm
    nct = N_CTX // bm
    return pl.pallas_call(
        _moe_combine_kernel,
        grid=(nt,),
        in_specs=[pl.BlockSpec((bm, 128), lambda i: (i, 0)),
                  pl.BlockSpec((bm, D), lambda i: (i, 0)),
                  pl.BlockSpec((None, 6, D), lambda i: (_group_of_rows(i * bm), 0, 0)),
                  pl.BlockSpec((1, D), lambda i: (0, 0)),
                  pl.BlockSpec((bm, D), lambda i: (i, 0)),
                  pl.BlockSpec((bm, D), lambda i: (nt + i, 0))],
        out_specs=[pl.BlockSpec((bm, D), lambda i: (jnp.minimum(i, nct - 1), 0)),
                   pl.BlockSpec((bm, D), lambda i: (jnp.maximum(i - nct, 0), 0))],
        out_shape=[jax.ShapeDtypeStruct((N_CTX, D), F32), jax.ShapeDtypeStruct((N_LAT, D), F32)],
        compiler_params=_cparams(("arbitrary",)),
        name="moe_combine",
    )(route, x, mods, norm_final.reshape(1, D), ys, ys)


def _moe(h2, route, x, mods, w_in, w_out, norm_final):
    src_tok, dst_row, tile_expert, tile_active, tile_count = _moe_routing(route)
    ys = _moe_group(h2, src_tok, dst_row, tile_expert, tile_active, tile_count, w_in, w_out)
    return _moe_combine(ys, route, x, mods, norm_final)


def _hgrn_lower_bounds(lb_param):
    sm = jax.nn.softmax(lb_param.astype(F32), axis=0)
    return jnp.cumsum(sm, axis=0) - sm[0:1]


def kernel(x_prompt, x_sample, cache_k, cache_v, state_hgrn, c, c_ctx, w_ada, b_ada, norm_mix, norm_ffn,
           w_in, hgrn_lb, hgrn_norm, pool_w, pool_scale, rpb, w_branch_a, w_branch_b, w_branch_c, w_out,
           ffn_w_in, ffn_w_out, router, moe_w_in, moe_w_out, norm_final):
    x = (x_prompt.reshape(N_CTX, D), x_sample.reshape(N_LAT, D))
    cond8 = jnp.concatenate([c_ctx[None], c, jnp.zeros((5, D), F32)], axis=0)
    mods = _adaln(cond8, w_ada, b_ada)[:, :3].reshape(DEPTH, 3, 6, D)
    lbs = _hgrn_lower_bounds(hgrn_lb)
    ck = cache_k.reshape(N_LAT_B, DEPTH, PAST, WIDTH).astype(BF16)
    cv = cache_v.reshape(N_LAT_B, DEPTH, PAST, WIDTH).astype(BF16)
    w_in16 = w_in.astype(BF16)
    router_pad = jnp.pad(router[0], ((0, 0), (0, 128 - N_EXP)))
    r_hi = router_pad.astype(BF16)
    r_lo = (router_pad - r_hi.astype(F32)).astype(BF16)

    kvs, ss = [], []
    for l in range(DEPTH):
        z, zg, kv, h16 = _inproj(x, norm_mix[l], mods[l], w_in16, l)
        of_c, ob_c, s_ctx = _hgrn(z, zg, lbs[l], None, 0, N_CTX_B, CTX_T)
        of_l, ob_l, _ = _hgrn(z, zg, lbs[l], state_hgrn[:, l], N_CTX, N_LAT_B, LAT_T)
        o_pool = _pool(z, pool_w[l].astype(BF16), pool_scale[l])
        oc_c = _ctx_attn(z)
        oc_l = _na_attn(z, ck, cv, _na_bias_table(rpb[l]), l)
        merged = _merge((of_c, ob_c, oc_c), (of_l, ob_l, oc_l), z, h16, w_in16, l, o_pool, x, mods[l],
                        hgrn_norm[l], norm_ffn[l],
                        w_branch_a[l].astype(BF16), w_branch_b[l].astype(BF16),
                        w_branch_c[l].astype(BF16), w_out[l].astype(BF16),
                        (r_hi, r_lo) if l % 2 == 1 else None)
        if l % 2 == 0:
            x, h2 = merged
            x = _ffn(h2, x, mods[l], ffn_w_in.astype(BF16), ffn_w_out.astype(BF16))
        else:
            x, h2, route = merged
            y_ctx, y_lat = _moe(h2, route, x, mods[l], moe_w_in, moe_w_out, norm_final)
        kvs.append(kv)
        ss.append(s_ctx)

    def cache(lo):
        per_layer = [kv[:, lo:lo + WIDTH].reshape(N_CTX_B, 1, CTX_T, NA_H, NA_D) for kv in kvs]
        return jnp.concatenate(per_layer, axis=1)

    y_prompt = y_ctx.reshape(N_CTX_B, CTX_T, D)
    y_sample = y_lat.reshape(N_LAT_B, LAT_T, D)
    return (y_prompt, y_sample, cache(0), cache(WIDTH), jnp.stack(ss, axis=1))
```

```python
import functools

import numpy as np
import jax
import jax.numpy as jnp
from jax import lax
from jax.experimental import pallas as pl
from jax.experimental.pallas import tpu as pltpu

F32 = jnp.float32
BF16 = jnp.bfloat16

D = 1024
N_CTX_B, CTX_T = 32, 256
N_LAT_B, LAT_T = 2, 4096
N_CTX = N_CTX_B * CTX_T
N_LAT = N_LAT_B * LAT_T
N_TOK = N_CTX + N_LAT
DEPTH = 2
GRID_W = 64
GRID_ROWS = LAT_T // GRID_W
PAST = 512
HG_H, HG_D = 4, 128
WIDTH = 512
NA_H, NA_D = 8, 64
WIN_ROWS, WIN_COLS = 8, 16
POOL_WINDOWS = (2, 4, 8, 16)
IN_COLS = 7680
F_DENSE = 2816
N_EXP = 8
F_EXP = 3584
GATE_CLIP = 30.0
EPS = 1e-6
NEG_BIG = -1e30

COL_AQ, COL_AFF, COL_AFB, COL_AI, COL_AG, COL_BU, COL_CQ, COL_CK, COL_CV = range(9)
GL_COL0 = 9 * WIDTH

ROUTE_I1, ROUTE_I2, ROUTE_W1, ROUTE_W2 = 8, 9, 10, 11

VMEM_LIMIT = 56 * 1024 * 1024


def _cparams(sem):
    return pltpu.CompilerParams(dimension_semantics=sem, vmem_limit_bytes=VMEM_LIMIT)


def _sigmoid(x):
    return 1.0 / (1.0 + jnp.exp(-x))


def _silu(x):
    return x / (1.0 + jnp.exp(-x))


def _group_of_rows(row0):
    return jnp.maximum(row0 - N_CTX + LAT_T, 0) // LAT_T


def _adaln_kernel(c_ref, w_ref, b_ref, o_ref):
    s = _silu(c_ref[...]).astype(BF16)
    o_ref[...] = jnp.dot(s, w_ref[...].astype(BF16), preferred_element_type=F32) + b_ref[...]


def _adaln(cond8, w_ada, b_ada):
    tn = 1536
    return pl.pallas_call(
        _adaln_kernel,
        grid=(DEPTH, 6 * D // tn),
        in_specs=[pl.BlockSpec((8, D), lambda l, j: (0, 0)),
                  pl.BlockSpec((None, D, tn), lambda l, j: (l, 0, j)),
                  pl.BlockSpec((None, 1, tn), lambda l, j: (l, 0, j))],
        out_specs=pl.BlockSpec((None, 8, tn), lambda l, j: (l, 0, j)),
        out_shape=jax.ShapeDtypeStruct((DEPTH, 8, 6 * D), F32),
        compiler_params=_cparams(("parallel", "parallel")),
        name="adaln",
    )(cond8, w_ada, b_ada.reshape(DEPTH, 1, 6 * D))


INPROJ_BM = 512


def _stream_rows(x, bm):
    if not isinstance(x, tuple):
        specs = [pl.BlockSpec((bm, D), lambda i, *_: (i, 0))]
        return specs, [x]
    nct = N_CTX // bm
    specs = [pl.BlockSpec((bm, D), lambda i, *_: (jnp.minimum(i, nct - 1), 0)),
             pl.BlockSpec((bm, D), lambda i, *_: (jnp.maximum(i - nct, 0), 0))]
    return specs, list(x)


def _read_stream_rows(x_refs, bm):
    if len(x_refs) == 1:
        return x_refs[0][...]
    return jnp.where(pl.program_id(0) < N_CTX // bm, x_refs[0][...], x_refs[1][...])


def _inproj_kernel(*refs):
    nw_ref, m_ref, w_ref, z_ref, zg_ref, k_ref, v_ref, h_ref = refs[-8:]
    x_refs = refs[:-8]
    x = _read_stream_rows(x_refs, INPROJ_BM)
    y = x * lax.rsqrt(jnp.mean(x * x, axis=-1, keepdims=True) + EPS) * nw_ref[...]
    h = (y * (1.0 + m_ref[1:2, :]) + m_ref[0:1, :]).astype(BF16)
    h_ref[...] = h
    acc = jnp.dot(h, w_ref[...], preferred_element_type=F32)
    z_ref[...] = acc.astype(BF16)
    zg_ref[...] = acc[:, COL_AFF * WIDTH:(COL_AFB + 1) * WIDTH]

    @pl.when(pl.program_id(0) < N_CTX // INPROJ_BM)
    def _():
        k_ref[...] = acc[:, COL_CK * WIDTH:(COL_CK + 1) * WIDTH]
        v_ref[...] = acc[:, COL_CV * WIDTH:(COL_CV + 1) * WIDTH]


def _inproj(x, norm_w, mods, w_in16, layer):
    bm = INPROJ_BM
    nct = N_CTX // bm
    x_specs, x_args = _stream_rows(x, bm)
    return pl.pallas_call(
        _inproj_kernel,
        grid=(N_TOK // bm,),
        in_specs=x_specs + [
            pl.BlockSpec((1, D), lambda i: (0, 0)),
            pl.BlockSpec((None, 6, D), lambda i: (_group_of_rows(i * bm), 0, 0)),
            pl.BlockSpec((None, D, GL_COL0), lambda i: (layer, 0, 0), pipeline_mode=pl.Buffered(1))],
        out_specs=[pl.BlockSpec((bm, GL_COL0), lambda i: (i, 0)),
                   pl.BlockSpec((bm, 2 * WIDTH), lambda i: (i, 0)),
                   pl.BlockSpec((bm, WIDTH), lambda i: (jnp.minimum(i, nct - 1), 0)),
                   pl.BlockSpec((bm, WIDTH), lambda i: (jnp.minimum(i, nct - 1), 0)),
                   pl.BlockSpec((bm, D), lambda i: (i, 0))],
        out_shape=[jax.ShapeDtypeStruct((N_TOK, GL_COL0), BF16),
                   jax.ShapeDtypeStruct((N_TOK, 2 * WIDTH), F32),
                   jax.ShapeDtypeStruct((N_CTX, WIDTH), F32),
                   jax.ShapeDtypeStruct((N_CTX, WIDTH), F32),
                   jax.ShapeDtypeStruct((N_TOK, D), BF16)],
        compiler_params=_cparams(("arbitrary",)),
        name="inproj",
    )(*x_args, norm_w.reshape(1, D), mods, w_in16)


HG_C = 128
HG_LEVELS = (4, 8, 16, 32, 64, 128)
LOG2_E = 1.4426950408889634


def _hgrn_level_ids(reverse):
    t = np.arange(HG_C)[:, None]
    s = np.arange(HG_C)[None, :]
    if reverse:
        t, s = s, t
    lev = np.full((HG_C, HG_C), -1, np.int32)
    lev[(t // 4 == s // 4) & (s <= t)] = 0
    for li, L in enumerate(HG_LEVELS[1:], start=1):
        m = (t // L == s // L) & (t % L >= L // 2) & (s % L < L // 2)
        lev[m] = li
    return lev


def _hgrn_ref_rows(b_scr, d, reverse):
    out = []
    r_lo, r_hi = (2, 6) if reverse else (1, 5)
    sub = lax.broadcasted_iota(jnp.int32, (8, WIDTH), 0)
    pieces = []
    for g in range(HG_C // 8):
        lo = jnp.broadcast_to(b_scr[d, 8 * g + r_lo:8 * g + r_lo + 1, :], (8, WIDTH))
        hi = jnp.broadcast_to(b_scr[d, 8 * g + r_hi:8 * g + r_hi + 1, :], (8, WIDTH))
        pieces.append(jnp.where(sub < 4, lo, hi))
    out.append(jnp.concatenate(pieces, axis=0))
    for L in HG_LEVELS[1:]:
        r = L // 2 - 1 if reverse else L // 2
        pieces = [jnp.broadcast_to(b_scr[d, L * g + r:L * g + r + 1, :], (L, WIDTH))
                  for g in range(HG_C // L)]
        out.append(pieces[0] if len(pieces) == 1 else jnp.concatenate(pieces, axis=0))
    return out


def _hgrn_kernel(*refs, has_init):
    if has_init:
        (qf_ref, ff_ref, vf_ref, qb_ref, fb_ref, vb_ref, lb_ref, levf_ref, levb_ref, s0_ref,
         of_ref, ob_ref, so_ref, st_scr, b_scr) = refs
    else:
        (qf_ref, ff_ref, vf_ref, qb_ref, fb_ref, vb_ref, lb_ref, levf_ref, levb_ref,
         of_ref, ob_ref, so_ref, st_scr, b_scr) = refs
    c = pl.program_id(1)
    C = HG_C

    @pl.when(c == 0)
    def _():
        for d in range(2):
            for h in range(HG_H):
                if has_init:
                    st_scr[d, h] = s0_ref[d, h].T
                else:
                    st_scr[d, h] = jnp.zeros((HG_D, HG_D), F32)

    row = lax.broadcasted_iota(jnp.int32, (C, C), 0)
    col = lax.broadcasted_iota(jnp.int32, (C, C), 1)
    dirs = ((qf_ref, ff_ref, vf_ref, levf_ref, of_ref), (qb_ref, fb_ref, vb_ref, levb_ref, ob_ref))
    for d, (q_ref, f_ref, v_ref, lev_ref, o_ref) in enumerate(dirs):
        reverse = d == 1
        tri = jnp.where((col >= row) if reverse else (col <= row), 1.0, 0.0).astype(BF16)
        q = _silu(q_ref[...].astype(F32))
        fx = jnp.clip(f_ref[...], -GATE_CLIP, GATE_CLIP)
        e = jnp.exp(-fx)
        sig_pos = 1.0 / (1.0 + e)
        sig_neg = e * sig_pos
        lb = lb_ref[d:d + 1, :]
        lf = jnp.log(lb + (1.0 - lb) * sig_pos) * LOG2_E
        k = (1.0 - lb) * sig_neg
        q16 = q.astype(BF16)
        k16 = k.astype(BF16)
        hi = lf.astype(BF16)
        lo = (lf - hi.astype(F32)).astype(BF16)
        b = (jnp.dot(tri, hi, preferred_element_type=F32)
             + jnp.dot(tri, lo, preferred_element_type=F32))
        b_scr[d] = b
        refs_m = _hgrn_ref_rows(b_scr, d, reverse)
        qs, ks = [], []
        for li, m in enumerate(refs_m):
            dlt = b - m
            if li == 0:
                qs.append(q16 * jnp.exp2(dlt).astype(BF16))
                ks.append(k16 * jnp.exp2(-dlt).astype(BF16))
            else:
                fac = jnp.exp2(-jnp.abs(dlt)).astype(BF16)
                qs.append(q16 * fac)
                ks.append(k16 * fac)
        lev = lev_ref[...]
        b_end = b[0:1, :] if reverse else b[C - 1:C, :]
        q_in = q16 * jnp.exp2(b).astype(BF16)
        k_out = k16 * jnp.exp2(b_end - b).astype(BF16)
        dec = jnp.exp2(b_end)
        vb16 = v_ref[...]
        for h in range(HG_H):
            sl = slice(h * HG_D, (h + 1) * HG_D)
            a = jnp.zeros((C, C), F32)
            for li in range(len(HG_LEVELS)):
                p = lax.dot_general(qs[li][:, sl], ks[li][:, sl], (((1,), (1,)), ((), ())),
                                    preferred_element_type=F32)
                a = jnp.where(lev == li, p, a)
            vh = vb16[:, sl]
            st = st_scr[d, h]
            o = (jnp.dot(a.astype(BF16), vh, preferred_element_type=F32)
                 + lax.dot_general(q_in[:, sl], st.astype(BF16), (((1,), (1,)), ((), ())),
                                   preferred_element_type=F32))
            o_ref[:, sl] = o
            upd = lax.dot_general(vh, k_out[:, sl], (((0,), (0,)), ((), ())),
                                  preferred_element_type=F32)
            st_scr[d, h] = st * dec[:, sl] + upd

    @pl.when(c == pl.num_programs(1) - 1)
    def _():
        for d in range(2):
            for h in range(HG_H):
                so_ref[d, h] = st_scr[d, h].T


def _hgrn(z, zg, lb, s0, row_off, n_b, seq_t):
    C = HG_C
    nc = seq_t // C
    base = row_off // C
    has_init = s0 is not None

    def fwd(col):
        return pl.BlockSpec((C, WIDTH), lambda b, c: (base + b * nc + c, col))

    def bwd(col):
        return pl.BlockSpec((C, WIDTH), lambda b, c: (base + b * nc + nc - 1 - c, col))

    full = lambda shape: pl.BlockSpec(shape, lambda b, c: (0,) * len(shape))
    in_specs = [fwd(COL_AQ), fwd(0), fwd(COL_AI), bwd(COL_AQ), bwd(1), bwd(COL_AI),
                full((2, WIDTH)), full((C, C)), full((C, C))]
    args = [z, zg, z, z, zg, z, lb, jnp.asarray(_hgrn_level_ids(False)), jnp.asarray(_hgrn_level_ids(True))]
    if has_init:
        in_specs.append(pl.BlockSpec((None, 2, HG_H, HG_D, HG_D), lambda b, c: (b, 0, 0, 0, 0)))
        args.append(s0)
    return pl.pallas_call(
        functools.partial(_hgrn_kernel, has_init=has_init),
        grid=(n_b, nc),
        in_specs=in_specs,
        out_specs=[pl.BlockSpec((C, WIDTH), lambda b, c: (b * nc + c, 0)),
                   pl.BlockSpec((C, WIDTH), lambda b, c: (b * nc + nc - 1 - c, 0)),
                   pl.BlockSpec((None, 2, HG_H, HG_D, HG_D), lambda b, c: (b, 0, 0, 0, 0))],
        out_shape=[jax.ShapeDtypeStruct((n_b * seq_t, WIDTH), F32),
                   jax.ShapeDtypeStruct((n_b * seq_t, WIDTH), F32),
                   jax.ShapeDtypeStruct((n_b, 2, HG_H, HG_D, HG_D), F32)],
        scratch_shapes=[pltpu.VMEM((2, HG_H, HG_D, HG_D), F32),
                        pltpu.VMEM((2, C, WIDTH), F32)],
        compiler_params=_cparams(("parallel", "arbitrary")),
        name="hgrn_lat" if has_init else "hgrn_ctx",
    )(*args)


POOL_T = 256
POOL_HALO = 16


def _pool_kernel(u_ref, up_ref, un_ref, w_ref, sc_ref, o_ref):
    i = pl.program_id(0)
    is_lat = i >= N_CTX // POOL_T
    seq_t = jnp.where(is_lat, LAT_T, CTX_T)
    t0 = jnp.where(is_lat, ((i - N_CTX // POOL_T) % (LAT_T // POOL_T)) * POOL_T, 0)
    TT, HL = POOL_T, POOL_HALO
    diff = lax.broadcasted_iota(jnp.int32, (TT, TT), 1) - lax.broadcasted_iota(jnp.int32, (TT, TT), 0)
    diff_h = lax.broadcasted_iota(jnp.int32, (TT, HL), 1) - lax.broadcasted_iota(jnp.int32, (TT, HL), 0)
    t_glob = t0 + lax.broadcasted_iota(jnp.int32, (TT, 128), 0)
    has_prev = t0 > 0
    has_next = t0 + TT < seq_t
    for g, w in enumerate(POOL_WINDOWS):
        half = w // 2
        sl = slice(g * 128, (g + 1) * 128)
        band = jnp.where((diff >= -half) & (diff < half), 1.0, 0.0).astype(BF16)
        dp = diff_h - HL
        band_p = jnp.where((dp >= -half) & (dp < half) & has_prev, 1.0, 0.0).astype(BF16)
        dn = diff_h + TT
        band_n = jnp.where((dn >= -half) & (dn < half) & has_next, 1.0, 0.0).astype(BF16)
        u = u_ref[:, sl]
        s = (jnp.dot(band, u, preferred_element_type=F32)
             + jnp.dot(band_p, up_ref[:, sl], preferred_element_type=F32)
             + jnp.dot(band_n, un_ref[:, sl], preferred_element_type=F32))
        cnt = (jnp.minimum(t_glob + half, seq_t) - jnp.maximum(t_glob - half, 0)).astype(F32)
        dd = s / cnt - u.astype(F32)
        y = jnp.dot(dd.astype(BF16), w_ref[g], preferred_element_type=F32)
        o_ref[:, sl] = y * sc_ref[:, sl]


def _pool(z, pool_w16, pool_scale):
    nt = N_TOK // POOL_T
    per = POOL_T // POOL_HALO
    return pl.pallas_call(
        _pool_kernel,
        grid=(nt,),
        in_specs=[pl.BlockSpec((POOL_T, WIDTH), lambda i: (i, COL_BU)),
                  pl.BlockSpec((POOL_HALO, WIDTH), lambda i: (jnp.maximum(i * per - 1, 0), COL_BU)),
                  pl.BlockSpec((POOL_HALO, WIDTH), lambda i: (jnp.minimum((i + 1) * per, nt * per - 1), COL_BU)),
                  pl.BlockSpec((4, 128, 128), lambda i: (0, 0, 0)),
                  pl.BlockSpec((1, WIDTH), lambda i: (0, 0))],
        out_specs=pl.BlockSpec((POOL_T, WIDTH), lambda i: (i, 0)),
        out_shape=jax.ShapeDtypeStruct((N_TOK, WIDTH), F32),
        compiler_params=_cparams(("parallel",)),
        name="pool",
    )(z, z, z, pool_w16, pool_scale.reshape(1, WIDTH))


def _head_mask(hh):
    lane = lax.broadcasted_iota(jnp.int32, (1, 128), 1)
    in_head = (lane >= hh * NA_D) & (lane < (hh + 1) * NA_D)
    return jnp.where(in_head, NA_D ** -0.5, 0.0).astype(BF16)


def _ctx_attn_kernel(q_ref, k_ref, v_ref, o_ref):
    lane = lax.broadcasted_iota(jnp.int32, (CTX_T, 128), 1)
    for j in range(NA_H // 2):
        sl = slice(j * 128, (j + 1) * 128)
        q = q_ref[:, sl]
        kt = k_ref[:, sl]
        vt = v_ref[:, sl]
        outs = []
        for hh in range(2):
            s = lax.dot_general(q * _head_mask(hh), kt, (((1,), (1,)), ((), ())), preferred_element_type=F32)
            m = jnp.max(s, axis=-1, keepdims=True)
            p = jnp.exp(s - m)
            l = jnp.sum(p, axis=-1, keepdims=True)
            outs.append(jnp.dot(p.astype(BF16), vt, preferred_element_type=F32) / l)
        o_ref[:, sl] = jnp.where(lane < NA_D, outs[0], outs[1])


def _ctx_attn(z):
    spec = lambda col: pl.BlockSpec((CTX_T, WIDTH), lambda b: (b, col))
    return pl.pallas_call(
        _ctx_attn_kernel,
        grid=(N_CTX_B,),
        in_specs=[spec(COL_CQ), spec(COL_CK), spec(COL_CV)],
        out_specs=pl.BlockSpec((CTX_T, WIDTH), lambda b: (b, 0)),
        out_shape=jax.ShapeDtypeStruct((N_CTX, WIDTH), F32),
        compiler_params=_cparams(("parallel",)),
        name="ctx_attn",
    )(z, z, z)


NA_QR = 4
NA_KR = 12
NA_NQ = NA_QR * GRID_W
NA_NK = NA_KR * GRID_W
NA_BLOCKS = GRID_ROWS // NA_QR


def _na_key_row0(blk):
    return jnp.clip(blk * NA_QR - WIN_ROWS // 2, 0, GRID_ROWS - NA_KR)


def _na_geometry():
    patterns, var_of_block = [], []
    for blk in range(NA_BLOCKS):
        r = blk * NA_QR + np.arange(NA_QR)[:, None]
        kr = int(np.clip(blk * NA_QR - WIN_ROWS // 2, 0, GRID_ROWS - NA_KR)) + np.arange(NA_KR)[None, :]
        rs = np.clip(r - WIN_ROWS // 2, 0, GRID_ROWS - WIN_ROWS)
        valid = (kr >= rs) & (kr < rs + WIN_ROWS)
        assert (valid.sum(axis=1) == WIN_ROWS).all(), "key rows must cover every query row's window"
        drow = np.where(valid, kr - r + WIN_ROWS - 1, 0)
        key = (drow.tobytes(), valid.tobytes())
        ids = [i for i, (k_, _, _) in enumerate(patterns) if k_ == key]
        if not ids:
            patterns.append((key, drow, valid))
            ids = [len(patterns) - 1]
        var_of_block.append(ids[0])
    drow = np.stack([p[1] for p in patterns])
    valid = np.stack([p[2] for p in patterns])
    return np.asarray(var_of_block, np.int32), drow, valid


NA_DROWS = 2 * WIN_ROWS - 1


def _na_bias_kernel(didx_ref, t2_ref, o_ref):
    v = pl.program_id(0)
    for qr in range(NA_QR):
        for kp in range(NA_KR // 2):
            base = (v * NA_QR + qr) * NA_KR + 2 * kp
            pair = jnp.concatenate([t2_ref[didx_ref[base]], t2_ref[didx_ref[base + 1]]], axis=1)
            o_ref[qr * GRID_W:(qr + 1) * GRID_W, kp * 128:(kp + 1) * 128] = pair


def _na_bias_table(rpb_l):
    _, drow, valid = _na_geometry()
    n_var = valid.shape[0]
    qc = np.arange(GRID_W)[:, None]
    kc = np.arange(GRID_W)[None, :]
    q0 = np.clip(qc - WIN_COLS // 2, 0, GRID_W - WIN_COLS)
    col_in = (kc >= q0) & (kc < q0 + WIN_COLS)
    dcol = np.clip(kc - qc, -(WIN_COLS - 1), WIN_COLS - 1) + WIN_COLS - 1
    oh_col = (dcol[None] == np.arange(2 * WIN_COLS - 1)[:, None, None]).astype(np.float32)
    t2 = jnp.einsum('hdc,cqk->hdqk', rpb_l, jnp.asarray(oh_col), precision=lax.Precision.HIGHEST)
    t2 = jnp.where(col_in[None, None], t2, NEG_BIG)
    t2 = jnp.concatenate([t2, jnp.full((NA_H, 1, GRID_W, GRID_W), NEG_BIG, F32)], axis=1)
    didx = np.where(valid, drow, NA_DROWS).astype(np.int32).reshape(-1)
    grid_spec = pltpu.PrefetchScalarGridSpec(
        num_scalar_prefetch=1,
        grid=(n_var, NA_H),
        in_specs=[pl.BlockSpec((None, NA_DROWS + 1, GRID_W, GRID_W), lambda v, h, didx: (h, 0, 0, 0))],
        out_specs=pl.BlockSpec((None, None, NA_NQ, NA_NK), lambda v, h, didx: (v, h, 0, 0)))
    return pl.pallas_call(
        _na_bias_kernel,
        grid_spec=grid_spec,
        out_shape=jax.ShapeDtypeStruct((n_var, NA_H, NA_NQ, NA_NK), F32),
        compiler_params=_cparams(("parallel", "parallel")),
        name="na_bias",
    )(jnp.asarray(didx), t2)


def _na_attn_kernel(var_ref, q_ref, k_ref, v_ref, kc_ref, vc_ref, tbl_ref, o_ref):
    del var_ref
    blk = pl.program_id(1)
    k0 = pl.multiple_of(_na_key_row0(blk) * GRID_W, GRID_W)
    lane = lax.broadcasted_iota(jnp.int32, (NA_NQ, 128), 1)
    for j in range(NA_H // 2):
        sl = slice(j * 128, (j + 1) * 128)
        q = q_ref[:, sl]
        kt = k_ref[pl.ds(k0, NA_NK), sl]
        vt = v_ref[pl.ds(k0, NA_NK), sl]
        kct = kc_ref[:, sl]
        vct = vc_ref[:, sl]
        outs = []
        for hh in range(2):
            qm = q * _head_mask(hh)
            s_loc = lax.dot_general(qm, kt, (((1,), (1,)), ((), ())), preferred_element_type=F32)
            s_ctx = lax.dot_general(qm, kct, (((1,), (1,)), ((), ())), preferred_element_type=F32)
            tb = tbl_ref[2 * j + hh]
            s_loc = jnp.where(tb > 0.5 * NEG_BIG, s_loc + tb, NEG_BIG)
            m = jnp.maximum(jnp.max(s_loc, axis=-1, keepdims=True), jnp.max(s_ctx, axis=-1, keepdims=True))
            p_loc = jnp.exp(s_loc - m)
            p_ctx = jnp.exp(s_ctx - m)
            l = jnp.sum(p_loc, axis=-1, keepdims=True) + jnp.sum(p_ctx, axis=-1, keepdims=True)
            o = (jnp.dot(p_loc.astype(BF16), vt, preferred_element_type=F32)
                 + jnp.dot(p_ctx.astype(BF16), vct, preferred_element_type=F32))
            outs.append(o / l)
        o_ref[:, sl] = jnp.where(lane < NA_D, outs[0], outs[1])


def _na_attn(z, cache_k16, cache_v16, tbl, layer):
    base_q = N_CTX // NA_NQ
    base_t = N_CTX // LAT_T
    ctx_spec = pl.BlockSpec((None, None, PAST, WIDTH), lambda b, r, var: (b, layer, 0, 0))
    grid_spec = pltpu.PrefetchScalarGridSpec(
        num_scalar_prefetch=1,
        grid=(N_LAT_B, NA_BLOCKS),
        in_specs=[pl.BlockSpec((NA_NQ, WIDTH), lambda b, r, var: (base_q + b * NA_BLOCKS + r, COL_CQ)),
                  pl.BlockSpec((LAT_T, WIDTH), lambda b, r, var: (base_t + b, COL_CK)),
                  pl.BlockSpec((LAT_T, WIDTH), lambda b, r, var: (base_t + b, COL_CV)),
                  ctx_spec, ctx_spec,
                  pl.BlockSpec((None, NA_H, NA_NQ, NA_NK), lambda b, r, var: (var[r], 0, 0, 0))],
        out_specs=pl.BlockSpec((NA_NQ, WIDTH), lambda b, r, var: (b * NA_BLOCKS + r, 0)))
    return pl.pallas_call(
        _na_attn_kernel,
        grid_spec=grid_spec,
        out_shape=jax.ShapeDtypeStruct((N_LAT, WIDTH), F32),
        compiler_params=_cparams(("parallel", "arbitrary")),
        name="na_attn",
    )(jnp.asarray(_na_geometry()[0]), z, z, z, cache_k16, cache_v16, tbl)


MERGE_BM = 512
MERGE_SLABS = 2


def _merge_kernel(*refs, with_router, n_x):
    x_refs, refs = refs[:n_x], refs[n_x:]
    (ofc_ref, obc_ref, occ_ref, ofl_ref, obl_ref, ocl_ref, ag_ref, op_ref, h16_ref, wg0_ref, wg1_ref, m_ref,
     hn_ref, nf_ref, wa_ref, wb_ref, wc_ref, wo_ref) = refs[:18]
    if with_router:
        rhi_ref, rlo_ref, xn_ref, h2_ref, route_ref = refs[18:]
    else:
        xn_ref, h2_ref = refs[18:]
    is_ctx = pl.program_id(0) < N_CTX // MERGE_BM
    for sl in range(MERGE_SLABS):
        rs = slice(sl * MERGE_BM // MERGE_SLABS, (sl + 1) * MERGE_BM // MERGE_SLABS)
        o = jnp.where(is_ctx, ofc_ref[rs, :] + obc_ref[rs, :], ofl_ref[rs, :] + obl_ref[rs, :])
        oc = jnp.where(is_ctx, occ_ref[rs, :], ocl_ref[rs, :])
        parts = []
        for h in range(HG_H):
            oh = o[:, h * HG_D:(h + 1) * HG_D]
            parts.append(oh * lax.rsqrt(jnp.mean(oh * oh, axis=-1, keepdims=True) + EPS))
        oa = jnp.concatenate(parts, axis=1) * hn_ref[...] * _silu(ag_ref[rs, :].astype(F32))
        h16 = h16_ref[rs, :]
        sg0 = _sigmoid(jnp.dot(h16, wg0_ref[...], preferred_element_type=F32).astype(BF16)).astype(F32)
        sg1 = _sigmoid(jnp.dot(h16, wg1_ref[...], preferred_element_type=F32).astype(BF16)).astype(F32)
        sga = sg0[:, :D]
        sgb = jnp.concatenate([sg0[:, D:], sg1[:, :WIDTH]], axis=1)
        sgc = sg1[:, WIDTH:]
        mix = (sga * jnp.dot(oa.astype(BF16), wa_ref[...], preferred_element_type=F32)
               + sgb * jnp.dot(op_ref[rs, :].astype(BF16), wb_ref[...], preferred_element_type=F32)
               + sgc * jnp.dot(oc.astype(BF16), wc_ref[...], preferred_element_type=F32))
        x = x_refs[0][rs, :] if len(x_refs) == 1 else jnp.where(is_ctx, x_refs[0][rs, :], x_refs[1][rs, :])
        xn = x + m_ref[2:3, :] * jnp.dot(mix.astype(BF16), wo_ref[...], preferred_element_type=F32)
        xn_ref[rs, :] = xn
        y = xn * lax.rsqrt(jnp.mean(xn * xn, axis=-1, keepdims=True) + EPS) * nf_ref[...]
        h2 = y * (1.0 + m_ref[4:5, :]) + m_ref[3:4, :]
        h2_ref[rs, :] = h2.astype(h2_ref.dtype)
        if not with_router:
            continue
        hhi = h2.astype(BF16)
        hlo = (h2 - hhi.astype(F32)).astype(BF16)
        logits = (jnp.dot(hhi, rhi_ref[...], preferred_element_type=F32)
                  + jnp.dot(hhi, rlo_ref[...], preferred_element_type=F32)
                  + jnp.dot(hlo, rhi_ref[...], preferred_element_type=F32))
        lane = lax.broadcasted_iota(jnp.int32, logits.shape, 1).astype(F32)
        lg = jnp.where(lane < N_EXP, logits, -jnp.inf)
        m1 = jnp.max(lg, axis=-1, keepdims=True)
        i1 = jnp.min(jnp.where(lg == m1, lane, 128.0), axis=-1, keepdims=True)
        lg2 = jnp.where(lane == i1, -jnp.inf, lg)
        m2 = jnp.max(lg2, axis=-1, keepdims=True)
        i2 = jnp.min(jnp.where(lg2 == m2, lane, 128.0), axis=-1, keepdims=True)
        e = jnp.exp(m2 - m1)
        w1 = 1.0 / (1.0 + e)
        route_ref[rs, :] = (jnp.where(lane == ROUTE_I1, i1, 0.0) + jnp.where(lane == ROUTE_I2, i2, 0.0)
                            + jnp.where(lane == ROUTE_W1, w1, 0.0) + jnp.where(lane == ROUTE_W2, e * w1, 0.0))


def _merge(ctx_parts, lat_parts, z, h16, w_in16, layer, o_pool, x, mods, hgrn_norm_l, norm_ffn_l, wa, wb, wc, wo,
           router_split):
    bm = MERGE_BM
    glw = 1536
    nct = N_CTX // bm
    with_router = router_split is not None
    row = lambda w, col=0: pl.BlockSpec((bm, w), lambda i: (i, col))
    ctx_row = pl.BlockSpec((bm, WIDTH), lambda i: (jnp.minimum(i, nct - 1), 0))
    lat_row = pl.BlockSpec((bm, WIDTH), lambda i: (jnp.maximum(i - nct, 0), 0))
    const = lambda shape: pl.BlockSpec(shape, lambda i: (0,) * len(shape))
    x_specs, x_args = _stream_rows(x, bm)
    gate_w = lambda t: pl.BlockSpec((None, D, glw), lambda i: (layer, 0, GL_COL0 // glw + t))
    in_specs = x_specs + [ctx_row] * 3 + [lat_row] * 3 + [
        row(WIDTH, COL_AG), row(WIDTH),
        row(D), gate_w(0), gate_w(1),
        pl.BlockSpec((None, 6, D), lambda i: (_group_of_rows(i * bm), 0, 0)),
        const((1, WIDTH)), const((1, D)),
        const((WIDTH, D)), const((WIDTH, D)), const((WIDTH, D)), const((D, D))]
    args = x_args + list(ctx_parts) + list(lat_parts) + [
        z, o_pool, h16, w_in16, w_in16, mods, jnp.tile(hgrn_norm_l, HG_H).reshape(1, WIDTH),
        norm_ffn_l.reshape(1, D), wa, wb, wc, wo]
    out_specs = [row(D), row(D)]
    out_shape = [jax.ShapeDtypeStruct((N_TOK, D), F32),
                 jax.ShapeDtypeStruct((N_TOK, D), F32 if with_router else BF16)]
    if with_router:
        in_specs += [const((D, 128)), const((D, 128))]
        args += list(router_split)
        out_specs.append(row(128))
        out_shape.append(jax.ShapeDtypeStruct((N_TOK, 128), F32))
    return pl.pallas_call(
        functools.partial(_merge_kernel, with_router=with_router, n_x=len(x_args)),
        grid=(N_TOK // bm,),
        in_specs=in_specs,
        out_specs=out_specs,
        out_shape=out_shape,
        compiler_params=_cparams(("parallel",)),
        name="merge_route" if with_router else "merge",
    )(*args)


def _ffn_kernel(h_ref, x_ref, m_ref, wa_ref, wb_ref, wo_ref, o_ref):
    h = h_ref[...]
    a = jnp.dot(h, wa_ref[...], preferred_element_type=F32)
    b = jnp.dot(h, wb_ref[...], preferred_element_type=F32)
    g = (_silu(a) * b).astype(BF16)
    o_ref[...] = x_ref[...] + m_ref[5:6, :] * jnp.dot(g, wo_ref[...], preferred_element_type=F32)


FFN_BM = 512


def _ffn(h2, x, mods, w_in, w_out):
    bm = FFN_BM
    resident = pl.Buffered(1)
    return pl.pallas_call(
        _ffn_kernel,
        grid=(N_TOK // bm,),
        in_specs=[pl.BlockSpec((bm, D), lambda i: (i, 0)),
                  pl.BlockSpec((bm, D), lambda i: (i, 0)),
                  pl.BlockSpec((None, 6, D), lambda i: (_group_of_rows(i * bm), 0, 0)),
                  pl.BlockSpec((None, D, F_DENSE), lambda i: (0, 0, 0), pipeline_mode=resident),
                  pl.BlockSpec((None, D, F_DENSE), lambda i: (0, 0, 1), pipeline_mode=resident),
                  pl.BlockSpec((None, F_DENSE, D), lambda i: (0, 0, 0), pipeline_mode=resident)],
        out_specs=pl.BlockSpec((bm, D), lambda i: (i, 0)),
        out_shape=jax.ShapeDtypeStruct((N_TOK, D), F32),
        compiler_params=_cparams(("parallel",)),
        name="ffn",
    )(h2, x, mods, w_in, w_in, w_out)


MOE_BM = 1024
MOE_TF = 512
MOE_NF = F_EXP // MOE_TF
MOE_PAIRS = 2 * N_TOK
MOE_ROWS = MOE_PAIRS + N_EXP * MOE_BM
MOE_TILES = MOE_ROWS // MOE_BM
MOE_CH = -(-MOE_BM // MOE_NF)
MOE_MOVED = MOE_CH * MOE_NF
MOE_BUF = -(-MOE_MOVED // 8) * 8
ROW_DMA_PRIORITY = 1


def _moe_routing(route):
    i1 = route[:, ROUTE_I1].astype(jnp.int32)
    i2 = route[:, ROUTE_I2].astype(jnp.int32)
    ep = jnp.stack([i1, i2], axis=1).reshape(-1)
    onehot = (ep[:, None] == jnp.arange(N_EXP, dtype=jnp.int32)[None, :]).astype(jnp.int32)
    counts = jnp.sum(onehot, axis=0)
    padded = ((counts + MOE_BM - 1) // MOE_BM) * MOE_BM
    ends = jnp.cumsum(padded)
    starts = ends - padded
    first = jnp.cumsum(counts) - counts
    order = jnp.sort(ep * MOE_PAIRS + jnp.arange(MOE_PAIRS, dtype=jnp.int32)) & (MOE_PAIRS - 1)
    tile_row0 = jnp.arange(MOE_TILES, dtype=jnp.int32) * MOE_BM
    tile_active = (tile_row0 < ends[-1]).astype(jnp.int32)
    last_row0 = jnp.maximum(ends[-1] - MOE_BM, 0)
    tile_expert = jnp.sum((jnp.minimum(tile_row0, last_row0)[:, None] >= ends[None, :]).astype(jnp.int32), axis=1)
    tile_expert = jnp.minimum(tile_expert, N_EXP - 1)
    row = jnp.arange(MOE_ROWS, dtype=jnp.int32)
    row_e = jnp.repeat(tile_expert, MOE_BM)
    rank = row - starts[row_e]
    valid = (rank < counts[row_e]) & (row < ends[-1])
    pair = jnp.where(valid, order[jnp.clip(first[row_e] + rank, 0, MOE_PAIRS - 1)], -1)
    pair = pair.reshape(MOE_TILES, MOE_BM)
    tile_count = jnp.sum((pair >= 0).astype(jnp.int32), axis=1)
    src_tok = jnp.pad(jnp.maximum(pair, 0) >> 1, ((0, 0), (0, MOE_BUF - MOE_BM)))
    dst_row = jnp.where(pair >= 0, (pair & 1) * N_TOK + (pair >> 1), -1)
    dst_row = jnp.pad(dst_row, ((1, 0), (0, MOE_BUF - MOE_BM)), constant_values=-1)
    return (src_tok.reshape(MOE_TILES, 1, MOE_BUF), dst_row.reshape(MOE_TILES + 1, 1, MOE_BUF),
            tile_expert, tile_active, tile_count)


def _row(ref, r):
    return ref.at[pl.ds(r, 1), :]


def _moe_group_kernel(te_ref, ta_ref, tc_ref, dprv_ref, dcur_ref, gcur_ref, gnxt_ref, h2_ref, wa_ref, wb_ref, wo_ref,
                      out_ref, xbuf, ybuf, h_scr, gsem, ssem):
    del te_ref
    i = pl.program_id(0)
    f = pl.program_id(1)
    nt = pl.num_programs(0)
    s = i % 2
    o = 1 - s
    active = ta_ref[i] == 1
    prv_active = (i >= 1) & (ta_ref[jnp.maximum(i - 1, 0)] == 1)

    def gather_row(idx_ref, k, slot):
        pltpu.make_async_copy(_row(h2_ref, idx_ref[0, k]), _row(xbuf.at[slot], k),
                              gsem.at[slot]).start(priority=ROW_DMA_PRIORITY)

    def scatter_row(idx_ref, k, slot):
        dst = idx_ref[0, k]

        @pl.when(dst >= 0)
        def _():
            pltpu.make_async_copy(_row(ybuf.at[slot], k), _row(out_ref, dst),
                                  ssem.at[slot]).start(priority=ROW_DMA_PRIORITY)

    def wait_rows(sem, n):
        n = jnp.asarray(n, jnp.int32)
        n8 = pl.multiple_of((n >> 3) << 3, 8)

        @pl.when(n8 > 0)
        def _():
            pltpu.make_async_copy(xbuf.at[0, pl.ds(0, n8), :], ybuf.at[0, pl.ds(0, n8), :], sem).wait()

        def one(k, carry):
            pltpu.make_async_copy(_row(xbuf.at[0], 0), _row(ybuf.at[0], 0), sem).wait()
            return carry
        lax.fori_loop(0, n - n8, one, 0)

    def rows_loop(fn, n):
        def body(k, carry):
            fn(k)
            return carry
        lax.fori_loop(0, n, body, 0, unroll=8)

    @pl.when((i == 0) & (f == 0))
    def _():
        rows_loop(lambda k: gather_row(gcur_ref, k, 0), MOE_MOVED)

    @pl.when((f == 0) & (i >= 2))
    def _():
        wait_rows(ssem.at[s], tc_ref[jnp.maximum(i - 2, 0)])

    @pl.when((f == 0) & ((i == 0) | prv_active))
    def _():
        wait_rows(gsem.at[s], MOE_MOVED)

    @pl.when((f == 0) & active)
    def _():
        h_scr[...] = xbuf[s, 0:MOE_BM, :].astype(BF16)
        ybuf[s, 0:MOE_BM, :] = jnp.zeros((MOE_BM, D), F32)

    def step(rows):
        for u in range(MOE_CH):
            gather_row(gnxt_ref, f * MOE_CH + u, o)
        for u in range(MOE_CH):
            scatter_row(dprv_ref, f * MOE_CH + u, o)
        h = h_scr[0:rows, :]
        a = jnp.dot(h, wa_ref[...].astype(BF16), preferred_element_type=F32)
        b = jnp.dot(h, wb_ref[...].astype(BF16), preferred_element_type=F32)
        g = (_silu(a) * b).astype(BF16)
        ybuf[s, 0:rows, :] += jnp.dot(g, wo_ref[...].astype(BF16), preferred_element_type=F32)

    few = tc_ref[i] <= MOE_BM // 2

    @pl.when(active & jnp.logical_not(few))
    def _():
        step(MOE_BM)

    @pl.when(active & few)
    def _():
        step(MOE_BM // 2)

    @pl.when(jnp.logical_not(active) & prv_active)
    def _():
        rows_loop(lambda u: scatter_row(dprv_ref, f * MOE_CH + u, o), MOE_CH)

    @pl.when((i == nt - 1) & (f == pl.num_programs(1) - 1))
    def _():
        wait_rows(ssem.at[o], tc_ref[jnp.maximum(i - 1, 0)])
        rows_loop(lambda k: scatter_row(dcur_ref, k, s), MOE_BM)
        wait_rows(ssem.at[s], tc_ref[i])

        @pl.when(active)
        def _():
            wait_rows(gsem.at[o], MOE_MOVED)


def _moe_group(h2, src_tok, dst_row, tile_expert, tile_active, tile_count, w_in, w_out):
    nf = MOE_NF
    last = MOE_TILES - 1

    def fblk(i, f, ta):
        return jnp.where(ta[i] == 1, f, nf - 1)

    idx_spec = lambda off, hi: pl.BlockSpec((None, 1, MOE_BUF),
                                            lambda i, f, te, ta, tc: (jnp.minimum(i + off, hi), 0, 0),
                                            memory_space=pltpu.SMEM)
    grid_spec = pltpu.PrefetchScalarGridSpec(
        num_scalar_prefetch=3,
        grid=(MOE_TILES, nf),
        in_specs=[idx_spec(0, last + 1), idx_spec(1, last + 1),
                  idx_spec(0, last), idx_spec(1, last),
                  pl.BlockSpec(memory_space=pl.ANY),
                  pl.BlockSpec((None, None, D, MOE_TF), lambda i, f, te, ta, tc: (0, te[i], 0, fblk(i, f, ta))),
                  pl.BlockSpec((None, None, D, MOE_TF), lambda i, f, te, ta, tc: (0, te[i], 0, nf + fblk(i, f, ta))),
                  pl.BlockSpec((None, None, MOE_TF, D), lambda i, f, te, ta, tc: (0, te[i], fblk(i, f, ta), 0))],
        out_specs=pl.BlockSpec(memory_space=pl.ANY),
        scratch_shapes=[pltpu.VMEM((2, MOE_BUF, D), F32), pltpu.VMEM((2, MOE_BUF, D), F32),
                        pltpu.VMEM((MOE_BM, D), BF16),
                        pltpu.SemaphoreType.DMA((2,)), pltpu.SemaphoreType.DMA((2,))])
    return pl.pallas_call(
        _moe_group_kernel,
        grid_spec=grid_spec,
        out_shape=jax.ShapeDtypeStruct((MOE_PAIRS, D), F32),
        compiler_params=_cparams(("arbitrary", "arbitrary")),
        name="moe_group",
    )(tile_expert, tile_active, tile_count, dst_row, dst_row, src_tok, src_tok, h2, w_in, w_in, w_out)


MOE_COMBINE_BM = 512


def _moe_combine_kernel(route_ref, x_ref, m_ref, nf_ref, y1_ref, y2_ref, outc_ref, outl_ref):
    route = route_ref[...]
    f = (y1_ref[...] * route[:, ROUTE_W1:ROUTE_W1 + 1]
         + y2_ref[...] * route[:, ROUTE_W2:ROUTE_W2 + 1])
    xn = x_ref[...] + m_ref[5:6, :] * f
    y = xn * lax.rsqrt(jnp.mean(xn * xn, axis=-1, keepdims=True) + EPS) * nf_ref[...]
    is_ctx = pl.program_id(0) < N_CTX // MOE_COMBINE_BM

    @pl.when(is_ctx)
    def _():
        outc_ref[...] = y

    @pl.when(jnp.logical_not(is_ctx))
    def _():
        outl_ref[...] = y


def _moe_combine(ys, route, x, mods, norm_final):
    bm = MOE_COMBINE_BM
    nt = N_TOK // bm
    nct = N_CTX // bm
    return pl.pallas_call(
        _moe_combine_kernel,
        grid=(nt,),
        in_specs=[pl.BlockSpec((bm, 128), lambda i: (i, 0)),
                  pl.BlockSpec((bm, D), lambda i: (i, 0)),
                  pl.BlockSpec((None, 6, D), lambda i: (_group_of_rows(i * bm), 0, 0)),
                  pl.BlockSpec((1, D), lambda i: (0, 0)),
                  pl.BlockSpec((bm, D), lambda i: (i, 0)),
                  pl.BlockSpec((bm, D), lambda i: (nt + i, 0))],
        out_specs=[pl.BlockSpec((bm, D), lambda i: (jnp.minimum(i, nct - 1), 0)),
                   pl.BlockSpec((bm, D), lambda i: (jnp.maximum(i - nct, 0), 0))],
        out_shape=[jax.ShapeDtypeStruct((N_CTX, D), F32), jax.ShapeDtypeStruct((N_LAT, D), F32)],
        compiler_params=_cparams(("arbitrary",)),
        name="moe_combine",
    )(route, x, mods, norm_final.reshape(1, D), ys, ys)


def _moe(h2, route, x, mods, w_in, w_out, norm_final):
    src_tok, dst_row, tile_expert, tile_active, tile_count = _moe_routing(route)
    ys = _moe_group(h2, src_tok, dst_row, tile_expert, tile_active, tile_count, w_in, w_out)
    return _moe_combine(ys, route, x, mods, norm_final)


def _hgrn_lower_bounds(lb_param):
    sm = jax.nn.softmax(lb_param.astype(F32), axis=0)
    return jnp.cumsum(sm, axis=0) - sm[0:1]


def kernel(x_prompt, x_sample, cache_k, cache_v, state_hgrn, c, c_ctx, w_ada, b_ada, norm_mix, norm_ffn,
           w_in, hgrn_lb, hgrn_norm, pool_w, pool_scale, rpb, w_branch_a, w_branch_b, w_branch_c, w_out,
           ffn_w_in, ffn_w_out, router, moe_w_in, moe_w_out, norm_final):
    x = (x_prompt.reshape(N_CTX, D), x_sample.reshape(N_LAT, D))
    cond8 = jnp.concatenate([c_ctx[None], c, jnp.zeros((5, D), F32)], axis=0)
    mods = _adaln(cond8, w_ada, b_ada)[:, :3].reshape(DEPTH, 3, 6, D)
    lbs = _hgrn_lower_bounds(hgrn_lb)
    ck = cache_k.reshape(N_LAT_B, DEPTH, PAST, WIDTH).astype(BF16)
    cv = cache_v.reshape(N_LAT_B, DEPTH, PAST, WIDTH).astype(BF16)
    w_in16 = w_in.astype(BF16)
    router_pad = jnp.pad(router[0], ((0, 0), (0, 128 - N_EXP)))
    r_hi = router_pad.astype(BF16)
    r_lo = (router_pad - r_hi.astype(F32)).astype(BF16)

    ks, vs, ss = [], [], []
    for l in range(DEPTH):
        z, zg, k_ctx, v_ctx, h16 = _inproj(x, norm_mix[l], mods[l], w_in16, l)
        of_c, ob_c, s_ctx = _hgrn(z, zg, lbs[l], None, 0, N_CTX_B, CTX_T)
        of_l, ob_l, _ = _hgrn(z, zg, lbs[l], state_hgrn[:, l], N_CTX, N_LAT_B, LAT_T)
        o_pool = _pool(z, pool_w[l].astype(BF16), pool_scale[l])
        oc_c = _ctx_attn(z)
        oc_l = _na_attn(z, ck, cv, _na_bias_table(rpb[l]), l)
        merged = _merge((of_c, ob_c, oc_c), (of_l, ob_l, oc_l), z, h16, w_in16, l, o_pool, x, mods[l],
                        hgrn_norm[l], norm_ffn[l],
                        w_branch_a[l].astype(BF16), w_branch_b[l].astype(BF16),
                        w_branch_c[l].astype(BF16), w_out[l].astype(BF16),
                        (r_hi, r_lo) if l % 2 == 1 else None)
        if l % 2 == 0:
            x, h2 = merged
            x = _ffn(h2, x, mods[l], ffn_w_in.astype(BF16), ffn_w_out.astype(BF16))
        else:
            x, h2, route = merged
            y_ctx, y_lat = _moe(h2, route, x, mods[l], moe_w_in, moe_w_out, norm_final)
        ks.append(k_ctx)
        vs.append(v_ctx)
        ss.append(s_ctx)

    def cache(per_layer):
        return jnp.concatenate([a.reshape(N_CTX_B, 1, CTX_T, NA_H, NA_D) for a in per_layer], axis=1)

    y_prompt = y_ctx.reshape(N_CTX_B, CTX_T, D)
    y_sample = y_lat.reshape(N_LAT_B, LAT_T, D)
    return (y_prompt, y_sample, cache(ks), cache(vs), jnp.stack(ss, axis=1))
```

```python
import functools

import numpy as np
import jax
import jax.numpy as jnp
from jax import lax
from jax.experimental import pallas as pl
from jax.experimental.pallas import tpu as pltpu

F32 = jnp.float32
BF16 = jnp.bfloat16

D = 1024
N_CTX_B, CTX_T = 32, 256
N_LAT_B, LAT_T = 2, 4096
N_CTX = N_CTX_B * CTX_T
N_LAT = N_LAT_B * LAT_T
N_TOK = N_CTX + N_LAT
DEPTH = 2
GRID_W = 64
GRID_ROWS = LAT_T // GRID_W
PAST = 512
HG_H, HG_D = 4, 128
WIDTH = 512
NA_H, NA_D = 8, 64
WIN_ROWS, WIN_COLS = 8, 16
POOL_WINDOWS = (2, 4, 8, 16)
IN_COLS = 7680
F_DENSE = 2816
N_EXP = 8
F_EXP = 3584
GATE_CLIP = 30.0
EPS = 1e-6
NEG_BIG = -1e30

COL_AQ, COL_AFF, COL_AFB, COL_AI, COL_AG, COL_BU, COL_CQ, COL_CK, COL_CV = range(9)
GL_COL0 = 9 * WIDTH

ROUTE_I1, ROUTE_I2, ROUTE_W1, ROUTE_W2 = 8, 9, 10, 11

VMEM_LIMIT = 56 * 1024 * 1024


def _cparams(sem):
    return pltpu.CompilerParams(dimension_semantics=sem, vmem_limit_bytes=VMEM_LIMIT)


def _sigmoid(x):
    return 1.0 / (1.0 + jnp.exp(-x))


def _silu(x):
    return x / (1.0 + jnp.exp(-x))


def _group_of_rows(row0):
    return jnp.maximum(row0 - N_CTX + LAT_T, 0) // LAT_T


def _adaln_kernel(c_ref, w_ref, b_ref, o_ref):
    s = _silu(c_ref[...]).astype(BF16)
    o_ref[...] = jnp.dot(s, w_ref[...].astype(BF16), preferred_element_type=F32) + b_ref[...]


def _adaln(cond8, w_ada, b_ada):
    tn = 1536
    return pl.pallas_call(
        _adaln_kernel,
        grid=(DEPTH, 6 * D // tn),
        in_specs=[pl.BlockSpec((8, D), lambda l, j: (0, 0)),
                  pl.BlockSpec((None, D, tn), lambda l, j: (l, 0, j)),
                  pl.BlockSpec((None, 1, tn), lambda l, j: (l, 0, j))],
        out_specs=pl.BlockSpec((None, 8, tn), lambda l, j: (l, 0, j)),
        out_shape=jax.ShapeDtypeStruct((DEPTH, 8, 6 * D), F32),
        compiler_params=_cparams(("parallel", "parallel")),
        name="adaln",
    )(cond8, w_ada, b_ada.reshape(DEPTH, 1, 6 * D))


INPROJ_BM = 512


def _stream_rows(x, bm):
    if not isinstance(x, tuple):
        specs = [pl.BlockSpec((bm, D), lambda i, *_: (i, 0))]
        return specs, [x]
    nct = N_CTX // bm
    specs = [pl.BlockSpec((bm, D), lambda i, *_: (jnp.minimum(i, nct - 1), 0)),
             pl.BlockSpec((bm, D), lambda i, *_: (jnp.maximum(i - nct, 0), 0))]
    return specs, list(x)


def _read_stream_rows(x_refs, bm):
    if len(x_refs) == 1:
        return x_refs[0][...]
    return jnp.where(pl.program_id(0) < N_CTX // bm, x_refs[0][...], x_refs[1][...])


def _inproj_kernel(*refs):
    nw_ref, m_ref, w_ref, z_ref, zg_ref, k_ref, v_ref, h_ref = refs[-8:]
    x_refs = refs[:-8]
    x = _read_stream_rows(x_refs, INPROJ_BM)
    y = x * lax.rsqrt(jnp.mean(x * x, axis=-1, keepdims=True) + EPS) * nw_ref[...]
    h = (y * (1.0 + m_ref[1:2, :]) + m_ref[0:1, :]).astype(BF16)
    h_ref[...] = h
    acc = jnp.dot(h, w_ref[...], preferred_element_type=F32)
    z_ref[...] = acc.astype(BF16)
    zg_ref[...] = acc[:, COL_AFF * WIDTH:(COL_AFB + 1) * WIDTH]

    @pl.when(pl.program_id(0) < N_CTX // INPROJ_BM)
    def _():
        k_ref[...] = acc[:, COL_CK * WIDTH:(COL_CK + 1) * WIDTH]
        v_ref[...] = acc[:, COL_CV * WIDTH:(COL_CV + 1) * WIDTH]


def _inproj(x, norm_w, mods, w_in16, layer):
    bm = INPROJ_BM
    nct = N_CTX // bm
    x_specs, x_args = _stream_rows(x, bm)
    return pl.pallas_call(
        _inproj_kernel,
        grid=(N_TOK // bm,),
        in_specs=x_specs + [
            pl.BlockSpec((1, D), lambda i: (0, 0)),
            pl.BlockSpec((None, 6, D), lambda i: (_group_of_rows(i * bm), 0, 0)),
            pl.BlockSpec((None, D, GL_COL0), lambda i: (layer, 0, 0), pipeline_mode=pl.Buffered(1))],
        out_specs=[pl.BlockSpec((bm, GL_COL0), lambda i: (i, 0)),
                   pl.BlockSpec((bm, 2 * WIDTH), lambda i: (i, 0)),
                   pl.BlockSpec((bm, WIDTH), lambda i: (jnp.minimum(i, nct - 1), 0)),
                   pl.BlockSpec((bm, WIDTH), lambda i: (jnp.minimum(i, nct - 1), 0)),
                   pl.BlockSpec((bm, D), lambda i: (i, 0))],
        out_shape=[jax.ShapeDtypeStruct((N_TOK, GL_COL0), BF16),
                   jax.ShapeDtypeStruct((N_TOK, 2 * WIDTH), F32),
                   jax.ShapeDtypeStruct((N_CTX, WIDTH), F32),
                   jax.ShapeDtypeStruct((N_CTX, WIDTH), F32),
                   jax.ShapeDtypeStruct((N_TOK, D), BF16)],
        compiler_params=_cparams(("arbitrary",)),
        name="inproj",
    )(*x_args, norm_w.reshape(1, D), mods, w_in16)


HG_C = 128
HG_LEVELS = (4, 8, 16, 32, 64, 128)
LOG2_E = 1.4426950408889634


def _hgrn_level_ids(reverse):
    t = np.arange(HG_C)[:, None]
    s = np.arange(HG_C)[None, :]
    if reverse:
        t, s = s, t
    lev = np.full((HG_C, HG_C), -1, np.int32)
    lev[(t // 4 == s // 4) & (s <= t)] = 0
    for li, L in enumerate(HG_LEVELS[1:], start=1):
        m = (t // L == s // L) & (t % L >= L // 2) & (s % L < L // 2)
        lev[m] = li
    return lev


def _hgrn_ref_rows(b_scr, d, reverse):
    out = []
    r_lo, r_hi = (2, 6) if reverse else (1, 5)
    sub = lax.broadcasted_iota(jnp.int32, (8, WIDTH), 0)
    pieces = []
    for g in range(HG_C // 8):
        lo = jnp.broadcast_to(b_scr[d, 8 * g + r_lo:8 * g + r_lo + 1, :], (8, WIDTH))
        hi = jnp.broadcast_to(b_scr[d, 8 * g + r_hi:8 * g + r_hi + 1, :], (8, WIDTH))
        pieces.append(jnp.where(sub < 4, lo, hi))
    out.append(jnp.concatenate(pieces, axis=0))
    for L in HG_LEVELS[1:]:
        r = L // 2 - 1 if reverse else L // 2
        pieces = [jnp.broadcast_to(b_scr[d, L * g + r:L * g + r + 1, :], (L, WIDTH))
                  for g in range(HG_C // L)]
        out.append(pieces[0] if len(pieces) == 1 else jnp.concatenate(pieces, axis=0))
    return out


def _hgrn_kernel(*refs, has_init):
    if has_init:
        (qf_ref, ff_ref, vf_ref, qb_ref, fb_ref, vb_ref, lb_ref, levf_ref, levb_ref, s0_ref,
         of_ref, ob_ref, so_ref, st_scr, b_scr) = refs
    else:
        (qf_ref, ff_ref, vf_ref, qb_ref, fb_ref, vb_ref, lb_ref, levf_ref, levb_ref,
         of_ref, ob_ref, so_ref, st_scr, b_scr) = refs
    c = pl.program_id(1)
    C = HG_C

    @pl.when(c == 0)
    def _():
        for d in range(2):
            for h in range(HG_H):
                if has_init:
                    st_scr[d, h] = s0_ref[d, h]
                else:
                    st_scr[d, h] = jnp.zeros((HG_D, HG_D), F32)

    row = lax.broadcasted_iota(jnp.int32, (C, C), 0)
    col = lax.broadcasted_iota(jnp.int32, (C, C), 1)
    dirs = ((qf_ref, ff_ref, vf_ref, levf_ref, of_ref), (qb_ref, fb_ref, vb_ref, levb_ref, ob_ref))
    for d, (q_ref, f_ref, v_ref, lev_ref, o_ref) in enumerate(dirs):
        reverse = d == 1
        tri = jnp.where((col >= row) if reverse else (col <= row), 1.0, 0.0).astype(BF16)
        q = _silu(q_ref[...].astype(F32))
        fx = jnp.clip(f_ref[...], -GATE_CLIP, GATE_CLIP)
        e = jnp.exp(-fx)
        sig_pos = 1.0 / (1.0 + e)
        sig_neg = e * sig_pos
        lb = lb_ref[d:d + 1, :]
        lf = jnp.log(lb + (1.0 - lb) * sig_pos) * LOG2_E
        k = (1.0 - lb) * sig_neg
        q16 = q.astype(BF16)
        k16 = k.astype(BF16)
        hi = lf.astype(BF16)
        lo = (lf - hi.astype(F32)).astype(BF16)
        b = (jnp.dot(tri, hi, preferred_element_type=F32)
             + jnp.dot(tri, lo, preferred_element_type=F32))
        b_scr[d] = b
        refs_m = _hgrn_ref_rows(b_scr, d, reverse)
        qs, ks = [], []
        for li, m in enumerate(refs_m):
            dlt = b - m
            if li == 0:
                qs.append(q16 * jnp.exp2(dlt).astype(BF16))
                ks.append(k16 * jnp.exp2(-dlt).astype(BF16))
            else:
                fac = jnp.exp2(-jnp.abs(dlt)).astype(BF16)
                qs.append(q16 * fac)
                ks.append(k16 * fac)
        lev = lev_ref[...]
        b_end = b[0:1, :] if reverse else b[C - 1:C, :]
        q_in = q16 * jnp.exp2(b).astype(BF16)
        k_out = k16 * jnp.exp2(b_end - b).astype(BF16)
        dec = jnp.exp2(b_end)
        vb16 = v_ref[...]
        for h in range(HG_H):
            sl = slice(h * HG_D, (h + 1) * HG_D)
            a = jnp.zeros((C, C), F32)
            for li in range(len(HG_LEVELS)):
                p = lax.dot_general(qs[li][:, sl], ks[li][:, sl], (((1,), (1,)), ((), ())),
                                    preferred_element_type=F32)
                a = jnp.where(lev == li, p, a)
            vh = vb16[:, sl]
            st = st_scr[d, h]
            o = (jnp.dot(a.astype(BF16), vh, preferred_element_type=F32)
                 + lax.dot_general(q_in[:, sl], st.astype(BF16), (((1,), (1,)), ((), ())),
                                   preferred_element_type=F32))
            o_ref[:, sl] = o
            upd = lax.dot_general(vh, k_out[:, sl], (((0,), (0,)), ((), ())),
                                  preferred_element_type=F32)
            st_scr[d, h] = st * dec[:, sl] + upd

    @pl.when(c == pl.num_programs(1) - 1)
    def _():
        for d in range(2):
            for h in range(HG_H):
                so_ref[d, h] = st_scr[d, h]


def _hgrn(z, zg, lb, s0, row_off, n_b, seq_t):
    C = HG_C
    nc = seq_t // C
    base = row_off // C
    has_init = s0 is not None

    def fwd(col):
        return pl.BlockSpec((C, WIDTH), lambda b, c: (base + b * nc + c, col))

    def bwd(col):
        return pl.BlockSpec((C, WIDTH), lambda b, c: (base + b * nc + nc - 1 - c, col))

    full = lambda shape: pl.BlockSpec(shape, lambda b, c: (0,) * len(shape))
    in_specs = [fwd(COL_AQ), fwd(0), fwd(COL_AI), bwd(COL_AQ), bwd(1), bwd(COL_AI),
                full((2, WIDTH)), full((C, C)), full((C, C))]
    args = [z, zg, z, z, zg, z, lb, jnp.asarray(_hgrn_level_ids(False)), jnp.asarray(_hgrn_level_ids(True))]
    if has_init:
        in_specs.append(pl.BlockSpec((None, 2, HG_H, HG_D, HG_D), lambda b, c: (b, 0, 0, 0, 0)))
        args.append(s0)
    return pl.pallas_call(
        functools.partial(_hgrn_kernel, has_init=has_init),
        grid=(n_b, nc),
        in_specs=in_specs,
        out_specs=[pl.BlockSpec((C, WIDTH), lambda b, c: (b * nc + c, 0)),
                   pl.BlockSpec((C, WIDTH), lambda b, c: (b * nc + nc - 1 - c, 0)),
                   pl.BlockSpec((None, 2, HG_H, HG_D, HG_D), lambda b, c: (b, 0, 0, 0, 0))],
        out_shape=[jax.ShapeDtypeStruct((n_b * seq_t, WIDTH), F32),
                   jax.ShapeDtypeStruct((n_b * seq_t, WIDTH), F32),
                   jax.ShapeDtypeStruct((n_b, 2, HG_H, HG_D, HG_D), F32)],
        scratch_shapes=[pltpu.VMEM((2, HG_H, HG_D, HG_D), F32),
                        pltpu.VMEM((2, C, WIDTH), F32)],
        compiler_params=_cparams(("parallel", "arbitrary")),
        name="hgrn_lat" if has_init else "hgrn_ctx",
    )(*args)


POOL_T = 256
POOL_HALO = 16


def _pool_kernel(u_ref, up_ref, un_ref, w_ref, sc_ref, o_ref):
    i = pl.program_id(0)
    is_lat = i >= N_CTX // POOL_T
    seq_t = jnp.where(is_lat, LAT_T, CTX_T)
    t0 = jnp.where(is_lat, ((i - N_CTX // POOL_T) % (LAT_T // POOL_T)) * POOL_T, 0)
    TT, HL = POOL_T, POOL_HALO
    diff = lax.broadcasted_iota(jnp.int32, (TT, TT), 1) - lax.broadcasted_iota(jnp.int32, (TT, TT), 0)
    diff_h = lax.broadcasted_iota(jnp.int32, (TT, HL), 1) - lax.broadcasted_iota(jnp.int32, (TT, HL), 0)
    t_glob = t0 + lax.broadcasted_iota(jnp.int32, (TT, 128), 0)
    has_prev = t0 > 0
    has_next = t0 + TT < seq_t
    for g, w in enumerate(POOL_WINDOWS):
        half = w // 2
        sl = slice(g * 128, (g + 1) * 128)
        band = jnp.where((diff >= -half) & (diff < half), 1.0, 0.0).astype(BF16)
        dp = diff_h - HL
        band_p = jnp.where((dp >= -half) & (dp < half) & has_prev, 1.0, 0.0).astype(BF16)
        dn = diff_h + TT
        band_n = jnp.where((dn >= -half) & (dn < half) & has_next, 1.0, 0.0).astype(BF16)
        u = u_ref[:, sl]
        s = (jnp.dot(band, u, preferred_element_type=F32)
             + jnp.dot(band_p, up_ref[:, sl], preferred_element_type=F32)
             + jnp.dot(band_n, un_ref[:, sl], preferred_element_type=F32))
        cnt = (jnp.minimum(t_glob + half, seq_t) - jnp.maximum(t_glob - half, 0)).astype(F32)
        dd = s / cnt - u.astype(F32)
        y = jnp.dot(dd.astype(BF16), w_ref[g], preferred_element_type=F32)
        o_ref[:, sl] = y * sc_ref[:, sl]


def _pool(z, pool_w16, pool_scale):
    nt = N_TOK // POOL_T
    per = POOL_T // POOL_HALO
    return pl.pallas_call(
        _pool_kernel,
        grid=(nt,),
        in_specs=[pl.BlockSpec((POOL_T, WIDTH), lambda i: (i, COL_BU)),
                  pl.BlockSpec((POOL_HALO, WIDTH), lambda i: (jnp.maximum(i * per - 1, 0), COL_BU)),
                  pl.BlockSpec((POOL_HALO, WIDTH), lambda i: (jnp.minimum((i + 1) * per, nt * per - 1), COL_BU)),
                  pl.BlockSpec((4, 128, 128), lambda i: (0, 0, 0)),
                  pl.BlockSpec((1, WIDTH), lambda i: (0, 0))],
        out_specs=pl.BlockSpec((POOL_T, WIDTH), lambda i: (i, 0)),
        out_shape=jax.ShapeDtypeStruct((N_TOK, WIDTH), F32),
        compiler_params=_cparams(("parallel",)),
        name="pool",
    )(z, z, z, pool_w16, pool_scale.reshape(1, WIDTH))


def _head_mask(hh):
    lane = lax.broadcasted_iota(jnp.int32, (1, 128), 1)
    in_head = (lane >= hh * NA_D) & (lane < (hh + 1) * NA_D)
    return jnp.where(in_head, NA_D ** -0.5, 0.0).astype(BF16)


def _ctx_attn_kernel(q_ref, k_ref, v_ref, o_ref):
    lane = lax.broadcasted_iota(jnp.int32, (CTX_T, 128), 1)
    for j in range(NA_H // 2):
        sl = slice(j * 128, (j + 1) * 128)
        q = q_ref[:, sl]
        kt = k_ref[:, sl]
        vt = v_ref[:, sl]
        outs = []
        for hh in range(2):
            s = lax.dot_general(q * _head_mask(hh), kt, (((1,), (1,)), ((), ())), preferred_element_type=F32)
            m = jnp.max(s, axis=-1, keepdims=True)
            p = jnp.exp(s - m)
            l = jnp.sum(p, axis=-1, keepdims=True)
            outs.append(jnp.dot(p.astype(BF16), vt, preferred_element_type=F32) / l)
        o_ref[:, sl] = jnp.where(lane < NA_D, outs[0], outs[1])


def _ctx_attn(z):
    spec = lambda col: pl.BlockSpec((CTX_T, WIDTH), lambda b: (b, col))
    return pl.pallas_call(
        _ctx_attn_kernel,
        grid=(N_CTX_B,),
        in_specs=[spec(COL_CQ), spec(COL_CK), spec(COL_CV)],
        out_specs=pl.BlockSpec((CTX_T, WIDTH), lambda b: (b, 0)),
        out_shape=jax.ShapeDtypeStruct((N_CTX, WIDTH), F32),
        compiler_params=_cparams(("parallel",)),
        name="ctx_attn",
    )(z, z, z)


NA_QR = 4
NA_KR = 12
NA_NQ = NA_QR * GRID_W
NA_NK = NA_KR * GRID_W
NA_BLOCKS = GRID_ROWS // NA_QR


def _na_key_row0(blk):
    return jnp.clip(blk * NA_QR - WIN_ROWS // 2, 0, GRID_ROWS - NA_KR)


def _na_geometry():
    patterns, var_of_block = [], []
    for blk in range(NA_BLOCKS):
        r = blk * NA_QR + np.arange(NA_QR)[:, None]
        kr = int(np.clip(blk * NA_QR - WIN_ROWS // 2, 0, GRID_ROWS - NA_KR)) + np.arange(NA_KR)[None, :]
        rs = np.clip(r - WIN_ROWS // 2, 0, GRID_ROWS - WIN_ROWS)
        valid = (kr >= rs) & (kr < rs + WIN_ROWS)
        assert (valid.sum(axis=1) == WIN_ROWS).all(), "key rows must cover every query row's window"
        drow = np.where(valid, kr - r + WIN_ROWS - 1, 0)
        key = (drow.tobytes(), valid.tobytes())
        ids = [i for i, (k_, _, _) in enumerate(patterns) if k_ == key]
        if not ids:
            patterns.append((key, drow, valid))
            ids = [len(patterns) - 1]
        var_of_block.append(ids[0])
    drow = np.stack([p[1] for p in patterns])
    valid = np.stack([p[2] for p in patterns])
    return np.asarray(var_of_block, np.int32), drow, valid


NA_DROWS = 2 * WIN_ROWS - 1


def _na_bias_kernel(didx_ref, t2_ref, o_ref):
    v = pl.program_id(0)
    for qr in range(NA_QR):
        for kp in range(NA_KR // 2):
            base = (v * NA_QR + qr) * NA_KR + 2 * kp
            pair = jnp.concatenate([t2_ref[didx_ref[base]], t2_ref[didx_ref[base + 1]]], axis=1)
            o_ref[qr * GRID_W:(qr + 1) * GRID_W, kp * 128:(kp + 1) * 128] = pair


def _na_bias_table(rpb_l):
    _, drow, valid = _na_geometry()
    n_var = valid.shape[0]
    qc = np.arange(GRID_W)[:, None]
    kc = np.arange(GRID_W)[None, :]
    q0 = np.clip(qc - WIN_COLS // 2, 0, GRID_W - WIN_COLS)
    col_in = (kc >= q0) & (kc < q0 + WIN_COLS)
    dcol = np.clip(kc - qc, -(WIN_COLS - 1), WIN_COLS - 1) + WIN_COLS - 1
    oh_col = (dcol[None] == np.arange(2 * WIN_COLS - 1)[:, None, None]).astype(np.float32)
    t2 = jnp.einsum('hdc,cqk->hdqk', rpb_l, jnp.asarray(oh_col), precision=lax.Precision.HIGHEST)
    t2 = jnp.where(col_in[None, None], t2, NEG_BIG)
    t2 = jnp.concatenate([t2, jnp.full((NA_H, 1, GRID_W, GRID_W), NEG_BIG, F32)], axis=1)
    didx = np.where(valid, drow, NA_DROWS).astype(np.int32).reshape(-1)
    grid_spec = pltpu.PrefetchScalarGridSpec(
        num_scalar_prefetch=1,
        grid=(n_var, NA_H),
        in_specs=[pl.BlockSpec((None, NA_DROWS + 1, GRID_W, GRID_W), lambda v, h, didx: (h, 0, 0, 0))],
        out_specs=pl.BlockSpec((None, None, NA_NQ, NA_NK), lambda v, h, didx: (v, h, 0, 0)))
    return pl.pallas_call(
        _na_bias_kernel,
        grid_spec=grid_spec,
        out_shape=jax.ShapeDtypeStruct((n_var, NA_H, NA_NQ, NA_NK), F32),
        compiler_params=_cparams(("parallel", "parallel")),
        name="na_bias",
    )(jnp.asarray(didx), t2)


def _na_attn_kernel(var_ref, q_ref, k_ref, v_ref, kc_ref, vc_ref, tbl_ref, o_ref):
    del var_ref
    blk = pl.program_id(1)
    k0 = pl.multiple_of(_na_key_row0(blk) * GRID_W, GRID_W)
    lane = lax.broadcasted_iota(jnp.int32, (NA_NQ, 128), 1)
    for j in range(NA_H // 2):
        sl = slice(j * 128, (j + 1) * 128)
        q = q_ref[:, sl]
        kt = k_ref[pl.ds(k0, NA_NK), sl]
        vt = v_ref[pl.ds(k0, NA_NK), sl]
        kct = kc_ref[:, sl]
        vct = vc_ref[:, sl]
        outs = []
        for hh in range(2):
            qm = q * _head_mask(hh)
            s_loc = lax.dot_general(qm, kt, (((1,), (1,)), ((), ())), preferred_element_type=F32)
            s_ctx = lax.dot_general(qm, kct, (((1,), (1,)), ((), ())), preferred_element_type=F32)
            tb = tbl_ref[2 * j + hh]
            s_loc = jnp.where(tb > 0.5 * NEG_BIG, s_loc + tb, NEG_BIG)
            m = jnp.maximum(jnp.max(s_loc, axis=-1, keepdims=True), jnp.max(s_ctx, axis=-1, keepdims=True))
            p_loc = jnp.exp(s_loc - m)
            p_ctx = jnp.exp(s_ctx - m)
            l = jnp.sum(p_loc, axis=-1, keepdims=True) + jnp.sum(p_ctx, axis=-1, keepdims=True)
            o = (jnp.dot(p_loc.astype(BF16), vt, preferred_element_type=F32)
                 + jnp.dot(p_ctx.astype(BF16), vct, preferred_element_type=F32))
            outs.append(o / l)
        o_ref[:, sl] = jnp.where(lane < NA_D, outs[0], outs[1])


def _na_attn(z, cache_k16, cache_v16, tbl, layer):
    base_q = N_CTX // NA_NQ
    base_t = N_CTX // LAT_T
    ctx_spec = pl.BlockSpec((None, None, PAST, WIDTH), lambda b, r, var: (b, layer, 0, 0))
    grid_spec = pltpu.PrefetchScalarGridSpec(
        num_scalar_prefetch=1,
        grid=(N_LAT_B, NA_BLOCKS),
        in_specs=[pl.BlockSpec((NA_NQ, WIDTH), lambda b, r, var: (base_q + b * NA_BLOCKS + r, COL_CQ)),
                  pl.BlockSpec((LAT_T, WIDTH), lambda b, r, var: (base_t + b, COL_CK)),
                  pl.BlockSpec((LAT_T, WIDTH), lambda b, r, var: (base_t + b, COL_CV)),
                  ctx_spec, ctx_spec,
                  pl.BlockSpec((None, NA_H, NA_NQ, NA_NK), lambda b, r, var: (var[r], 0, 0, 0))],
        out_specs=pl.BlockSpec((NA_NQ, WIDTH), lambda b, r, var: (b * NA_BLOCKS + r, 0)))
    return pl.pallas_call(
        _na_attn_kernel,
        grid_spec=grid_spec,
        out_shape=jax.ShapeDtypeStruct((N_LAT, WIDTH), F32),
        compiler_params=_cparams(("parallel", "arbitrary")),
        name="na_attn",
    )(jnp.asarray(_na_geometry()[0]), z, z, z, cache_k16, cache_v16, tbl)


MERGE_BM = 512
MERGE_SLABS = 2


def _merge_kernel(*refs, with_router, n_x):
    x_refs, refs = refs[:n_x], refs[n_x:]
    (ofc_ref, obc_ref, occ_ref, ofl_ref, obl_ref, ocl_ref, ag_ref, op_ref, h16_ref, wg0_ref, wg1_ref, m_ref,
     hn_ref, nf_ref, wa_ref, wb_ref, wc_ref, wo_ref) = refs[:18]
    if with_router:
        rhi_ref, rlo_ref, xn_ref, h2_ref, route_ref = refs[18:]
    else:
        xn_ref, h2_ref = refs[18:]
    is_ctx = pl.program_id(0) < N_CTX // MERGE_BM
    for sl in range(MERGE_SLABS):
        rs = slice(sl * MERGE_BM // MERGE_SLABS, (sl + 1) * MERGE_BM // MERGE_SLABS)
        o = jnp.where(is_ctx, ofc_ref[rs, :] + obc_ref[rs, :], ofl_ref[rs, :] + obl_ref[rs, :])
        oc = jnp.where(is_ctx, occ_ref[rs, :], ocl_ref[rs, :])
        parts = []
        for h in range(HG_H):
            oh = o[:, h * HG_D:(h + 1) * HG_D]
            parts.append(oh * lax.rsqrt(jnp.mean(oh * oh, axis=-1, keepdims=True) + EPS))
        oa = jnp.concatenate(parts, axis=1) * hn_ref[...] * _silu(ag_ref[rs, :].astype(F32))
        h16 = h16_ref[rs, :]
        sg0 = _sigmoid(jnp.dot(h16, wg0_ref[...], preferred_element_type=F32).astype(BF16)).astype(F32)
        sg1 = _sigmoid(jnp.dot(h16, wg1_ref[...], preferred_element_type=F32).astype(BF16)).astype(F32)
        sga = sg0[:, :D]
        sgb = jnp.concatenate([sg0[:, D:], sg1[:, :WIDTH]], axis=1)
        sgc = sg1[:, WIDTH:]
        mix = (sga * jnp.dot(oa.astype(BF16), wa_ref[...], preferred_element_type=F32)
               + sgb * jnp.dot(op_ref[rs, :].astype(BF16), wb_ref[...], preferred_element_type=F32)
               + sgc * jnp.dot(oc.astype(BF16), wc_ref[...], preferred_element_type=F32))
        x = x_refs[0][rs, :] if len(x_refs) == 1 else jnp.where(is_ctx, x_refs[0][rs, :], x_refs[1][rs, :])
        xn = x + m_ref[2:3, :] * jnp.dot(mix.astype(BF16), wo_ref[...], preferred_element_type=F32)
        xn_ref[rs, :] = xn
        y = xn * lax.rsqrt(jnp.mean(xn * xn, axis=-1, keepdims=True) + EPS) * nf_ref[...]
        h2 = y * (1.0 + m_ref[4:5, :]) + m_ref[3:4, :]
        h2_ref[rs, :] = h2.astype(h2_ref.dtype)
        if not with_router:
            continue
        hhi = h2.astype(BF16)
        hlo = (h2 - hhi.astype(F32)).astype(BF16)
        logits = (jnp.dot(hhi, rhi_ref[...], preferred_element_type=F32)
                  + jnp.dot(hhi, rlo_ref[...], preferred_element_type=F32)
                  + jnp.dot(hlo, rhi_ref[...], preferred_element_type=F32))
        lane = lax.broadcasted_iota(jnp.int32, logits.shape, 1).astype(F32)
        lg = jnp.where(lane < N_EXP, logits, -jnp.inf)
        m1 = jnp.max(lg, axis=-1, keepdims=True)
        i1 = jnp.min(jnp.where(lg == m1, lane, 128.0), axis=-1, keepdims=True)
        lg2 = jnp.where(lane == i1, -jnp.inf, lg)
        m2 = jnp.max(lg2, axis=-1, keepdims=True)
        i2 = jnp.min(jnp.where(lg2 == m2, lane, 128.0), axis=-1, keepdims=True)
        e = jnp.exp(m2 - m1)
        w1 = 1.0 / (1.0 + e)
        route_ref[rs, :] = (jnp.where(lane == ROUTE_I1, i1, 0.0) + jnp.where(lane == ROUTE_I2, i2, 0.0)
                            + jnp.where(lane == ROUTE_W1, w1, 0.0) + jnp.where(lane == ROUTE_W2, e * w1, 0.0))


def _merge(ctx_parts, lat_parts, z, h16, w_in16, layer, o_pool, x, mods, hgrn_norm_l, norm_ffn_l, wa, wb, wc, wo,
           router_split):
    bm = MERGE_BM
    glw = 1536
    nct = N_CTX // bm
    with_router = router_split is not None
    row = lambda w, col=0: pl.BlockSpec((bm, w), lambda i: (i, col))
    ctx_row = pl.BlockSpec((bm, WIDTH), lambda i: (jnp.minimum(i, nct - 1), 0))
    lat_row = pl.BlockSpec((bm, WIDTH), lambda i: (jnp.maximum(i - nct, 0), 0))
    const = lambda shape: pl.BlockSpec(shape, lambda i: (0,) * len(shape))
    x_specs, x_args = _stream_rows(x, bm)
    gate_w = lambda t: pl.BlockSpec((None, D, glw), lambda i: (layer, 0, GL_COL0 // glw + t))
    in_specs = x_specs + [ctx_row] * 3 + [lat_row] * 3 + [
        row(WIDTH, COL_AG), row(WIDTH),
        row(D), gate_w(0), gate_w(1),
        pl.BlockSpec((None, 6, D), lambda i: (_group_of_rows(i * bm), 0, 0)),
        const((1, WIDTH)), const((1, D)),
        const((WIDTH, D)), const((WIDTH, D)), const((WIDTH, D)), const((D, D))]
    args = x_args + list(ctx_parts) + list(lat_parts) + [
        z, o_pool, h16, w_in16, w_in16, mods, jnp.tile(hgrn_norm_l, HG_H).reshape(1, WIDTH),
        norm_ffn_l.reshape(1, D), wa, wb, wc, wo]
    out_specs = [row(D), row(D)]
    out_shape = [jax.ShapeDtypeStruct((N_TOK, D), F32),
                 jax.ShapeDtypeStruct((N_TOK, D), F32 if with_router else BF16)]
    if with_router:
        in_specs += [const((D, 128)), const((D, 128))]
        args += list(router_split)
        out_specs.append(row(128))
        out_shape.append(jax.ShapeDtypeStruct((N_TOK, 128), F32))
    return pl.pallas_call(
        functools.partial(_merge_kernel, with_router=with_router, n_x=len(x_args)),
        grid=(N_TOK // bm,),
        in_specs=in_specs,
        out_specs=out_specs,
        out_shape=out_shape,
        compiler_params=_cparams(("parallel",)),
        name="merge_route" if with_router else "merge",
    )(*args)


def _ffn_kernel(h_ref, x_ref, m_ref, wa_ref, wb_ref, wo_ref, o_ref):
    h = h_ref[...]
    a = jnp.dot(h, wa_ref[...], preferred_element_type=F32)
    b = jnp.dot(h, wb_ref[...], preferred_element_type=F32)
    g = (_silu(a) * b).astype(BF16)
    o_ref[...] = x_ref[...] + m_ref[5:6, :] * jnp.dot(g, wo_ref[...], preferred_element_type=F32)


FFN_BM = 512


def _ffn(h2, x, mods, w_in, w_out):
    bm = FFN_BM
    resident = pl.Buffered(1)
    return pl.pallas_call(
        _ffn_kernel,
        grid=(N_TOK // bm,),
        in_specs=[pl.BlockSpec((bm, D), lambda i: (i, 0)),
                  pl.BlockSpec((bm, D), lambda i: (i, 0)),
                  pl.BlockSpec((None, 6, D), lambda i: (_group_of_rows(i * bm), 0, 0)),
                  pl.BlockSpec((None, D, F_DENSE), lambda i: (0, 0, 0), pipeline_mode=resident),
                  pl.BlockSpec((None, D, F_DENSE), lambda i: (0, 0, 1), pipeline_mode=resident),
                  pl.BlockSpec((None, F_DENSE, D), lambda i: (0, 0, 0), pipeline_mode=resident)],
        out_specs=pl.BlockSpec((bm, D), lambda i: (i, 0)),
        out_shape=jax.ShapeDtypeStruct((N_TOK, D), F32),
        compiler_params=_cparams(("parallel",)),
        name="ffn",
    )(h2, x, mods, w_in, w_in, w_out)


MOE_BM = 1024
MOE_TF = 512
MOE_NF = F_EXP // MOE_TF
MOE_PAIRS = 2 * N_TOK
MOE_ROWS = MOE_PAIRS + N_EXP * MOE_BM
MOE_TILES = MOE_ROWS // MOE_BM
MOE_CH = -(-MOE_BM // MOE_NF)
MOE_MOVED = MOE_CH * MOE_NF
MOE_BUF = -(-MOE_MOVED // 8) * 8
ROW_DMA_PRIORITY = 1


def _moe_routing(route):
    i1 = route[:, ROUTE_I1].astype(jnp.int32)
    i2 = route[:, ROUTE_I2].astype(jnp.int32)
    ep = jnp.stack([i1, i2], axis=1).reshape(-1)
    onehot = (ep[:, None] == jnp.arange(N_EXP, dtype=jnp.int32)[None, :]).astype(jnp.int32)
    counts = jnp.sum(onehot, axis=0)
    padded = ((counts + MOE_BM - 1) // MOE_BM) * MOE_BM
    ends = jnp.cumsum(padded)
    starts = ends - padded
    first = jnp.cumsum(counts) - counts
    order = jnp.sort(ep * MOE_PAIRS + jnp.arange(MOE_PAIRS, dtype=jnp.int32)) & (MOE_PAIRS - 1)
    tile_row0 = jnp.arange(MOE_TILES, dtype=jnp.int32) * MOE_BM
    tile_active = (tile_row0 < ends[-1]).astype(jnp.int32)
    last_row0 = jnp.maximum(ends[-1] - MOE_BM, 0)
    tile_expert = jnp.sum((jnp.minimum(tile_row0, last_row0)[:, None] >= ends[None, :]).astype(jnp.int32), axis=1)
    tile_expert = jnp.minimum(tile_expert, N_EXP - 1)
    row = jnp.arange(MOE_ROWS, dtype=jnp.int32)
    row_e = jnp.repeat(tile_expert, MOE_BM)
    rank = row - starts[row_e]
    valid = (rank < counts[row_e]) & (row < ends[-1])
    pair = jnp.where(valid, order[jnp.clip(first[row_e] + rank, 0, MOE_PAIRS - 1)], -1)
    pair = pair.reshape(MOE_TILES, MOE_BM)
    tile_count = jnp.sum((pair >= 0).astype(jnp.int32), axis=1)
    src_tok = jnp.pad(jnp.maximum(pair, 0) >> 1, ((0, 0), (0, MOE_BUF - MOE_BM)))
    dst_row = jnp.where(pair >= 0, (pair & 1) * N_TOK + (pair >> 1), -1)
    dst_row = jnp.pad(dst_row, ((1, 0), (0, MOE_BUF - MOE_BM)), constant_values=-1)
    return (src_tok.reshape(MOE_TILES, 1, MOE_BUF), dst_row.reshape(MOE_TILES + 1, 1, MOE_BUF),
            tile_expert, tile_active, tile_count)


def _row(ref, r):
    return ref.at[pl.ds(r, 1), :]


def _moe_group_kernel(te_ref, ta_ref, tc_ref, dprv_ref, dcur_ref, gcur_ref, gnxt_ref, h2_ref, wa_ref, wb_ref, wo_ref,
                      out_ref, xbuf, ybuf, h_scr, gsem, ssem):
    del te_ref
    i = pl.program_id(0)
    f = pl.program_id(1)
    nt = pl.num_programs(0)
    s = i % 2
    o = 1 - s
    active = ta_ref[i] == 1
    prv_active = (i >= 1) & (ta_ref[jnp.maximum(i - 1, 0)] == 1)

    def gather_row(idx_ref, k, slot):
        pltpu.make_async_copy(_row(h2_ref, idx_ref[0, k]), _row(xbuf.at[slot], k),
                              gsem.at[slot]).start(priority=ROW_DMA_PRIORITY)

    def scatter_row(idx_ref, k, slot):
        dst = idx_ref[0, k]

        @pl.when(dst >= 0)
        def _():
            pltpu.make_async_copy(_row(ybuf.at[slot], k), _row(out_ref, dst),
                                  ssem.at[slot]).start(priority=ROW_DMA_PRIORITY)

    def wait_rows(sem, n):
        n = jnp.asarray(n, jnp.int32)
        n8 = pl.multiple_of((n >> 3) << 3, 8)

        @pl.when(n8 > 0)
        def _():
            pltpu.make_async_copy(xbuf.at[0, pl.ds(0, n8), :], ybuf.at[0, pl.ds(0, n8), :], sem).wait()

        def one(k, carry):
            pltpu.make_async_copy(_row(xbuf.at[0], 0), _row(ybuf.at[0], 0), sem).wait()
            return carry
        lax.fori_loop(0, n - n8, one, 0)

    def rows_loop(fn, n):
        def body(k, carry):
            fn(k)
            return carry
        lax.fori_loop(0, n, body, 0, unroll=8)

    @pl.when((i == 0) & (f == 0))
    def _():
        rows_loop(lambda k: gather_row(gcur_ref, k, 0), MOE_MOVED)

    @pl.when((f == 0) & (i >= 2))
    def _():
        wait_rows(ssem.at[s], tc_ref[jnp.maximum(i - 2, 0)])

    @pl.when((f == 0) & ((i == 0) | prv_active))
    def _():
        wait_rows(gsem.at[s], MOE_MOVED)

    @pl.when((f == 0) & active)
    def _():
        h_scr[...] = xbuf[s, 0:MOE_BM, :].astype(BF16)
        ybuf[s, 0:MOE_BM, :] = jnp.zeros((MOE_BM, D), F32)

    def step(rows):
        for u in range(MOE_CH):
            gather_row(gnxt_ref, f * MOE_CH + u, o)
        for u in range(MOE_CH):
            scatter_row(dprv_ref, f * MOE_CH + u, o)
        h = h_scr[0:rows, :]
        a = jnp.dot(h, wa_ref[...].astype(BF16), preferred_element_type=F32)
        b = jnp.dot(h, wb_ref[...].astype(BF16), preferred_element_type=F32)
        g = (_silu(a) * b).astype(BF16)
        ybuf[s, 0:rows, :] += jnp.dot(g, wo_ref[...].astype(BF16), preferred_element_type=F32)

    few = tc_ref[i] <= MOE_BM // 2

    @pl.when(active & jnp.logical_not(few))
    def _():
        step(MOE_BM)

    @pl.when(active & few)
    def _():
        step(MOE_BM // 2)

    @pl.when(jnp.logical_not(active) & prv_active)
    def _():
        rows_loop(lambda u: scatter_row(dprv_ref, f * MOE_CH + u, o), MOE_CH)

    @pl.when((i == nt - 1) & (f == pl.num_programs(1) - 1))
    def _():
        wait_rows(ssem.at[o], tc_ref[jnp.maximum(i - 1, 0)])
        rows_loop(lambda k: scatter_row(dcur_ref, k, s), MOE_BM)
        wait_rows(ssem.at[s], tc_ref[i])

        @pl.when(active)
        def _():
            wait_rows(gsem.at[o], MOE_MOVED)


def _moe_group(h2, src_tok, dst_row, tile_expert, tile_active, tile_count, w_in, w_out):
    nf = MOE_NF
    last = MOE_TILES - 1

    def fblk(i, f, ta):
        return jnp.where(ta[i] == 1, f, nf - 1)

    idx_spec = lambda off, hi: pl.BlockSpec((None, 1, MOE_BUF),
                                            lambda i, f, te, ta, tc: (jnp.minimum(i + off, hi), 0, 0),
                                            memory_space=pltpu.SMEM)
    grid_spec = pltpu.PrefetchScalarGridSpec(
        num_scalar_prefetch=3,
        grid=(MOE_TILES, nf),
        in_specs=[idx_spec(0, last + 1), idx_spec(1, last + 1),
                  idx_spec(0, last), idx_spec(1, last),
                  pl.BlockSpec(memory_space=pl.ANY),
                  pl.BlockSpec((None, None, D, MOE_TF), lambda i, f, te, ta, tc: (0, te[i], 0, fblk(i, f, ta))),
                  pl.BlockSpec((None, None, D, MOE_TF), lambda i, f, te, ta, tc: (0, te[i], 0, nf + fblk(i, f, ta))),
                  pl.BlockSpec((None, None, MOE_TF, D), lambda i, f, te, ta, tc: (0, te[i], fblk(i, f, ta), 0))],
        out_specs=pl.BlockSpec(memory_space=pl.ANY),
        scratch_shapes=[pltpu.VMEM((2, MOE_BUF, D), F32), pltpu.VMEM((2, MOE_BUF, D), F32),
                        pltpu.VMEM((MOE_BM, D), BF16),
                        pltpu.SemaphoreType.DMA((2,)), pltpu.SemaphoreType.DMA((2,))])
    return pl.pallas_call(
        _moe_group_kernel,
        grid_spec=grid_spec,
        out_shape=jax.ShapeDtypeStruct((MOE_PAIRS, D), F32),
        compiler_params=_cparams(("arbitrary", "arbitrary")),
        name="moe_group",
    )(tile_expert, tile_active, tile_count, dst_row, dst_row, src_tok, src_tok, h2, w_in, w_in, w_out)


MOE_COMBINE_BM = 512


def _moe_combine_kernel(route_ref, x_ref, m_ref, nf_ref, y1_ref, y2_ref, outc_ref, outl_ref):
    route = route_ref[...]
    f = (y1_ref[...] * route[:, ROUTE_W1:ROUTE_W1 + 1]
         + y2_ref[...] * route[:, ROUTE_W2:ROUTE_W2 + 1])
    xn = x_ref[...] + m_ref[5:6, :] * f
    y = xn * lax.rsqrt(jnp.mean(xn * xn, axis=-1, keepdims=True) + EPS) * nf_ref[...]
    is_ctx = pl.program_id(0) < N_CTX // MOE_COMBINE_BM

    @pl.when(is_ctx)
    def _():
        outc_ref[...] = y

    @pl.when(jnp.logical_not(is_ctx))
    def _():
        outl_ref[...] = y


def _moe_combine(ys, route, x, mods, norm_final):
    bm = MOE_COMBINE_BM
    nt = N_TOK // bm
    nct = N_CTX // bm
    return pl.pallas_call(
        _moe_combine_kernel,
        grid=(nt,),
        in_specs=[pl.BlockSpec((bm, 128), lambda i: (i, 0)),
                  pl.BlockSpec((bm, D), lambda i: (i, 0)),
                  pl.BlockSpec((None, 6, D), lambda i: (_group_of_rows(i * bm), 0, 0)),
                  pl.BlockSpec((1, D), lambda i: (0, 0)),
                  pl.BlockSpec((bm, D), lambda i: (i, 0)),
                  pl.BlockSpec((bm, D), lambda i: (nt + i, 0))],
        out_specs=[pl.BlockSpec((bm, D), lambda i: (jnp.minimum(i, nct - 1), 0)),
                   pl.BlockSpec((bm, D), lambda i: (jnp.maximum(i - nct, 0), 0))],
        out_shape=[jax.ShapeDtypeStruct((N_CTX, D), F32), jax.ShapeDtypeStruct((N_LAT, D), F32)],
        compiler_params=_cparams(("arbitrary",)),
        name="moe_combine",
    )(route, x, mods, norm_final.reshape(1, D), ys, ys)


def _moe(h2, route, x, mods, w_in, w_out, norm_final):
    src_tok, dst_row, tile_expert, tile_active, tile_count = _moe_routing(route)
    ys = _moe_group(h2, src_tok, dst_row, tile_expert, tile_active, tile_count, w_in, w_out)
    return _moe_combine(ys, route, x, mods, norm_final)


def _hgrn_lower_bounds(lb_param):
    sm = jax.nn.softmax(lb_param.astype(F32), axis=0)
    return jnp.cumsum(sm, axis=0) - sm[0:1]


def kernel(x_prompt, x_sample, cache_k, cache_v, state_hgrn, c, c_ctx, w_ada, b_ada, norm_mix, norm_ffn,
           w_in, hgrn_lb, hgrn_norm, pool_w, pool_scale, rpb, w_branch_a, w_branch_b, w_branch_c, w_out,
           ffn_w_in, ffn_w_out, router, moe_w_in, moe_w_out, norm_final):
    x = (x_prompt.reshape(N_CTX, D), x_sample.reshape(N_LAT, D))
    cond8 = jnp.concatenate([c_ctx[None], c, jnp.zeros((5, D), F32)], axis=0)
    mods = _adaln(cond8, w_ada, b_ada)[:, :3].reshape(DEPTH, 3, 6, D)
    lbs = _hgrn_lower_bounds(hgrn_lb)
    ck = cache_k.reshape(N_LAT_B, DEPTH, PAST, WIDTH).astype(BF16)
    cv = cache_v.reshape(N_LAT_B, DEPTH, PAST, WIDTH).astype(BF16)
    w_in16 = w_in.astype(BF16)
    router_pad = jnp.pad(router[0], ((0, 0), (0, 128 - N_EXP)))
    r_hi = router_pad.astype(BF16)
    r_lo = (router_pad - r_hi.astype(F32)).astype(BF16)

    ks, vs, ss = [], [], []
    for l in range(DEPTH):
        z, zg, k_ctx, v_ctx, h16 = _inproj(x, norm_mix[l], mods[l], w_in16, l)
        of_c, ob_c, s_ctx = _hgrn(z, zg, lbs[l], None, 0, N_CTX_B, CTX_T)
        of_l, ob_l, _ = _hgrn(z, zg, lbs[l], jnp.swapaxes(state_hgrn[:, l], -1, -2), N_CTX, N_LAT_B, LAT_T)
        o_pool = _pool(z, pool_w[l].astype(BF16), pool_scale[l])
        oc_c = _ctx_attn(z)
        oc_l = _na_attn(z, ck, cv, _na_bias_table(rpb[l]), l)
        merged = _merge((of_c, ob_c, oc_c), (of_l, ob_l, oc_l), z, h16, w_in16, l, o_pool, x, mods[l],
                        hgrn_norm[l], norm_ffn[l],
                        w_branch_a[l].astype(BF16), w_branch_b[l].astype(BF16),
                        w_branch_c[l].astype(BF16), w_out[l].astype(BF16),
                        (r_hi, r_lo) if l % 2 == 1 else None)
        if l % 2 == 0:
            x, h2 = merged
            x = _ffn(h2, x, mods[l], ffn_w_in.astype(BF16), ffn_w_out.astype(BF16))
        else:
            x, h2, route = merged
            y_ctx, y_lat = _moe(h2, route, x, mods[l], moe_w_in, moe_w_out, norm_final)
        ks.append(k_ctx)
        vs.append(v_ctx)
        ss.append(jnp.swapaxes(s_ctx, -1, -2))

    def cache(per_layer):
        return jnp.concatenate([a.reshape(N_CTX_B, 1, CTX_T, NA_H, NA_D) for a in per_layer], axis=1)

    y_prompt = y_ctx.reshape(N_CTX_B, CTX_T, D)
    y_sample = y_lat.reshape(N_LAT_B, LAT_T, D)
    return (y_prompt, y_sample, cache(ks), cache(vs), jnp.stack(ss, axis=1))
```
